```python
import math
import jax
import jax.numpy as jnp
from jax import lax
import numpy as np

D_MODEL = 1024
BATCH = 8
SEQ = 4096
DEPTH = 2

GDN_HEADS = 4
GDN_HEAD_DIM = 128
GDN_WIDTH = GDN_HEADS * GDN_HEAD_DIM
GDN_CONV = 4
GDN_CHUNK = 64
S5_GROUP = 16
S5_GROUPS = 16
S5_WIDTH = S5_GROUPS * S5_GROUP
S5_STATE = 64
MOBA_HEADS = 4
MOBA_HEAD_DIM = 64
MOBA_WIDTH = MOBA_HEADS * MOBA_HEAD_DIM
MOBA_BLOCK = 256
MOBA_TOPK = 3
MOBA_Q_CHUNK = 64
N_BRANCH = 3
IN_SIZES = (3 * GDN_WIDTH, GDN_WIDTH, GDN_HEADS, GDN_HEADS, S5_WIDTH, 3 * MOBA_WIDTH, N_BRANCH * D_MODEL)
IN_COLS = 3 * GDN_WIDTH + GDN_WIDTH + 2 * GDN_HEADS + S5_WIDTH + 3 * MOBA_WIDTH + N_BRANCH * D_MODEL
PEER_HEADS = 8
PEER_NKEYS = 128
PEER_EXPERTS = PEER_NKEYS * PEER_NKEYS
PEER_QDIM = 256
PEER_TOPK = 16
PEER_TOKEN_CHUNK = 128
RMS_EPS = 1e-6
NEG_INF = -1e30

kernel_name = 'hybrid_gdn_s5_moba_peer_adaln'

F32 = jnp.float32


def _rmsnorm(x, w):
    xf = x.astype(F32)
    y = xf * lax.rsqrt(jnp.mean(xf * xf, axis=-1, keepdims=True) + RMS_EPS)
    return (y * w.astype(F32)).astype(x.dtype)


def _l2norm(t):
    return t * lax.rsqrt(jnp.sum(t * t, axis=-1, keepdims=True) + 1e-6)


def _causal_conv(x, w):
    k, ch = w.shape
    return lax.conv_general_dilated(
        x, w[:, None, :].astype(x.dtype), window_strides=(1,), padding=[(k - 1, 0)],
        dimension_numbers=('NWC', 'WIO', 'NWC'), feature_group_count=ch)


def _inv_unit_lower(a):
    n = -a
    eye = jnp.eye(a.shape[-1], dtype=a.dtype)
    t = eye + n
    p = n
    for _ in range(int(math.log2(a.shape[-1])) - 1):
        p = p @ p
        t = t + t @ p
    return t


def _gated_delta_rule(q, k, v, g, beta):
    bt, nh, L, dk = q.shape
    dv = v.shape[-1]
    C = GDN_CHUNK
    nc = L // C
    q, k, v = (t.reshape(bt, nh, nc, C, -1) for t in (q, k, v))
    g = jnp.cumsum(g.reshape(bt, nh, nc, C), axis=-1)
    beta = beta.reshape(bt, nh, nc, C)
    tril = jnp.tril(jnp.ones((C, C), bool))
    strict = jnp.tril(jnp.ones((C, C), bool), -1)
    diff = g[..., :, None] - g[..., None, :]
    decay = jnp.where(tril, jnp.exp(jnp.where(tril, diff, 0.0)), 0.0)
    k_beta = k * beta[..., None]
    a = jnp.where(strict, jnp.einsum('bhnid,bhnjd->bhnij', k_beta, k) * decay, 0.0)
    t_inv = _inv_unit_lower(a)
    u = t_inv @ (v * beta[..., None])
    w = t_inv @ (k_beta * jnp.exp(g)[..., None])
    qk = jnp.where(tril, jnp.einsum('bhnid,bhnjd->bhnij', q, k) * decay, 0.0)

    def step(state, xs):
        q_c, k_c, u_c, w_c, qk_c, g_c = xs
        v_new = u_c - jnp.einsum('bhck,bhkv->bhcv', w_c, state)
        o = (jnp.einsum('bhck,bhkv->bhcv', q_c * jnp.exp(g_c)[..., None], state)
             + jnp.einsum('bhcs,bhsv->bhcv', qk_c, v_new))
        g_last = g_c[..., -1:]
        state = (state * jnp.exp(g_last)[..., None]
                 + jnp.einsum('bhck,bhcv->bhkv', k_c * jnp.exp(g_last - g_c)[..., None], v_new))
        return state, o

    xs = tuple(jnp.moveaxis(t, 2, 0) for t in (q, k, u, w, qk, g))
    state0 = jnp.zeros((bt, nh, dk, dv), F32)
    _, o = lax.scan(step, state0, xs)
    return jnp.moveaxis(o, 0, 2).reshape(bt, nh, L, dv)


def _gdn_branch(qkv, z, b, a, conv_w, a_log, dt_bias, norm_w):
    bt, L, _ = qkv.shape
    qkv = jax.nn.silu(_causal_conv(qkv, conv_w))
    q, k, v = jnp.split(qkv, 3, axis=-1)
    heads = lambda t: t.reshape(bt, L, GDN_HEADS, GDN_HEAD_DIM).transpose(0, 2, 1, 3).astype(F32)
    q = _l2norm(heads(q)) * (GDN_HEAD_DIM ** -0.5)
    k = _l2norm(heads(k))
    v = heads(v)
    beta = jax.nn.sigmoid(b.astype(F32)).transpose(0, 2, 1)
    g = (-jnp.exp(a_log.astype(F32)) * jax.nn.softplus(a.astype(F32) + dt_bias.astype(F32))).transpose(0, 2, 1)
    o = _gated_delta_rule(q, k, v, g, beta).transpose(0, 2, 1, 3)
    zf = z.reshape(bt, L, GDN_HEADS, GDN_HEAD_DIM).astype(F32)
    o = o * lax.rsqrt(jnp.mean(o * o, axis=-1, keepdims=True) + RMS_EPS) * norm_w.astype(F32) * jax.nn.silu(zf)
    return o.reshape(bt, L, GDN_WIDTH).astype(qkv.dtype)


def _s5_branch(u, a_re, a_im, b_re, b_im, c_re, c_im, d, log_dt, glu_w, glu_b):
    bt, L, _ = u.shape
    uf = u.astype(F32)
    ug = uf.reshape(bt, L, S5_GROUPS, S5_GROUP)
    lam = lax.complex(a_re.astype(F32), a_im.astype(F32))
    dt = jnp.exp(log_dt.astype(F32))[:, None]
    a_bar = jnp.exp(lam * dt)
    b_bar = ((a_bar - 1.0) / lam)[..., None] * lax.complex(b_re.astype(F32), b_im.astype(F32))
    bu = lax.complex(jnp.einsum('gph,blgh->lbgp', jnp.real(b_bar), ug),
                     jnp.einsum('gph,blgh->lbgp', jnp.imag(b_bar), ug))
    a_seq = jnp.broadcast_to(a_bar, (L, 1) + a_bar.shape)

    def combine(e1, e2):
        a1, s1 = e1
        a2, s2 = e2
        return a1 * a2, a2 * s1 + s2

    _, states = lax.associative_scan(combine, (a_seq, bu), axis=0)
    y = (jnp.einsum('ghp,lbgp->blgh', c_re.astype(F32), jnp.real(states))
         - jnp.einsum('ghp,lbgp->blgh', c_im.astype(F32), jnp.imag(states)))
    y = y.reshape(bt, L, S5_WIDTH) + d.astype(F32) * uf
    zg = jax.nn.gelu(y, approximate=False)
    zz = zg @ glu_w.astype(F32) + glu_b.astype(F32)
    out = zz[..., :S5_WIDTH] * jax.nn.sigmoid(zz[..., S5_WIDTH:])
    return out.astype(u.dtype)


def _moba_attention(q, k, v):
    bt, nh, L, dh = q.shape
    n_blk = -(-L // MOBA_BLOCK)
    lp = n_blk * MOBA_BLOCK
    pad = [(0, 0), (0, 0), (0, lp - L), (0, 0)]
    q, k, v = (jnp.pad(t, pad).astype(F32) for t in (q, k, v))
    kb = k.reshape(bt, nh, n_blk, MOBA_BLOCK, dh)
    vb = v.reshape(bt, nh, n_blk, MOBA_BLOCK, dh)
    k_mean = jnp.mean(kb, axis=3)
    gate = jnp.einsum('bhld,bhnd->bhln', q, k_mean)
    q_blk = jnp.arange(lp) // MOBA_BLOCK
    past = jnp.arange(n_blk)[None, :] < q_blk[:, None]
    gate = jnp.where(past, gate, NEG_INF)
    n_sel = min(MOBA_TOPK, n_blk)
    _, sel = lax.top_k(gate, n_sel)
    scale = dh ** -0.5
    b_ix = jnp.arange(bt)[:, None, None, None]
    h_ix = jnp.arange(nh)[None, :, None, None]

    def one_chunk(ci):
        start = ci * MOBA_Q_CHUNK
        own = start // MOBA_BLOCK
        qc = lax.dynamic_slice_in_dim(q, start, MOBA_Q_CHUNK, axis=2)
        sc = lax.dynamic_slice_in_dim(sel, start, MOBA_Q_CHUNK, axis=2)
        k_sel = kb[b_ix, h_ix, sc]
        v_sel = vb[b_ix, h_ix, sc]
        s_sel = jnp.einsum('bhqd,bhqnsd->bhqns', qc, k_sel) * scale
        s_sel = jnp.where((sc < own)[..., None], s_sel, NEG_INF)
        k_own = lax.dynamic_index_in_dim(kb, own, axis=2, keepdims=False)
        v_own = lax.dynamic_index_in_dim(vb, own, axis=2, keepdims=False)
        s_own = jnp.einsum('bhqd,bhsd->bhqs', qc, k_own) * scale
        q_pos = start + jnp.arange(MOBA_Q_CHUNK)
        k_pos = own * MOBA_BLOCK + jnp.arange(MOBA_BLOCK)
        s_own = jnp.where(k_pos[None, :] <= q_pos[:, None], s_own, NEG_INF)
        s_all = jnp.concatenate([s_sel.reshape(bt, nh, MOBA_Q_CHUNK, n_sel * MOBA_BLOCK), s_own], axis=-1)
        p = jax.nn.softmax(s_all, axis=-1)
        p_sel = p[..., :n_sel * MOBA_BLOCK].reshape(bt, nh, MOBA_Q_CHUNK, n_sel, MOBA_BLOCK)
        p_own = p[..., n_sel * MOBA_BLOCK:]
        return (jnp.einsum('bhqns,bhqnsd->bhqd', p_sel, v_sel)
                + jnp.einsum('bhqs,bhsd->bhqd', p_own, v_own))

    out = lax.map(one_chunk, jnp.arange(lp // MOBA_Q_CHUNK))
    return jnp.moveaxis(out, 0, 2).reshape(bt, nh, lp, dh)[:, :, :L]


def _moba_branch(qkv):
    bt, L, _ = qkv.shape
    q, k, v = jnp.split(qkv, 3, axis=-1)
    heads = lambda t: t.reshape(bt, L, MOBA_HEADS, MOBA_HEAD_DIM).transpose(0, 2, 1, 3)
    o = _moba_attention(heads(q), heads(k), heads(v))
    return o.transpose(0, 2, 1, 3).reshape(bt, L, MOBA_WIDTH).astype(qkv.dtype)


def _hybrid_mixer(h, w_in, conv_w, a_log, dt_bias, gdn_norm_w,
                  s5_a_re, s5_a_im, s5_b_re, s5_b_im, s5_c_re, s5_c_im, s5_d, s5_log_dt,
                  s5_glu_w, s5_glu_b, w_branch_a, w_branch_b, w_branch_c, w_out):
    bt, L, _ = h.shape
    p = h @ w_in
    cuts = np.cumsum(IN_SIZES)[:-1].tolist()
    qkv_a, z_a, beta_a, alpha_a, u_b, qkv_c, gate_in = jnp.split(p, cuts, axis=-1)
    y_a = _gdn_branch(qkv_a, z_a, beta_a, alpha_a, conv_w, a_log, dt_bias, gdn_norm_w) @ w_branch_a
    y_b = _s5_branch(u_b, s5_a_re, s5_a_im, s5_b_re, s5_b_im, s5_c_re, s5_c_im, s5_d,
                     s5_log_dt, s5_glu_w, s5_glu_b) @ w_branch_b
    y_c = _moba_branch(qkv_c) @ w_branch_c
    gates = jax.nn.sigmoid(gate_in.astype(F32)).astype(h.dtype).reshape(bt, L, N_BRANCH, D_MODEL)
    merged = gates[:, :, 0] * y_a + gates[:, :, 1] * y_b + gates[:, :, 2] * y_c
    return merged @ w_out


def _peer(h, wq, k1, k2, u_tab, v_tab):
    bt, L, d = h.shape
    half = PEER_QDIM // 2
    q = (h @ wq).astype(F32).reshape(bt, L, PEER_HEADS, PEER_QDIM)
    s1 = jnp.einsum('blhd,hnd->blhn', q[..., :half], k1.astype(F32))
    s2 = jnp.einsum('blhd,hnd->blhn', q[..., half:], k2.astype(F32))
    v1, i1 = lax.top_k(s1, PEER_TOPK)
    v2, i2 = lax.top_k(s2, PEER_TOPK)
    n_cand = PEER_TOPK * PEER_TOPK
    cand_s = (v1[..., :, None] + v2[..., None, :]).reshape(bt, L, PEER_HEADS, n_cand)
    cand_i = (i1[..., :, None] * PEER_NKEYS + i2[..., None, :]).reshape(bt, L, PEER_HEADS, n_cand)
    top_s, pos = lax.top_k(cand_s, PEER_TOPK)
    expert = jnp.take_along_axis(cand_i, pos, axis=-1)
    gate = jax.nn.softmax(top_s, axis=-1)
    n_sel = PEER_HEADS * PEER_TOPK
    n_chunk = (bt * L) // PEER_TOKEN_CHUNK
    hs = h.reshape(n_chunk, PEER_TOKEN_CHUNK, d)
    es = expert.reshape(n_chunk, PEER_TOKEN_CHUNK, n_sel)
    gs = gate.reshape(n_chunk, PEER_TOKEN_CHUNK, n_sel)

    def chunk(args):
        hc, ec, gc = args
        act = jax.nn.gelu(jnp.einsum('td,tkd->tk', hc.astype(F32), u_tab[ec].astype(F32)), approximate=False)
        return jnp.einsum('tk,tkd->td', gc * act, v_tab[ec].astype(F32))

    out = lax.map(chunk, (hs, es, gs))
    return out.reshape(bt, L, d).astype(h.dtype)


def setup_inputs(seed: int = 0) -> dict:
    key = jax.random.key(seed)
    ks = iter(jax.random.split(key, 48))
    nrm = lambda shape, s: jax.random.normal(next(ks), shape, F32) * s
    D = D_MODEL
    L = DEPTH
    G, P = S5_GROUPS, S5_STATE
    x = nrm((BATCH, SEQ, D), 1.0)
    c = nrm((BATCH, D), 1.0)
    ada_w = nrm((L, D, 6 * D), 0.5 * D ** -0.5)
    ada_b = nrm((L, 6 * D), 0.02)
    norm1_w = 1.0 + nrm((L, D), 0.02)
    w_in = nrm((L, D, IN_COLS), D ** -0.5)
    gdn_conv_w = nrm((L, GDN_CONV, 3 * GDN_WIDTH), GDN_CONV ** -0.5)
    gdn_a_log = jnp.log(jax.random.uniform(next(ks), (L, GDN_HEADS), F32, 1.0, 16.0))
    dt0 = jnp.exp(jax.random.uniform(next(ks), (L, GDN_HEADS), F32, math.log(1e-3), math.log(1e-1)))
    gdn_dt_bias = dt0 + jnp.log(-jnp.expm1(-dt0))
    gdn_norm_w = 1.0 + nrm((L, GDN_HEAD_DIM), 0.02)
    s5_a_re = -0.5 * jnp.exp(nrm((L, G, P), 0.02))
    s5_a_im = jnp.pi * jnp.arange(P, dtype=F32) + nrm((L, G, P), 0.02)
    s5_b_re = nrm((L, G, P, S5_GROUP), (2 * S5_GROUP) ** -0.5)
    s5_b_im = nrm((L, G, P, S5_GROUP), (2 * S5_GROUP) ** -0.5)
    s5_c_re = nrm((L, G, S5_GROUP, P), (2 * P) ** -0.5)
    s5_c_im = nrm((L, G, S5_GROUP, P), (2 * P) ** -0.5)
    s5_d = nrm((L, S5_WIDTH), 1.0)
    s5_log_dt = jax.random.uniform(next(ks), (L, G), F32, math.log(1e-3), math.log(1e-1))
    s5_glu_w = nrm((L, S5_WIDTH, 2 * S5_WIDTH), S5_WIDTH ** -0.5)
    s5_glu_b = nrm((L, 2 * S5_WIDTH), 0.02)
    w_branch_a = nrm((L, GDN_WIDTH, D), GDN_WIDTH ** -0.5)
    w_branch_b = nrm((L, S5_WIDTH, D), S5_WIDTH ** -0.5)
    w_branch_c = nrm((L, MOBA_WIDTH, D), MOBA_WIDTH ** -0.5)
    w_out = nrm((L, D, D), D ** -0.5)
    norm2_w = 1.0 + nrm((L, D), 0.02)
    peer_wq = nrm((L, D, PEER_HEADS * PEER_QDIM), D ** -0.5)
    peer_k1 = nrm((L, PEER_HEADS, PEER_NKEYS, PEER_QDIM // 2), (PEER_QDIM // 2) ** -0.5)
    peer_k2 = nrm((L, PEER_HEADS, PEER_NKEYS, PEER_QDIM // 2), (PEER_QDIM // 2) ** -0.5)
    peer_u = nrm((L, PEER_EXPERTS, D), D ** -0.5)
    peer_v = nrm((L, PEER_EXPERTS, D), 0.5)
    final_norm_w = 1.0 + nrm((D,), 0.02)
    return {'x': x, 'c': c, 'ada_w': ada_w, 'ada_b': ada_b, 'norm1_w': norm1_w, 'w_in': w_in,
            'gdn_conv_w': gdn_conv_w, 'gdn_a_log': gdn_a_log, 'gdn_dt_bias': gdn_dt_bias,
            'gdn_norm_w': gdn_norm_w, 's5_a_re': s5_a_re, 's5_a_im': s5_a_im, 's5_b_re': s5_b_re,
            's5_b_im': s5_b_im, 's5_c_re': s5_c_re, 's5_c_im': s5_c_im, 's5_d': s5_d,
            's5_log_dt': s5_log_dt, 's5_glu_w': s5_glu_w, 's5_glu_b': s5_glu_b,
            'w_branch_a': w_branch_a, 'w_branch_b': w_branch_b, 'w_branch_c': w_branch_c,
            'w_out': w_out, 'norm2_w': norm2_w, 'peer_wq': peer_wq, 'peer_k1': peer_k1,
            'peer_k2': peer_k2, 'peer_u': peer_u, 'peer_v': peer_v, 'final_norm_w': final_norm_w}


def reference(x, c, ada_w, ada_b, norm1_w, w_in, gdn_conv_w, gdn_a_log, gdn_dt_bias, gdn_norm_w,
              s5_a_re, s5_a_im, s5_b_re, s5_b_im, s5_c_re, s5_c_im, s5_d, s5_log_dt, s5_glu_w,
              s5_glu_b, w_branch_a, w_branch_b, w_branch_c, w_out, norm2_w, peer_wq, peer_k1,
              peer_k2, peer_u, peer_v, final_norm_w):
    for l in range(DEPTH):
        mod = jax.nn.silu(c) @ ada_w[l] + ada_b[l]
        sh1, sc1, g1, sh2, sc2, g2 = jnp.split(mod[:, None, :], 6, axis=-1)
        h = _rmsnorm(x, norm1_w[l]) * (1.0 + sc1) + sh1
        x = x + g1 * _hybrid_mixer(h, w_in[l], gdn_conv_w[l], gdn_a_log[l], gdn_dt_bias[l], gdn_norm_w[l],
                                   s5_a_re[l], s5_a_im[l], s5_b_re[l], s5_b_im[l], s5_c_re[l], s5_c_im[l],
                                   s5_d[l], s5_log_dt[l], s5_glu_w[l], s5_glu_b[l],
                                   w_branch_a[l], w_branch_b[l], w_branch_c[l], w_out[l])
        h = _rmsnorm(x, norm2_w[l]) * (1.0 + sc2) + sh2
        x = x + g2 * _peer(h, peer_wq[l], peer_k1[l], peer_k2[l], peer_u[l], peer_v[l])
    return _rmsnorm(x, final_norm_w)
```

```python
import functools
import math

import jax
import jax.numpy as jnp
import numpy as np
from jax import lax
from jax.experimental import pallas as pl
from jax.experimental.pallas import tpu as pltpu

F32 = jnp.float32
BF16 = jnp.bfloat16
HIGHEST = lax.Precision.HIGHEST

D_MODEL = 1024
GDN_HEADS = 4
GDN_HEAD_DIM = 128
GDN_WIDTH = GDN_HEADS * GDN_HEAD_DIM
GDN_CONV = 4
GDN_CHUNK = 64
S5_GROUP = 16
S5_GROUPS = 16
S5_WIDTH = S5_GROUPS * S5_GROUP
S5_STATE = 64
MOBA_HEADS = 4
MOBA_HEAD_DIM = 64
MOBA_WIDTH = MOBA_HEADS * MOBA_HEAD_DIM
MOBA_BLOCK = 256
MOBA_TOPK = 3
MOBA_Q_CHUNK = 64
N_BRANCH = 3
PEER_HEADS = 8
PEER_NKEYS = 128
PEER_QDIM = 256
PEER_TOPK = 16
PEER_SEL = PEER_HEADS * PEER_TOPK
RMS_EPS = 1e-6
NEG_INF = -1e30

SUBLANES = 8
LANES = 128
VMEM_LIMIT_BYTES = 56 * 1024 * 1024

_OFF_QKV_A = 0
_OFF_Z_A = 3 * GDN_WIDTH
_OFF_BETA = _OFF_Z_A + GDN_WIDTH
_OFF_ALPHA = _OFF_BETA + GDN_HEADS
_OFF_UB = _OFF_ALPHA + GDN_HEADS
_OFF_QKV_C = _OFF_UB + S5_WIDTH
_OFF_GATE = _OFF_QKV_C + 3 * MOBA_WIDTH
_IN_COLS = _OFF_GATE + N_BRANCH * D_MODEL


def _cparams(*sem):
    return pltpu.CompilerParams(dimension_semantics=sem, vmem_limit_bytes=VMEM_LIMIT_BYTES)


def _resident(shape):
    nd = len(shape)
    return pl.BlockSpec(shape, lambda *_: (0,) * nd)


def _ada_kernel(c_ref, w_ref, b_ref, o_ref):
    c = c_ref[...]
    sc = c * jax.nn.sigmoid(c)
    o_ref[...] = jnp.dot(sc, w_ref[...], precision=HIGHEST, preferred_element_type=F32) + b_ref[...]


def _ada_call(c, ada_w, ada_b):
    depth, d, d6 = ada_w.shape
    bt = c.shape[0]
    nj = d6 // d
    return pl.pallas_call(
        _ada_kernel,
        grid=(depth, nj),
        in_specs=[
            pl.BlockSpec((bt, d), lambda l, j: (0, 0)),
            pl.BlockSpec((None, d, d), lambda l, j: (l, 0, j)),
            pl.BlockSpec((None, 1, d), lambda l, j: (l, 0, j)),
        ],
        out_specs=pl.BlockSpec((None, bt, d), lambda l, j: (l, 0, j)),
        out_shape=jax.ShapeDtypeStruct((depth, bt, d6), F32),
        compiler_params=_cparams("parallel", "parallel"),
        name="ada_mod",
    )(c, ada_w, ada_b.reshape(depth, 1, d6))


def _norm_mod(x, nw, sc, sh):
    y = x * lax.rsqrt(jnp.mean(x * x, axis=-1, keepdims=True) + RMS_EPS)
    return (y * nw) * (1.0 + sc) + sh


_IN_SPLITS = (4 * GDN_WIDTH, S5_WIDTH, 3 * MOBA_WIDTH, N_BRANCH * D_MODEL, LANES)


def _in_proj_kernel(x_ref, nw_ref, sc_ref, sh_ref, w_ref, oa_ref, ob_ref, oc_ref, og_ref, oba_ref):
    h = _norm_mod(x_ref[...], nw_ref[...], sc_ref[...], sh_ref[...]).astype(BF16)
    off = 0
    for o_ref, width in zip((oa_ref, ob_ref, oc_ref, og_ref, oba_ref), _IN_SPLITS):
        o_ref[...] = jnp.dot(h, w_ref[:, off:off + width], preferred_element_type=F32)
        off += width


def _in_proj_call(x2d, nw, sc, sh, w_r, seq, tm=256):
    n, d = x2d.shape
    per_b = seq // tm
    wcols = w_r.shape[1]
    outs = [jax.ShapeDtypeStruct((n, wd), F32) for wd in _IN_SPLITS]
    return pl.pallas_call(
        _in_proj_kernel,
        grid=(n // tm,),
        in_specs=[
            pl.BlockSpec((tm, d), lambda i: (i, 0)),
            _resident((1, d)),
            pl.BlockSpec((None, 1, d), lambda i: (i // per_b, 0, 0)),
            pl.BlockSpec((None, 1, d), lambda i: (i // per_b, 0, 0)),
            _resident((d, wcols)),
        ],
        out_specs=[pl.BlockSpec((tm, wd), lambda i: (i, 0)) for wd in _IN_SPLITS],
        out_shape=outs,
        compiler_params=_cparams("parallel"),
        name="in_proj",
    )(x2d, nw, sc, sh, w_r)


def _arrange_w_in(w_in_l):
    pad = jnp.zeros((w_in_l.shape[0], LANES - 2 * GDN_HEADS), w_in_l.dtype)
    return jnp.concatenate([
        w_in_l[:, _OFF_QKV_A:_OFF_BETA],
        w_in_l[:, _OFF_UB:_OFF_QKV_C],
        w_in_l[:, _OFF_QKV_C:_OFF_GATE],
        w_in_l[:, _OFF_GATE:_IN_COLS],
        w_in_l[:, _OFF_BETA:_OFF_UB], pad,
    ], axis=1).astype(BF16)


def _merge_kernel(oa_ref, ob_ref, oc_ref, gt_ref, x_ref, g1_ref, wa_ref, wb_ref, wc_ref, wo_ref, o_ref):
    d = D_MODEL
    ya = jnp.dot(oa_ref[...].astype(BF16), wa_ref[...], preferred_element_type=F32)
    yb = jnp.dot(ob_ref[...].astype(BF16), wb_ref[...], preferred_element_type=F32)
    yc = jnp.dot(oc_ref[...].astype(BF16), wc_ref[...], preferred_element_type=F32)
    merged = (jax.nn.sigmoid(gt_ref[:, 0:d]) * ya + jax.nn.sigmoid(gt_ref[:, d:2 * d]) * yb
              + jax.nn.sigmoid(gt_ref[:, 2 * d:3 * d]) * yc)
    y = jnp.dot(merged.astype(BF16), wo_ref[...], preferred_element_type=F32)
    o_ref[...] = x_ref[...] + g1_ref[...] * y


def _merge_call(o_a, o_b, o_c, p_gate, x2d, g1, wa, wb, wc, wo, seq, tm=512):
    n, d = x2d.shape
    per_b = seq // tm
    row = lambda wd: pl.BlockSpec((tm, wd), lambda i: (i, 0))
    return pl.pallas_call(
        _merge_kernel,
        grid=(n // tm,),
        in_specs=[row(GDN_WIDTH), row(S5_WIDTH), row(MOBA_WIDTH), row(N_BRANCH * d), row(d),
                  pl.BlockSpec((None, 1, d), lambda i: (i // per_b, 0, 0)),
                  _resident(wa.shape), _resident(wb.shape), _resident(wc.shape), _resident(wo.shape)],
        out_specs=row(d),
        out_shape=jax.ShapeDtypeStruct((n, d), F32),
        compiler_params=_cparams("parallel"),
        name="merge_out",
    )(o_a, o_b, o_c, p_gate, x2d, g1, wa, wb, wc, wo)


def _topk_rows(s_ref, n_rows, vals_ref, idx_ref, payload_ref=None):
    shape = s_ref.shape
    iota = lax.broadcasted_iota(jnp.int32, shape, 0).astype(F32)

    def body(k, carry):
        s = s_ref[...]
        m = jnp.max(s, axis=0, keepdims=True)
        pos = jnp.min(jnp.where(s == m, iota, float(n_rows)), axis=0, keepdims=True)
        hit = iota == pos
        vals_ref[pl.ds(k, 1), :] = m
        if payload_ref is None:
            idx_ref[pl.ds(k, 1), :] = pos
        else:
            idx_ref[pl.ds(k, 1), :] = jnp.sum(jnp.where(hit, payload_ref[...], 0.0), axis=0, keepdims=True)
        s_ref[...] = jnp.where(hit, -jnp.inf, s)
        return carry

    lax.fori_loop(0, PEER_TOPK, body, 0)


def _peer_query_kernel(x_ref, nw_ref, sc_ref, sh_ref, wqt_ref, k1_ref, k2_ref, h_ref, idx_ref, gate_ref,
                       q_scr, s_scr, v1_scr, i1_scr, v2_scr, i2_scr, cs_scr, ci_scr, ts_scr, te_scr):
    h = _norm_mod(x_ref[...], nw_ref[...], sc_ref[...], sh_ref[...])
    h_ref[...] = h
    q_scr[...] = lax.dot_general(wqt_ref[...], h.astype(BF16), (((1,), (1,)), ((), ())),
                                 preferred_element_type=F32)
    half = PEER_QDIM // 2

    def head_body(hd, carry):
        base = pl.multiple_of(hd * PEER_QDIM, PEER_QDIM)
        s_scr[...] = jnp.dot(k1_ref[hd], q_scr[pl.ds(base, half), :], precision=HIGHEST,
                             preferred_element_type=F32)
        _topk_rows(s_scr, PEER_NKEYS, v1_scr, i1_scr)
        s_scr[...] = jnp.dot(k2_ref[hd], q_scr[pl.ds(base + half, half), :], precision=HIGHEST,
                             preferred_element_type=F32)
        _topk_rows(s_scr, PEER_NKEYS, v2_scr, i2_scr)
        v2 = v2_scr[...]
        i2 = i2_scr[...]
        for a in range(PEER_TOPK):
            cs_scr[a * PEER_TOPK:(a + 1) * PEER_TOPK, :] = v1_scr[a:a + 1, :] + v2
            ci_scr[a * PEER_TOPK:(a + 1) * PEER_TOPK, :] = i1_scr[a:a + 1, :] * float(PEER_NKEYS) + i2
        _topk_rows(cs_scr, PEER_TOPK * PEER_TOPK, ts_scr, te_scr, payload_ref=ci_scr)
        ts = ts_scr[...]
        e = jnp.exp(ts - jnp.max(ts, axis=0, keepdims=True))
        out_base = pl.multiple_of(hd * PEER_TOPK, PEER_TOPK)
        gate_ref[pl.ds(out_base, PEER_TOPK), :] = e / jnp.sum(e, axis=0, keepdims=True)
        idx_ref[pl.ds(out_base, PEER_TOPK), :] = te_scr[...].astype(jnp.int32)
        return carry

    lax.fori_loop(0, PEER_HEADS, head_body, 0)


def _peer_query_call(x2d, nw, sc, sh, wqt, k1, k2, seq, tm=256):
    n, d = x2d.shape
    per_b = seq // tm
    nq = PEER_HEADS * PEER_QDIM
    k = PEER_TOPK
    scratch = [pltpu.VMEM((nq, tm), F32), pltpu.VMEM((PEER_NKEYS, tm), F32),
               pltpu.VMEM((k, tm), F32), pltpu.VMEM((k, tm), F32), pltpu.VMEM((k, tm), F32), pltpu.VMEM((k, tm), F32),
               pltpu.VMEM((k * k, tm), F32), pltpu.VMEM((k * k, tm), F32),
               pltpu.VMEM((k, tm), F32), pltpu.VMEM((k, tm), F32)]
    return pl.pallas_call(
        _peer_query_kernel,
        grid=(n // tm,),
        in_specs=[
            pl.BlockSpec((tm, d), lambda i: (i, 0)),
            _resident((1, d)),
            pl.BlockSpec((None, 1, d), lambda i: (i // per_b, 0, 0)),
            pl.BlockSpec((None, 1, d), lambda i: (i // per_b, 0, 0)),
            _resident(wqt.shape), _resident(k1.shape), _resident(k2.shape),
        ],
        out_specs=[pl.BlockSpec((tm, d), lambda i: (i, 0)),
                   pl.BlockSpec((PEER_SEL, tm), lambda i: (0, i)),
                   pl.BlockSpec((PEER_SEL, tm), lambda i: (0, i))],
        out_shape=[jax.ShapeDtypeStruct((n, d), F32),
                   jax.ShapeDtypeStruct((PEER_SEL, n), jnp.int32),
                   jax.ShapeDtypeStruct((PEER_SEL, n), F32)],
        scratch_shapes=scratch,
        compiler_params=_cparams("parallel"),
        name="peer_query",
    )(x2d, nw, sc, sh, wqt, k1, k2)


PEER_TOKENS_PER_STEP = 8
_ROW_SUB = 2 * SUBLANES


def _fold_pair(a, b, shift, keep_a):
    fa = a + pltpu.roll(a, shift, axis=0)
    fb = b + pltpu.roll(b, SUBLANES - shift, axis=0)
    return jnp.where(keep_a, fa, fb)


def _slot_order():
    tiles = [[k] * SUBLANES for k in range(PEER_SEL)]
    for shift, keep in ((4, [j < 4 for j in range(8)]), (2, [(j % 4) >= 2 for j in range(8)]),
                        (1, [(j % 2) == 1 for j in range(8)])):
        tiles = [[tiles[2 * m][j] if keep[j] else tiles[2 * m + 1][j] for j in range(8)]
                 for m in range(len(tiles) // 2)]
    return [s for t in tiles for s in t]


_SLOT_AT = _slot_order()
_POS_OF = [0] * PEER_SEL
for _p, _s in enumerate(_SLOT_AT):
    _POS_OF[_s] = _p


def _peer_gather_kernel(idx0_ref, idxn_ref, g_ref, h_ref, x_ref, g2_ref, uv_hbm, o_ref, buf, wb_scr, sems):
    tb = PEER_TOKENS_PER_STEP
    i = pl.program_id(0)
    nb = pl.num_programs(0)
    cur = (i % 2) * tb
    nxt = tb - cur

    def row_copy(e, row, slot):
        return pltpu.make_async_copy(uv_hbm.at[e], buf.at[row], sems.at[slot])

    def slot_copy(slot):
        return pltpu.make_async_copy(uv_hbm.at[pl.ds(0, PEER_SEL)], buf.at[pl.ds(slot * PEER_SEL, PEER_SEL)],
                                     sems.at[slot])

    @pl.when(i == 0)
    def _():
        def issue0(t, c):
            for k in range(PEER_SEL):
                row_copy(idx0_ref[0, t * PEER_SEL + k], t * PEER_SEL + k, t).start()
            return c
        lax.fori_loop(0, tb, issue0, 0)

    sub = lax.broadcasted_iota(jnp.int32, (SUBLANES, LANES), 0)
    keep4 = sub < 4
    keep2 = (sub % 4) >= 2
    keep1 = (sub % 2) == 1
    eye = (lax.broadcasted_iota(jnp.int32, (PEER_SEL, LANES), 0)
           == lax.broadcasted_iota(jnp.int32, (PEER_SEL, LANES), 1))

    def token(t, c):
        slot = cur + t
        nslot = nxt + t
        slot_copy(slot).wait()
        for k in range(PEER_SEL):
            row_copy(idxn_ref[0, t * PEER_SEL + k], nslot * PEER_SEL + k, nslot).start()
        base = slot * PEER_SEL
        ht = h_ref[t]
        tiles = [buf[base + k, 0:SUBLANES, :] * ht for k in range(PEER_SEL)]
        tiles = [_fold_pair(tiles[2 * m], tiles[2 * m + 1], 4, keep4) for m in range(PEER_SEL // 2)]
        tiles = [_fold_pair(tiles[2 * m], tiles[2 * m + 1], 2, keep2) for m in range(PEER_SEL // 4)]
        tiles = [_fold_pair(tiles[2 * m], tiles[2 * m + 1], 1, keep1) for m in range(PEER_SEL // 8)]
        part = jnp.concatenate(tiles, axis=0)
        s = jnp.sum(part, axis=-1, keepdims=True)
        act = 0.5 * s * (1.0 + lax.erf(s * (2.0 ** -0.5)))
        grow = jnp.broadcast_to(g_ref[pl.ds(t, 1), :], (PEER_SEL, LANES))
        gcol = jnp.sum(jnp.where(eye, grow, 0.0), axis=-1, keepdims=True)
        wb_scr[...] = jnp.broadcast_to(gcol * act, (PEER_SEL, LANES))
        accs = [jnp.zeros((SUBLANES, LANES), F32) for _ in range(4)]
        for k in range(PEER_SEL):
            p = _POS_OF[k]
            wk = jnp.broadcast_to(wb_scr[p:p + 1, :], (SUBLANES, LANES))
            accs[k % 4] = accs[k % 4] + wk * buf[base + k, SUBLANES:_ROW_SUB, :]
        y = (accs[0] + accs[1]) + (accs[2] + accs[3])
        o_ref[t] = x_ref[t] + g2_ref[...] * y
        return c

    lax.fori_loop(0, tb, token, 0)

    @pl.when(i == nb - 1)
    def _():
        for t in range(tb):
            slot_copy(nxt + t).wait()


def _peer_gather_call(idx, gate_p, h3, x3, g2_3, uv, seq):
    n = idx.shape[0]
    tb = PEER_TOKENS_PER_STEP
    nb = n // tb
    per_b = seq // tb
    idx3 = idx.reshape(nb, 1, tb * PEER_SEL)
    smem_blk = lambda fn: pl.BlockSpec((None, 1, tb * PEER_SEL), fn, memory_space=pltpu.SMEM)
    tok3 = pl.BlockSpec((tb, SUBLANES, LANES), lambda i: (i, 0, 0))
    return pl.pallas_call(
        _peer_gather_kernel,
        grid=(nb,),
        in_specs=[
            smem_blk(lambda i: (0, 0, 0)),
            smem_blk(lambda i: (jnp.minimum(i + 1, nb - 1), 0, 0)),
            pl.BlockSpec((tb, PEER_SEL), lambda i: (i, 0)),
            tok3, tok3,
            pl.BlockSpec((None, SUBLANES, LANES), lambda i: (i // per_b, 0, 0)),
            pl.BlockSpec(memory_space=pl.ANY),
        ],
        out_specs=tok3,
        out_shape=jax.ShapeDtypeStruct((n, SUBLANES, LANES), F32),
        scratch_shapes=[pltpu.VMEM((2 * tb * PEER_SEL, _ROW_SUB, LANES), F32),
                        pltpu.VMEM((PEER_SEL, LANES), F32),
                        pltpu.SemaphoreType.DMA((2 * tb,))],
        compiler_params=_cparams("arbitrary"),
        name="peer_gather",
    )(idx3, idx3, gate_p, h3, x3, g2_3, uv)


def _final_norm_kernel(x_ref, w_ref, o_ref):
    x = x_ref[...]
    o_ref[...] = x * lax.rsqrt(jnp.mean(x * x, axis=-1, keepdims=True) + RMS_EPS) * w_ref[...]


def _final_norm_call(x2d, w, tm=1024):
    n, d = x2d.shape
    return pl.pallas_call(
        _final_norm_kernel,
        grid=(n // tm,),
        in_specs=[pl.BlockSpec((tm, d), lambda i: (i, 0)), _resident((1, d))],
        out_specs=pl.BlockSpec((tm, d), lambda i: (i, 0)),
        out_shape=jax.ShapeDtypeStruct((n, d), F32),
        compiler_params=_cparams("parallel"),
        name="final_norm",
    )(x2d, w)


def _l2norm(t):
    return t * lax.rsqrt(jnp.sum(t * t, axis=-1, keepdims=True) + 1e-6)


def _causal_conv(x, w):
    k, ch = w.shape
    return lax.conv_general_dilated(
        x, w[:, None, :].astype(x.dtype), window_strides=(1,), padding=[(k - 1, 0)],
        dimension_numbers=('NWC', 'WIO', 'NWC'), feature_group_count=ch)


def _inv_unit_lower(a):
    n = -a
    eye = jnp.eye(a.shape[-1], dtype=a.dtype)
    t = eye + n
    p = n
    for _ in range(int(math.log2(a.shape[-1])) - 1):
        p = p @ p
        t = t + t @ p
    return t


def _gated_delta_rule(q, k, v, g, beta):
    bt, nh, L, dk = q.shape
    dv = v.shape[-1]
    C = GDN_CHUNK
    nc = L // C
    q, k, v = (t.reshape(bt, nh, nc, C, -1) for t in (q, k, v))
    g = jnp.cumsum(g.reshape(bt, nh, nc, C), axis=-1)
    beta = beta.reshape(bt, nh, nc, C)
    tril = jnp.tril(jnp.ones((C, C), bool))
    strict = jnp.tril(jnp.ones((C, C), bool), -1)
    diff = g[..., :, None] - g[..., None, :]
    decay = jnp.where(tril, jnp.exp(jnp.where(tril, diff, 0.0)), 0.0)
    k_beta = k * beta[..., None]
    a = jnp.where(strict, jnp.einsum('bhnid,bhnjd->bhnij', k_beta, k) * decay, 0.0)
    t_inv = _inv_unit_lower(a)
    u = t_inv @ (v * beta[..., None])
    w = t_inv @ (k_beta * jnp.exp(g)[..., None])
    qk = jnp.where(tril, jnp.einsum('bhnid,bhnjd->bhnij', q, k) * decay, 0.0)

    def step(state, xs):
        q_c, k_c, u_c, w_c, qk_c, g_c = xs
        v_new = u_c - jnp.einsum('bhck,bhkv->bhcv', w_c, state)
        o = (jnp.einsum('bhck,bhkv->bhcv', q_c * jnp.exp(g_c)[..., None], state)
             + jnp.einsum('bhcs,bhsv->bhcv', qk_c, v_new))
        g_last = g_c[..., -1:]
        state = (state * jnp.exp(g_last)[..., None]
                 + jnp.einsum('bhck,bhcv->bhkv', k_c * jnp.exp(g_last - g_c)[..., None], v_new))
        return state, o

    xs = tuple(jnp.moveaxis(t, 2, 0) for t in (q, k, u, w, qk, g))
    state0 = jnp.zeros((bt, nh, dk, dv), F32)
    _, o = lax.scan(step, state0, xs)
    return jnp.moveaxis(o, 0, 2).reshape(bt, nh, L, dv)


def _gdn_branch(qkv, z, b, a, conv_w, a_log, dt_bias, norm_w):
    bt, L, _ = qkv.shape
    qkv = jax.nn.silu(_causal_conv(qkv, conv_w))
    q, k, v = jnp.split(qkv, 3, axis=-1)
    heads = lambda t: t.reshape(bt, L, GDN_HEADS, GDN_HEAD_DIM).transpose(0, 2, 1, 3).astype(F32)
    q = _l2norm(heads(q)) * (GDN_HEAD_DIM ** -0.5)
    k = _l2norm(heads(k))
    v = heads(v)
    beta = jax.nn.sigmoid(b.astype(F32)).transpose(0, 2, 1)
    g = (-jnp.exp(a_log.astype(F32)) * jax.nn.softplus(a.astype(F32) + dt_bias.astype(F32))).transpose(0, 2, 1)
    o = _gated_delta_rule(q, k, v, g, beta).transpose(0, 2, 1, 3)
    zf = z.reshape(bt, L, GDN_HEADS, GDN_HEAD_DIM).astype(F32)
    o = o * lax.rsqrt(jnp.mean(o * o, axis=-1, keepdims=True) + RMS_EPS) * norm_w.astype(F32) * jax.nn.silu(zf)
    return o.reshape(bt, L, GDN_WIDTH).astype(qkv.dtype)


def _s5_branch(u, a_re, a_im, b_re, b_im, c_re, c_im, d, log_dt, glu_w, glu_b):
    bt, L, _ = u.shape
    uf = u.astype(F32)
    ug = uf.reshape(bt, L, S5_GROUPS, S5_GROUP)
    lam = lax.complex(a_re.astype(F32), a_im.astype(F32))
    dt = jnp.exp(log_dt.astype(F32))[:, None]
    a_bar = jnp.exp(lam * dt)
    b_bar = ((a_bar - 1.0) / lam)[..., None] * lax.complex(b_re.astype(F32), b_im.astype(F32))
    bu = lax.complex(jnp.einsum('gph,blgh->lbgp', jnp.real(b_bar), ug),
                     jnp.einsum('gph,blgh->lbgp', jnp.imag(b_bar), ug))
    a_seq = jnp.broadcast_to(a_bar, (L, 1) + a_bar.shape)

    def combine(e1, e2):
        a1, s1 = e1
        a2, s2 = e2
        return a1 * a2, a2 * s1 + s2

    _, states = lax.associative_scan(combine, (a_seq, bu), axis=0)
    y = (jnp.einsum('ghp,lbgp->blgh', c_re.astype(F32), jnp.real(states))
         - jnp.einsum('ghp,lbgp->blgh', c_im.astype(F32), jnp.imag(states)))
    y = y.reshape(bt, L, S5_WIDTH) + d.astype(F32) * uf
    zg = jax.nn.gelu(y, approximate=False)
    zz = zg @ glu_w.astype(F32) + glu_b.astype(F32)
    out = zz[..., :S5_WIDTH] * jax.nn.sigmoid(zz[..., S5_WIDTH:])
    return out.astype(u.dtype)


def _moba_attention(q, k, v):
    bt, nh, L, dh = q.shape
    n_blk = -(-L // MOBA_BLOCK)
    lp = n_blk * MOBA_BLOCK
    pad = [(0, 0), (0, 0), (0, lp - L), (0, 0)]
    q, k, v = (jnp.pad(t, pad).astype(F32) for t in (q, k, v))
    kb = k.reshape(bt, nh, n_blk, MOBA_BLOCK, dh)
    vb = v.reshape(bt, nh, n_blk, MOBA_BLOCK, dh)
    k_mean = jnp.mean(kb, axis=3)
    gate = jnp.einsum('bhld,bhnd->bhln', q, k_mean)
    q_blk = jnp.arange(lp) // MOBA_BLOCK
    past = jnp.arange(n_blk)[None, :] < q_blk[:, None]
    gate = jnp.where(past, gate, NEG_INF)
    n_sel = min(MOBA_TOPK, n_blk)
    _, sel = lax.top_k(gate, n_sel)
    scale = dh ** -0.5
    b_ix = jnp.arange(bt)[:, None, None, None]
    h_ix = jnp.arange(nh)[None, :, None, None]

    def one_chunk(ci):
        start = ci * MOBA_Q_CHUNK
        own = start // MOBA_BLOCK
        qc = lax.dynamic_slice_in_dim(q, start, MOBA_Q_CHUNK, axis=2)
        sc = lax.dynamic_slice_in_dim(sel, start, MOBA_Q_CHUNK, axis=2)
        k_sel = kb[b_ix, h_ix, sc]
        v_sel = vb[b_ix, h_ix, sc]
        s_sel = jnp.einsum('bhqd,bhqnsd->bhqns', qc, k_sel) * scale
        s_sel = jnp.where((sc < own)[..., None], s_sel, NEG_INF)
        k_own = lax.dynamic_index_in_dim(kb, own, axis=2, keepdims=False)
        v_own = lax.dynamic_index_in_dim(vb, own, axis=2, keepdims=False)
        s_own = jnp.einsum('bhqd,bhsd->bhqs', qc, k_own) * scale
        q_pos = start + jnp.arange(MOBA_Q_CHUNK)
        k_pos = own * MOBA_BLOCK + jnp.arange(MOBA_BLOCK)
        s_own = jnp.where(k_pos[None, :] <= q_pos[:, None], s_own, NEG_INF)
        s_all = jnp.concatenate([s_sel.reshape(bt, nh, MOBA_Q_CHUNK, n_sel * MOBA_BLOCK), s_own], axis=-1)
        p = jax.nn.softmax(s_all, axis=-1)
        p_sel = p[..., :n_sel * MOBA_BLOCK].reshape(bt, nh, MOBA_Q_CHUNK, n_sel, MOBA_BLOCK)
        p_own = p[..., n_sel * MOBA_BLOCK:]
        return (jnp.einsum('bhqns,bhqnsd->bhqd', p_sel, v_sel)
                + jnp.einsum('bhqs,bhsd->bhqd', p_own, v_own))

    out = lax.map(one_chunk, jnp.arange(lp // MOBA_Q_CHUNK))
    return jnp.moveaxis(out, 0, 2).reshape(bt, nh, lp, dh)[:, :, :L]


def _moba_branch(qkv):
    bt, L, _ = qkv.shape
    q, k, v = jnp.split(qkv, 3, axis=-1)
    heads = lambda t: t.reshape(bt, L, MOBA_HEADS, MOBA_HEAD_DIM).transpose(0, 2, 1, 3)
    o = _moba_attention(heads(q), heads(k), heads(v))
    return o.transpose(0, 2, 1, 3).reshape(bt, L, MOBA_WIDTH).astype(qkv.dtype)


def kernel(x, c, ada_w, ada_b, norm1_w, w_in, gdn_conv_w, gdn_a_log, gdn_dt_bias, gdn_norm_w, s5_a_re, s5_a_im, s5_b_re, s5_b_im, s5_c_re, s5_c_im, s5_d, s5_log_dt, s5_glu_w, s5_glu_b, w_branch_a, w_branch_b, w_branch_c, w_out, norm2_w, peer_wq, peer_k1, peer_k2, peer_u, peer_v, final_norm_w):
    bt, seq, d = x.shape
    n = bt * seq
    depth = ada_w.shape[0]
    x2d = x.reshape(n, d)
    mod = _ada_call(c, ada_w, ada_b)
    gate_perm = np.asarray(_SLOT_AT, np.int32)
    for l in range(depth):
        sh1, sc1, g1, sh2, sc2, g2 = (mod[l, :, j * d:(j + 1) * d].reshape(bt, 1, d) for j in range(6))
        p_a, p_ub, p_c, p_gate, p_ba = _in_proj_call(
            x2d, norm1_w[l].reshape(1, d), sc1, sh1, _arrange_w_in(w_in[l]), seq)
        qkv_a = p_a[:, :3 * GDN_WIDTH].reshape(bt, seq, -1)
        z_a = p_a[:, 3 * GDN_WIDTH:].reshape(bt, seq, -1)
        beta_a = p_ba[:, :GDN_HEADS].reshape(bt, seq, -1)
        alpha_a = p_ba[:, GDN_HEADS:2 * GDN_HEADS].reshape(bt, seq, -1)
        o_a = _gdn_branch(qkv_a, z_a, beta_a, alpha_a, gdn_conv_w[l], gdn_a_log[l], gdn_dt_bias[l],
                          gdn_norm_w[l]).reshape(n, GDN_WIDTH)
        o_b = _s5_branch(p_ub.reshape(bt, seq, -1), s5_a_re[l], s5_a_im[l], s5_b_re[l], s5_b_im[l], s5_c_re[l],
                         s5_c_im[l], s5_d[l], s5_log_dt[l], s5_glu_w[l], s5_glu_b[l]).reshape(n, S5_WIDTH)
        o_c = _moba_branch(p_c.reshape(bt, seq, -1)).reshape(n, MOBA_WIDTH)
        x2d = _merge_call(o_a, o_b, o_c, p_gate, x2d, g1, w_branch_a[l].astype(BF16), w_branch_b[l].astype(BF16),
                          w_branch_c[l].astype(BF16), w_out[l].astype(BF16), seq)
        h2, idx_t, gate_t = _peer_query_call(x2d, norm2_w[l].reshape(1, d), sc2, sh2,
                                             peer_wq[l].T.astype(BF16), peer_k1[l], peer_k2[l], seq)
        uv = jnp.concatenate([peer_u[l].reshape(-1, SUBLANES, LANES), peer_v[l].reshape(-1, SUBLANES, LANES)], axis=1)
        x3 = _peer_gather_call(idx_t.T, gate_t.T[:, gate_perm], h2.reshape(n, SUBLANES, LANES),
                               x2d.reshape(n, SUBLANES, LANES), g2.reshape(bt, SUBLANES, LANES), uv, seq)
        x2d = x3.reshape(n, d)
    return _final_norm_call(x2d, final_norm_w.reshape(1, d)).reshape(bt, seq, d)
```

```python
import functools
import math

import jax
import jax.numpy as jnp
import numpy as np
from jax import lax
from jax.experimental import pallas as pl
from jax.experimental.pallas import tpu as pltpu

F32 = jnp.float32
BF16 = jnp.bfloat16
HIGHEST = lax.Precision.HIGHEST

D_MODEL = 1024
GDN_HEADS = 4
GDN_HEAD_DIM = 128
GDN_WIDTH = GDN_HEADS * GDN_HEAD_DIM
GDN_CONV = 4
GDN_CHUNK = 64
S5_GROUP = 16
S5_GROUPS = 16
S5_WIDTH = S5_GROUPS * S5_GROUP
S5_STATE = 64
MOBA_HEADS = 4
MOBA_HEAD_DIM = 64
MOBA_WIDTH = MOBA_HEADS * MOBA_HEAD_DIM
MOBA_BLOCK = 256
MOBA_TOPK = 3
MOBA_Q_CHUNK = 64
N_BRANCH = 3
PEER_HEADS = 8
PEER_NKEYS = 128
PEER_QDIM = 256
PEER_TOPK = 16
PEER_SEL = PEER_HEADS * PEER_TOPK
RMS_EPS = 1e-6
NEG_INF = -1e30

SUBLANES = 8
LANES = 128
VMEM_LIMIT_BYTES = 56 * 1024 * 1024

_OFF_QKV_A = 0
_OFF_Z_A = 3 * GDN_WIDTH
_OFF_BETA = _OFF_Z_A + GDN_WIDTH
_OFF_ALPHA = _OFF_BETA + GDN_HEADS
_OFF_UB = _OFF_ALPHA + GDN_HEADS
_OFF_QKV_C = _OFF_UB + S5_WIDTH
_OFF_GATE = _OFF_QKV_C + 3 * MOBA_WIDTH
_IN_COLS = _OFF_GATE + N_BRANCH * D_MODEL


def _cparams(*sem):
    return pltpu.CompilerParams(dimension_semantics=sem, vmem_limit_bytes=VMEM_LIMIT_BYTES)


def _resident(shape):
    nd = len(shape)
    return pl.BlockSpec(shape, lambda *_: (0,) * nd)


def _ada_kernel(c_ref, w_ref, b_ref, o_ref):
    c = c_ref[...]
    sc = c * jax.nn.sigmoid(c)
    o_ref[...] = jnp.dot(sc, w_ref[...], precision=HIGHEST, preferred_element_type=F32) + b_ref[...]


def _ada_call(c, ada_w, ada_b):
    depth, d, d6 = ada_w.shape
    bt = c.shape[0]
    nj = d6 // d
    return pl.pallas_call(
        _ada_kernel,
        grid=(depth, nj),
        in_specs=[
            pl.BlockSpec((bt, d), lambda l, j: (0, 0)),
            pl.BlockSpec((None, d, d), lambda l, j: (l, 0, j)),
            pl.BlockSpec((None, 1, d), lambda l, j: (l, 0, j)),
        ],
        out_specs=pl.BlockSpec((None, bt, d), lambda l, j: (l, 0, j)),
        out_shape=jax.ShapeDtypeStruct((depth, bt, d6), F32),
        compiler_params=_cparams("parallel", "parallel"),
        name="ada_mod",
    )(c, ada_w, ada_b.reshape(depth, 1, d6))


def _norm_mod(x, nw, sc, sh):
    y = x * lax.rsqrt(jnp.mean(x * x, axis=-1, keepdims=True) + RMS_EPS)
    return (y * nw) * (1.0 + sc) + sh


_IN_SPLITS = (4 * GDN_WIDTH, S5_WIDTH, 3 * MOBA_WIDTH, N_BRANCH * D_MODEL, LANES)


def _in_proj_kernel(x_ref, nw_ref, sc_ref, sh_ref, w_ref, oa_ref, ob_ref, oc_ref, og_ref, oba_ref):
    h = _norm_mod(x_ref[...], nw_ref[...], sc_ref[...], sh_ref[...]).astype(BF16)
    off = 0
    for o_ref, width in zip((oa_ref, ob_ref, oc_ref, og_ref, oba_ref), _IN_SPLITS):
        o_ref[...] = jnp.dot(h, w_ref[:, off:off + width], preferred_element_type=F32)
        off += width


def _in_proj_call(x2d, nw, sc, sh, w_r, seq, tm=256):
    n, d = x2d.shape
    per_b = seq // tm
    wcols = w_r.shape[1]
    outs = [jax.ShapeDtypeStruct((n, wd), F32) for wd in _IN_SPLITS]
    return pl.pallas_call(
        _in_proj_kernel,
        grid=(n // tm,),
        in_specs=[
            pl.BlockSpec((tm, d), lambda i: (i, 0)),
            _resident((1, d)),
            pl.BlockSpec((None, 1, d), lambda i: (i // per_b, 0, 0)),
            pl.BlockSpec((None, 1, d), lambda i: (i // per_b, 0, 0)),
            _resident((d, wcols)),
        ],
        out_specs=[pl.BlockSpec((tm, wd), lambda i: (i, 0)) for wd in _IN_SPLITS],
        out_shape=outs,
        compiler_params=_cparams("parallel"),
        name="in_proj",
    )(x2d, nw, sc, sh, w_r)


def _arrange_w_in(w_in_l):
    pad = jnp.zeros((w_in_l.shape[0], LANES - 2 * GDN_HEADS), w_in_l.dtype)
    return jnp.concatenate([
        w_in_l[:, _OFF_QKV_A:_OFF_BETA],
        w_in_l[:, _OFF_UB:_OFF_QKV_C],
        w_in_l[:, _OFF_QKV_C:_OFF_GATE],
        w_in_l[:, _OFF_GATE:_IN_COLS],
        w_in_l[:, _OFF_BETA:_OFF_UB], pad,
    ], axis=1).astype(BF16)


def _merge_kernel(oa_ref, ob_ref, oc_ref, gt_ref, x_ref, g1_ref, wa_ref, wb_ref, wc_ref, wo_ref, o_ref):
    d = D_MODEL
    ya = jnp.dot(oa_ref[...].astype(BF16), wa_ref[...], preferred_element_type=F32)
    yb = jnp.dot(ob_ref[...].astype(BF16), wb_ref[...], preferred_element_type=F32)
    yc = jnp.dot(oc_ref[...].astype(BF16), wc_ref[...], preferred_element_type=F32)
    merged = (jax.nn.sigmoid(gt_ref[:, 0:d]) * ya + jax.nn.sigmoid(gt_ref[:, d:2 * d]) * yb
              + jax.nn.sigmoid(gt_ref[:, 2 * d:3 * d]) * yc)
    y = jnp.dot(merged.astype(BF16), wo_ref[...], preferred_element_type=F32)
    o_ref[...] = x_ref[...] + g1_ref[...] * y


def _merge_call(o_a, o_b, o_c, p_gate, x2d, g1, wa, wb, wc, wo, seq, tm=512):
    n, d = x2d.shape
    per_b = seq // tm
    row = lambda wd: pl.BlockSpec((tm, wd), lambda i: (i, 0))
    return pl.pallas_call(
        _merge_kernel,
        grid=(n // tm,),
        in_specs=[row(GDN_WIDTH), row(S5_WIDTH), row(MOBA_WIDTH), row(N_BRANCH * d), row(d),
                  pl.BlockSpec((None, 1, d), lambda i: (i // per_b, 0, 0)),
                  _resident(wa.shape), _resident(wb.shape), _resident(wc.shape), _resident(wo.shape)],
        out_specs=row(d),
        out_shape=jax.ShapeDtypeStruct((n, d), F32),
        compiler_params=_cparams("parallel"),
        name="merge_out",
    )(o_a, o_b, o_c, p_gate, x2d, g1, wa, wb, wc, wo)


def _topk_rows(s_ref, n_rows, vals_ref, idx_ref, payload_ref=None):
    shape = s_ref.shape
    iota = lax.broadcasted_iota(jnp.int32, shape, 0).astype(F32)

    def body(k, carry):
        s = s_ref[...]
        m = jnp.max(s, axis=0, keepdims=True)
        pos = jnp.min(jnp.where(s == m, iota, float(n_rows)), axis=0, keepdims=True)
        hit = iota == pos
        vals_ref[pl.ds(k, 1), :] = m
        if payload_ref is None:
            idx_ref[pl.ds(k, 1), :] = pos
        else:
            idx_ref[pl.ds(k, 1), :] = jnp.sum(jnp.where(hit, payload_ref[...], 0.0), axis=0, keepdims=True)
        s_ref[...] = jnp.where(hit, -jnp.inf, s)
        return carry

    lax.fori_loop(0, PEER_TOPK, body, 0)


def _peer_query_kernel(x_ref, nw_ref, sc_ref, sh_ref, wqt_ref, k1_ref, k2_ref, h_ref, idx_ref, gate_ref,
                       q_scr, s_scr, v1_scr, i1_scr, v2_scr, i2_scr, cs_scr, ci_scr, ts_scr, te_scr):
    h = _norm_mod(x_ref[...], nw_ref[...], sc_ref[...], sh_ref[...])
    h_ref[...] = h
    q_scr[...] = lax.dot_general(wqt_ref[...], h.astype(BF16), (((1,), (1,)), ((), ())),
                                 preferred_element_type=F32)
    half = PEER_QDIM // 2

    def head_body(hd, carry):
        base = pl.multiple_of(hd * PEER_QDIM, PEER_QDIM)
        s_scr[...] = jnp.dot(k1_ref[hd], q_scr[pl.ds(base, half), :], precision=HIGHEST,
                             preferred_element_type=F32)
        _topk_rows(s_scr, PEER_NKEYS, v1_scr, i1_scr)
        s_scr[...] = jnp.dot(k2_ref[hd], q_scr[pl.ds(base + half, half), :], precision=HIGHEST,
                             preferred_element_type=F32)
        _topk_rows(s_scr, PEER_NKEYS, v2_scr, i2_scr)
        v2 = v2_scr[...]
        i2 = i2_scr[...]
        for a in range(PEER_TOPK):
            cs_scr[a * PEER_TOPK:(a + 1) * PEER_TOPK, :] = v1_scr[a:a + 1, :] + v2
            ci_scr[a * PEER_TOPK:(a + 1) * PEER_TOPK, :] = i1_scr[a:a + 1, :] * float(PEER_NKEYS) + i2
        _topk_rows(cs_scr, PEER_TOPK * PEER_TOPK, ts_scr, te_scr, payload_ref=ci_scr)
        ts = ts_scr[...]
        e = jnp.exp(ts - jnp.max(ts, axis=0, keepdims=True))
        out_base = pl.multiple_of(hd * PEER_TOPK, PEER_TOPK)
        gate_ref[pl.ds(out_base, PEER_TOPK), :] = e / jnp.sum(e, axis=0, keepdims=True)
        idx_ref[pl.ds(out_base, PEER_TOPK), :] = te_scr[...].astype(jnp.int32)
        return carry

    lax.fori_loop(0, PEER_HEADS, head_body, 0)


def _peer_query_call(x2d, nw, sc, sh, wqt, k1, k2, seq, tm=256):
    n, d = x2d.shape
    per_b = seq // tm
    nq = PEER_HEADS * PEER_QDIM
    k = PEER_TOPK
    scratch = [pltpu.VMEM((nq, tm), F32), pltpu.VMEM((PEER_NKEYS, tm), F32),
               pltpu.VMEM((k, tm), F32), pltpu.VMEM((k, tm), F32), pltpu.VMEM((k, tm), F32), pltpu.VMEM((k, tm), F32),
               pltpu.VMEM((k * k, tm), F32), pltpu.VMEM((k * k, tm), F32),
               pltpu.VMEM((k, tm), F32), pltpu.VMEM((k, tm), F32)]
    return pl.pallas_call(
        _peer_query_kernel,
        grid=(n // tm,),
        in_specs=[
            pl.BlockSpec((tm, d), lambda i: (i, 0)),
            _resident((1, d)),
            pl.BlockSpec((None, 1, d), lambda i: (i // per_b, 0, 0)),
            pl.BlockSpec((None, 1, d), lambda i: (i // per_b, 0, 0)),
            _resident(wqt.shape), _resident(k1.shape), _resident(k2.shape),
        ],
        out_specs=[pl.BlockSpec((tm, d), lambda i: (i, 0)),
                   pl.BlockSpec((PEER_SEL, tm), lambda i: (0, i)),
                   pl.BlockSpec((PEER_SEL, tm), lambda i: (0, i))],
        out_shape=[jax.ShapeDtypeStruct((n, d), F32),
                   jax.ShapeDtypeStruct((PEER_SEL, n), jnp.int32),
                   jax.ShapeDtypeStruct((PEER_SEL, n), F32)],
        scratch_shapes=scratch,
        compiler_params=_cparams("parallel"),
        name="peer_query",
    )(x2d, nw, sc, sh, wqt, k1, k2)


PEER_TOKENS_PER_STEP = 8
_ROW_SUB = 2 * SUBLANES


def _fold_pair(a, b, shift, keep_a):
    fa = a + pltpu.roll(a, shift, axis=0)
    fb = b + pltpu.roll(b, SUBLANES - shift, axis=0)
    return jnp.where(keep_a, fa, fb)


def _slot_order():
    tiles = [[k] * SUBLANES for k in range(PEER_SEL)]
    for shift, keep in ((4, [j < 4 for j in range(8)]), (2, [(j % 4) >= 2 for j in range(8)]),
                        (1, [(j % 2) == 1 for j in range(8)])):
        tiles = [[tiles[2 * m][j] if keep[j] else tiles[2 * m + 1][j] for j in range(8)]
                 for m in range(len(tiles) // 2)]
    return [s for t in tiles for s in t]


_SLOT_AT = _slot_order()
_POS_OF = [0] * PEER_SEL
for _p, _s in enumerate(_SLOT_AT):
    _POS_OF[_s] = _p


def _peer_gather_kernel(idx0_ref, idxn_ref, g_ref, h_ref, x_ref, g2_ref, uv_hbm, o_ref, buf, wb_scr, sems):
    tb = PEER_TOKENS_PER_STEP
    i = pl.program_id(0)
    nb = pl.num_programs(0)
    cur = (i % 2) * tb
    nxt = tb - cur

    def row_copy(e, row, slot):
        return pltpu.make_async_copy(uv_hbm.at[e], buf.at[row], sems.at[slot])

    def slot_copy(slot):
        return pltpu.make_async_copy(uv_hbm.at[pl.ds(0, PEER_SEL)], buf.at[pl.ds(slot * PEER_SEL, PEER_SEL)],
                                     sems.at[slot])

    @pl.when(i == 0)
    def _():
        def issue0(t, c):
            for k in range(PEER_SEL):
                row_copy(idx0_ref[0, t * PEER_SEL + k], t * PEER_SEL + k, t).start()
            return c
        lax.fori_loop(0, tb, issue0, 0)

    sub = lax.broadcasted_iota(jnp.int32, (SUBLANES, LANES), 0)
    keep4 = sub < 4
    keep2 = (sub % 4) >= 2
    keep1 = (sub % 2) == 1
    eye = (lax.broadcasted_iota(jnp.int32, (PEER_SEL, LANES), 0)
           == lax.broadcasted_iota(jnp.int32, (PEER_SEL, LANES), 1))

    def token(t, c):
        slot = cur + t
        nslot = nxt + t
        slot_copy(slot).wait()
        for k in range(PEER_SEL):
            row_copy(idxn_ref[0, t * PEER_SEL + k], nslot * PEER_SEL + k, nslot).start()
        base = slot * PEER_SEL
        ht = h_ref[t]
        tiles = [buf[base + k, 0:SUBLANES, :] * ht for k in range(PEER_SEL)]
        tiles = [_fold_pair(tiles[2 * m], tiles[2 * m + 1], 4, keep4) for m in range(PEER_SEL // 2)]
        tiles = [_fold_pair(tiles[2 * m], tiles[2 * m + 1], 2, keep2) for m in range(PEER_SEL // 4)]
        tiles = [_fold_pair(tiles[2 * m], tiles[2 * m + 1], 1, keep1) for m in range(PEER_SEL // 8)]
        part = jnp.concatenate(tiles, axis=0)
        s = jnp.sum(part, axis=-1, keepdims=True)
        act = 0.5 * s * (1.0 + lax.erf(s * (2.0 ** -0.5)))
        grow = jnp.broadcast_to(g_ref[pl.ds(t, 1), :], (PEER_SEL, LANES))
        gcol = jnp.sum(jnp.where(eye, grow, 0.0), axis=-1, keepdims=True)
        wb_scr[...] = jnp.broadcast_to(gcol * act, (PEER_SEL, LANES))
        accs = [jnp.zeros((SUBLANES, LANES), F32) for _ in range(4)]
        for k in range(PEER_SEL):
            p = _POS_OF[k]
            wk = jnp.broadcast_to(wb_scr[p:p + 1, :], (SUBLANES, LANES))
            accs[k % 4] = accs[k % 4] + wk * buf[base + k, SUBLANES:_ROW_SUB, :]
        y = (accs[0] + accs[1]) + (accs[2] + accs[3])
        o_ref[t] = x_ref[t] + g2_ref[...] * y
        return c

    lax.fori_loop(0, tb, token, 0)

    @pl.when(i == nb - 1)
    def _():
        for t in range(tb):
            slot_copy(nxt + t).wait()


def _peer_gather_call(idx, gate_p, h3, x3, g2_3, uv, seq):
    n = idx.shape[0]
    tb = PEER_TOKENS_PER_STEP
    nb = n // tb
    per_b = seq // tb
    idx3 = idx.reshape(nb, 1, tb * PEER_SEL)
    smem_blk = lambda fn: pl.BlockSpec((None, 1, tb * PEER_SEL), fn, memory_space=pltpu.SMEM)
    tok3 = pl.BlockSpec((tb, SUBLANES, LANES), lambda i: (i, 0, 0))
    return pl.pallas_call(
        _peer_gather_kernel,
        grid=(nb,),
        in_specs=[
            smem_blk(lambda i: (0, 0, 0)),
            smem_blk(lambda i: (jnp.minimum(i + 1, nb - 1), 0, 0)),
            pl.BlockSpec((tb, PEER_SEL), lambda i: (i, 0)),
            tok3, tok3,
            pl.BlockSpec((None, SUBLANES, LANES), lambda i: (i // per_b, 0, 0)),
            pl.BlockSpec(memory_space=pl.ANY),
        ],
        out_specs=tok3,
        out_shape=jax.ShapeDtypeStruct((n, SUBLANES, LANES), F32),
        scratch_shapes=[pltpu.VMEM((2 * tb * PEER_SEL, _ROW_SUB, LANES), F32),
                        pltpu.VMEM((PEER_SEL, LANES), F32),
                        pltpu.SemaphoreType.DMA((2 * tb,))],
        compiler_params=_cparams("arbitrary"),
        name="peer_gather",
    )(idx3, idx3, gate_p, h3, x3, g2_3, uv)


def _final_norm_kernel(x_ref, w_ref, o_ref):
    x = x_ref[...]
    o_ref[...] = x * lax.rsqrt(jnp.mean(x * x, axis=-1, keepdims=True) + RMS_EPS) * w_ref[...]


def _final_norm_call(x2d, w, tm=1024):
    n, d = x2d.shape
    return pl.pallas_call(
        _final_norm_kernel,
        grid=(n // tm,),
        in_specs=[pl.BlockSpec((tm, d), lambda i: (i, 0)), _resident((1, d))],
        out_specs=pl.BlockSpec((tm, d), lambda i: (i, 0)),
        out_shape=jax.ShapeDtypeStruct((n, d), F32),
        compiler_params=_cparams("parallel"),
        name="final_norm",
    )(x2d, w)


S5_LANES = S5_GROUPS * S5_STATE


def _s5_disc_kernel(are_ref, aim_ref, ldt_ref, bre_ref, bim_ref, ar_ref, ai_ref, br_ref, bi_ref):
    a_re = are_ref[...]
    a_im = aim_ref[...]
    dt = jnp.exp(ldt_ref[...])
    mag = jnp.exp(a_re * dt)
    ar = mag * jnp.cos(a_im * dt)
    ai = mag * jnp.sin(a_im * dt)
    den = a_re * a_re + a_im * a_im
    cr = ((ar - 1.0) * a_re + ai * a_im) / den
    ci = (ai * a_re - (ar - 1.0) * a_im) / den
    ar_ref[...] = ar
    ai_ref[...] = ai
    br_ref[...] = cr * bre_ref[...] - ci * bim_ref[...]
    bi_ref[...] = cr * bim_ref[...] + ci * bre_ref[...]


def _s5_disc_call(a_re, a_im, log_dt, b_re, b_im):
    rows = S5_LANES
    col = lambda t: t.reshape(rows, 1)
    ldt = jnp.broadcast_to(log_dt[:, None], (S5_GROUPS, S5_STATE))
    out = [jax.ShapeDtypeStruct((rows, 1), F32)] * 2 + [jax.ShapeDtypeStruct((rows, S5_GROUP), F32)] * 2
    return pl.pallas_call(_s5_disc_kernel, out_shape=out, name="s5_disc")(
        col(a_re), col(a_im), col(ldt), b_re.reshape(rows, S5_GROUP), b_im.reshape(rows, S5_GROUP))


def _s5_kernel(u_ref, bcat_ref, ar_ref, ai_ref, ccat_ref, d_ref, gw_ref, gb_ref, o_ref, st_scr, bu_scr, *, steps, bt):
    @pl.when(pl.program_id(0) == 0)
    def _():
        st_scr[...] = jnp.zeros_like(st_scr)

    u = u_ref[...]
    bu_scr[...] = jnp.dot(u, bcat_ref[...], precision=HIGHEST, preferred_element_type=F32)
    a_re = jnp.broadcast_to(ar_ref[...], (bt, S5_LANES))
    a_im = jnp.broadcast_to(ai_ref[...], (bt, S5_LANES))

    def step(t, carry):
        s_re, s_im = carry
        r = pl.multiple_of(t * bt, bt)
        n_re = a_re * s_re - a_im * s_im + bu_scr[pl.ds(r, bt), 0:S5_LANES]
        n_im = a_re * s_im + a_im * s_re + bu_scr[pl.ds(r, bt), S5_LANES:2 * S5_LANES]
        bu_scr[pl.ds(r, bt), 0:S5_LANES] = n_re
        bu_scr[pl.ds(r, bt), S5_LANES:2 * S5_LANES] = n_im
        return n_re, n_im

    s_re, s_im = lax.fori_loop(0, steps, step, (st_scr[:, 0:S5_LANES], st_scr[:, S5_LANES:2 * S5_LANES]))
    st_scr[:, 0:S5_LANES] = s_re
    st_scr[:, S5_LANES:2 * S5_LANES] = s_im
    y = jnp.dot(bu_scr[...], ccat_ref[...], precision=HIGHEST, preferred_element_type=F32) + d_ref[...] * u
    zg = 0.5 * y * (1.0 + lax.erf(y * (2.0 ** -0.5)))
    zz = jnp.dot(zg.astype(BF16), gw_ref[...], preferred_element_type=F32) + gb_ref[...]
    o_ref[...] = zz[:, 0:S5_WIDTH] * jax.nn.sigmoid(zz[:, S5_WIDTH:2 * S5_WIDTH])


def _s5_call(u_tb, bcat, abar_re, abar_im, ccat, d, glu_w, glu_b, bt, steps=64):
    rows = u_tb.shape[0]
    blk = steps * bt
    return pl.pallas_call(
        functools.partial(_s5_kernel, steps=steps, bt=bt),
        grid=(rows // blk,),
        in_specs=[pl.BlockSpec((blk, S5_WIDTH), lambda i: (i, 0)),
                  _resident(bcat.shape), _resident(abar_re.shape), _resident(abar_im.shape), _resident(ccat.shape),
                  _resident(d.shape), _resident(glu_w.shape), _resident(glu_b.shape)],
        out_specs=pl.BlockSpec((blk, S5_WIDTH), lambda i: (i, 0)),
        out_shape=jax.ShapeDtypeStruct((rows, S5_WIDTH), F32),
        scratch_shapes=[pltpu.VMEM((bt, 2 * S5_LANES), F32), pltpu.VMEM((blk, 2 * S5_LANES), F32)],
        compiler_params=_cparams("arbitrary"),
        name="s5_scan",
    )(u_tb, bcat, abar_re, abar_im, ccat, d, glu_w, glu_b)


def _s5_branch_pallas(p_ub, bt, seq, a_re, a_im, b_re, b_im, c_re, c_im, d, log_dt, glu_w, glu_b):
    ar, ai, br, bi = _s5_disc_call(a_re, a_im, log_dt, b_re, b_im)
    g_ix = jnp.arange(S5_GROUPS)
    def blockdiag_in(b):
        b = b.reshape(S5_GROUPS, S5_STATE, S5_GROUP)
        full = jnp.zeros((S5_GROUPS, S5_GROUP, S5_GROUPS, S5_STATE), F32)
        return full.at[g_ix, :, g_ix, :].set(b.transpose(0, 2, 1)).reshape(S5_WIDTH, S5_LANES)

    def blockdiag_out(c):
        full = jnp.zeros((S5_GROUPS, S5_STATE, S5_GROUPS, S5_GROUP), F32)
        return full.at[g_ix, :, g_ix, :].set(c.transpose(0, 2, 1)).reshape(S5_LANES, S5_WIDTH)

    bcat = jnp.concatenate([blockdiag_in(br), blockdiag_in(bi)], axis=1)
    ccat = jnp.concatenate([blockdiag_out(c_re), -blockdiag_out(c_im)], axis=0)
    u_tb = p_ub.reshape(bt, seq, S5_WIDTH).transpose(1, 0, 2).reshape(seq * bt, S5_WIDTH)
    o_tb = _s5_call(u_tb, bcat, ar.reshape(1, S5_LANES), ai.reshape(1, S5_LANES), ccat, d.reshape(1, S5_WIDTH),
                    glu_w.astype(BF16), glu_b.reshape(1, 2 * S5_WIDTH), bt)
    return o_tb.reshape(seq, bt, S5_WIDTH).transpose(1, 0, 2).reshape(bt * seq, S5_WIDTH)


def _mm(a, b):
    return jnp.dot(a.astype(BF16), b.astype(BF16), preferred_element_type=F32)


def _mm_nt(a, b, precision=None):
    if precision is None:
        a, b = a.astype(BF16), b.astype(BF16)
    return lax.dot_general(a, b, (((1,), (1,)), ((), ())), precision=precision, preferred_element_type=F32)


def _mm_tn(a, b):
    return lax.dot_general(a.astype(BF16), b.astype(BF16), (((0,), (0,)), ((), ())), preferred_element_type=F32)


def _softplus(x):
    return jnp.maximum(x, 0.0) + jnp.log1p(jnp.exp(-jnp.abs(x)))


def _gdn_kernel(pa_ref, ba_ref, cw_ref, alog_ref, dtb_ref, nw_ref, o_ref,
                st_scr, carry_scr, q_scr, k_scr, v_scr, g_scr, b_scr):
    j = pl.program_id(1)
    t_blk = pa_ref.shape[0]
    c_len, wid, dh = GDN_CHUNK, GDN_WIDTH, GDN_HEAD_DIM

    @pl.when(j == 0)
    def _():
        st_scr[...] = jnp.zeros_like(st_scr)
        carry_scr[...] = jnp.zeros_like(carry_scr)

    x = pa_ref[:, 0:3 * wid]
    xx = jnp.concatenate([carry_scr[...], x], axis=0)
    conv = cw_ref[GDN_CONV - 1:GDN_CONV, :] * x
    for k in range(1, GDN_CONV):
        conv = conv + cw_ref[GDN_CONV - 1 - k:GDN_CONV - k, :] * pltpu.roll(xx, k, axis=0)[SUBLANES:, :]
    carry_scr[...] = x[t_blk - SUBLANES:t_blk, :]
    act = conv * jax.nn.sigmoid(conv)
    for h in range(GDN_HEADS):
        qh = act[:, h * dh:(h + 1) * dh]
        kh = act[:, wid + h * dh:wid + (h + 1) * dh]
        q_scr[:, h * dh:(h + 1) * dh] = qh * lax.rsqrt(jnp.sum(qh * qh, axis=-1, keepdims=True) + 1e-6) * (dh ** -0.5)
        k_scr[:, h * dh:(h + 1) * dh] = kh * lax.rsqrt(jnp.sum(kh * kh, axis=-1, keepdims=True) + 1e-6)
    v_scr[...] = act[:, 2 * wid:3 * wid]
    ba = ba_ref[...]
    g_scr[...] = -jnp.exp(alog_ref[...]) * _softplus(ba + dtb_ref[...])
    b_scr[...] = jax.nn.sigmoid(ba)

    row = lax.broadcasted_iota(jnp.int32, (c_len, c_len), 0)
    col = lax.broadcasted_iota(jnp.int32, (c_len, c_len), 1)
    tril = row >= col
    strict = row > col
    tril_f = jnp.where(tril, 1.0, 0.0)
    eye = jnp.where(row == col, 1.0, 0.0)
    lane = lax.broadcasted_iota(jnp.int32, (c_len, LANES), 1)
    nw = nw_ref[...]

    def chunk(c, carry):
        r0 = pl.multiple_of(c * c_len, c_len)
        rows = pl.ds(r0, c_len)
        gcum = jnp.dot(tril_f, g_scr[rows, :], precision=HIGHEST, preferred_element_type=F32)
        beta_all = b_scr[rows, :]
        for h in range(GDN_HEADS):
            hs = slice(h * dh, (h + 1) * dh)
            gi = gcum[:, GDN_HEADS + h:GDN_HEADS + h + 1]
            beta = beta_all[:, h:h + 1]
            q_c, k_c, v_c = q_scr[rows, hs], k_scr[rows, hs], v_scr[rows, hs]
            g1 = jnp.where(lane == 0, gi, jnp.where(lane == 1, 1.0, 0.0))
            g2 = jnp.where(lane == 0, 1.0, jnp.where(lane == 1, -gi, 0.0))
            diff = _mm_nt(g1, g2, precision=HIGHEST)
            decay = jnp.where(tril, jnp.exp(jnp.where(tril, diff, 0.0)), 0.0)
            kb = k_c * beta
            a = jnp.where(strict, _mm_nt(kb, k_c) * decay, 0.0)
            p = -a
            t = eye + p
            for _ in range(int(math.log2(c_len)) - 1):
                p = _mm(p, p)
                t = t + _mm(t, p)
            e_g = jnp.exp(gi)
            u = _mm(t, v_c * beta)
            w = _mm(t, kb * e_g)
            qk = jnp.where(tril, _mm_nt(q_c, k_c) * decay, 0.0)
            state = st_scr[h]
            v_new = u - _mm(w, state)
            o = _mm(q_c * e_g, state) + _mm(qk, v_new)
            g_last = gi[c_len - 1:c_len, :]
            st_scr[h] = state * jnp.exp(g_last) + _mm_tn(k_c * jnp.exp(g_last - gi), v_new)
            z = pa_ref[rows, 3 * wid + h * dh:3 * wid + (h + 1) * dh]
            o_ref[rows, hs] = (o * lax.rsqrt(jnp.mean(o * o, axis=-1, keepdims=True) + RMS_EPS) * nw
                               * (z * jax.nn.sigmoid(z)))
        return carry

    lax.fori_loop(0, t_blk // c_len, chunk, 0)


def _gdn_branch_pallas(p_a, p_ba, conv_w, a_log, dt_bias, norm_w, bt, seq, t_blk=256):
    n = p_a.shape[0]
    nj = seq // t_blk
    wid = GDN_WIDTH
    lane_pad = lambda v: jnp.zeros((1, LANES), F32).at[0, GDN_HEADS:2 * GDN_HEADS].set(v)
    rows = lambda wd: pl.BlockSpec((t_blk, wd), lambda b, j: (b * nj + j, 0))
    return pl.pallas_call(
        _gdn_kernel,
        grid=(bt, nj),
        in_specs=[rows(4 * wid), rows(LANES), _resident((GDN_CONV, 3 * wid)), _resident((1, LANES)),
                  _resident((1, LANES)), _resident((1, GDN_HEAD_DIM))],
        out_specs=rows(wid),
        out_shape=jax.ShapeDtypeStruct((n, wid), F32),
        scratch_shapes=[pltpu.VMEM((GDN_HEADS, GDN_HEAD_DIM, GDN_HEAD_DIM), F32), pltpu.VMEM((SUBLANES, 3 * wid), F32),
                        pltpu.VMEM((t_blk, wid), F32), pltpu.VMEM((t_blk, wid), F32), pltpu.VMEM((t_blk, wid), F32),
                        pltpu.VMEM((t_blk, LANES), F32), pltpu.VMEM((t_blk, LANES), F32)],
        compiler_params=_cparams("parallel", "arbitrary"),
        name="gdn_delta",
    )(p_a, p_ba, conv_w, lane_pad(a_log), lane_pad(dt_bias), norm_w.reshape(1, GDN_HEAD_DIM))


def _moba_kernel(qt_ref, k_ref, vt_ref, o_ref, km_scr, sel_scr):
    i = pl.program_id(2)
    nblk = k_ref.shape[0]

    @pl.when(i == 0)
    def _():
        for n in range(nblk):
            km_scr[n:n + 1, :] = jnp.mean(k_ref[n], axis=0, keepdims=True)

    qt = qt_ref[...]
    gate = jnp.dot(km_scr[...], qt, precision=HIGHEST, preferred_element_type=F32)
    blk = lax.broadcasted_iota(jnp.int32, gate.shape, 0)
    cnt = jnp.zeros(gate.shape, F32)
    for m in range(nblk):
        gm = gate[m:m + 1, :]
        beats = (gm > gate) | ((gm == gate) & (m < blk))
        cnt = cnt + jnp.where(beats & (m < i), 1.0, 0.0)
    sel_scr[...] = jnp.where((blk < i) & (cnt < float(MOBA_TOPK)), 1.0, 0.0)

    scale = MOBA_HEAD_DIM ** -0.5
    qb = qt.astype(BF16)
    s = jnp.dot(k_ref[i].astype(BF16), qb, preferred_element_type=F32) * scale
    kpos = lax.broadcasted_iota(jnp.int32, s.shape, 0)
    qpos = lax.broadcasted_iota(jnp.int32, s.shape, 1)
    s = jnp.where(kpos <= qpos, s, NEG_INF)
    m0 = jnp.max(s, axis=0, keepdims=True)
    p = jnp.exp(s - m0)
    l0 = jnp.sum(p, axis=0, keepdims=True)
    acc0 = jnp.dot(vt_ref[i].astype(BF16), p.astype(BF16), preferred_element_type=F32)

    def body(n, carry):
        m, l, acc = carry
        s = jnp.dot(k_ref[n].astype(BF16), qb, preferred_element_type=F32) * scale
        s = jnp.where(sel_scr[pl.ds(n, 1), :] > 0.5, s, NEG_INF)
        m_new = jnp.maximum(m, jnp.max(s, axis=0, keepdims=True))
        p = jnp.exp(s - m_new)
        alpha = jnp.exp(m - m_new)
        l = alpha * l + jnp.sum(p, axis=0, keepdims=True)
        acc = alpha * acc + jnp.dot(vt_ref[n].astype(BF16), p.astype(BF16), preferred_element_type=F32)
        return m_new, l, acc

    m, l, acc = lax.fori_loop(0, i, body, (m0, l0, acc0))
    o_ref[...] = acc / l


def _moba_branch_pallas(p_c, bt, seq):
    nh, dh, bs = MOBA_HEADS, MOBA_HEAD_DIM, MOBA_BLOCK
    assert seq % bs == 0
    nblk = seq // bs
    qkv = p_c.reshape(bt, seq, 3, nh, dh)
    qt = qkv[:, :, 0].transpose(0, 2, 3, 1)
    k5 = qkv[:, :, 1].reshape(bt, nblk, bs, nh, dh).transpose(0, 3, 1, 2, 4)
    vt5 = qkv[:, :, 2].reshape(bt, nblk, bs, nh, dh).transpose(0, 3, 1, 4, 2)
    ot = pl.pallas_call(
        _moba_kernel,
        grid=(bt, nh, nblk),
        in_specs=[pl.BlockSpec((None, None, dh, bs), lambda b, h, i: (b, h, 0, i)),
                  pl.BlockSpec((None, None, nblk, bs, dh), lambda b, h, i: (b, h, 0, 0, 0)),
                  pl.BlockSpec((None, None, nblk, dh, bs), lambda b, h, i: (b, h, 0, 0, 0))],
        out_specs=pl.BlockSpec((None, None, dh, bs), lambda b, h, i: (b, h, 0, i)),
        out_shape=jax.ShapeDtypeStruct((bt, nh, dh, seq), F32),
        scratch_shapes=[pltpu.VMEM((nblk, dh), F32), pltpu.VMEM((nblk, bs), F32)],
        compiler_params=_cparams("parallel", "parallel", "arbitrary"),
        name="moba_attn",
    )(qt, k5, vt5)
    return ot.transpose(0, 3, 1, 2).reshape(bt * seq, nh * dh)


def kernel(x, c, ada_w, ada_b, norm1_w, w_in, gdn_conv_w, gdn_a_log, gdn_dt_bias, gdn_norm_w, s5_a_re, s5_a_im, s5_b_re, s5_b_im, s5_c_re, s5_c_im, s5_d, s5_log_dt, s5_glu_w, s5_glu_b, w_branch_a, w_branch_b, w_branch_c, w_out, norm2_w, peer_wq, peer_k1, peer_k2, peer_u, peer_v, final_norm_w):
    bt, seq, d = x.shape
    n = bt * seq
    depth = ada_w.shape[0]
    x2d = x.reshape(n, d)
    mod = _ada_call(c, ada_w, ada_b)
    gate_perm = np.asarray(_SLOT_AT, np.int32)
    for l in range(depth):
        sh1, sc1, g1, sh2, sc2, g2 = (mod[l, :, j * d:(j + 1) * d].reshape(bt, 1, d) for j in range(6))
        p_a, p_ub, p_c, p_gate, p_ba = _in_proj_call(
            x2d, norm1_w[l].reshape(1, d), sc1, sh1, _arrange_w_in(w_in[l]), seq)
        o_a = _gdn_branch_pallas(p_a, p_ba, gdn_conv_w[l], gdn_a_log[l], gdn_dt_bias[l], gdn_norm_w[l], bt, seq)
        o_b = _s5_branch_pallas(p_ub, bt, seq, s5_a_re[l], s5_a_im[l], s5_b_re[l], s5_b_im[l], s5_c_re[l],
                                s5_c_im[l], s5_d[l], s5_log_dt[l], s5_glu_w[l], s5_glu_b[l])
        o_c = _moba_branch_pallas(p_c, bt, seq)
        x2d = _merge_call(o_a, o_b, o_c, p_gate, x2d, g1, w_branch_a[l].astype(BF16), w_branch_b[l].astype(BF16),
                          w_branch_c[l].astype(BF16), w_out[l].astype(BF16), seq)
        h2, idx_t, gate_t = _peer_query_call(x2d, norm2_w[l].reshape(1, d), sc2, sh2,
                                             peer_wq[l].T.astype(BF16), peer_k1[l], peer_k2[l], seq)
        uv = jnp.concatenate([peer_u[l].reshape(-1, SUBLANES, LANES), peer_v[l].reshape(-1, SUBLANES, LANES)], axis=1)
        x3 = _peer_gather_call(idx_t.T, gate_t.T[:, gate_perm], h2.reshape(n, SUBLANES, LANES),
                               x2d.reshape(n, SUBLANES, LANES), g2.reshape(bt, SUBLANES, LANES), uv, seq)
        x2d = x3.reshape(n, d)
    return _final_norm_call(x2d, final_norm_w.reshape(1, d)).reshape(bt, seq, d)
```

```python
import functools
import math

import jax
import jax.numpy as jnp
import numpy as np
from jax import lax
from jax.experimental import pallas as pl
from jax.experimental.pallas import tpu as pltpu

F32 = jnp.float32
BF16 = jnp.bfloat16
HIGHEST = lax.Precision.HIGHEST

D_MODEL = 1024
GDN_HEADS = 4
GDN_HEAD_DIM = 128
GDN_WIDTH = GDN_HEADS * GDN_HEAD_DIM
GDN_CONV = 4
GDN_CHUNK = 64
S5_GROUP = 16
S5_GROUPS = 16
S5_WIDTH = S5_GROUPS * S5_GROUP
S5_STATE = 64
MOBA_HEADS = 4
MOBA_HEAD_DIM = 64
MOBA_WIDTH = MOBA_HEADS * MOBA_HEAD_DIM
MOBA_BLOCK = 256
MOBA_TOPK = 3
MOBA_Q_CHUNK = 64
N_BRANCH = 3
PEER_HEADS = 8
PEER_NKEYS = 128
PEER_QDIM = 256
PEER_TOPK = 16
PEER_SEL = PEER_HEADS * PEER_TOPK
RMS_EPS = 1e-6
NEG_INF = -1e30

SUBLANES = 8
LANES = 128
VMEM_LIMIT_BYTES = 56 * 1024 * 1024

_OFF_QKV_A = 0
_OFF_Z_A = 3 * GDN_WIDTH
_OFF_BETA = _OFF_Z_A + GDN_WIDTH
_OFF_ALPHA = _OFF_BETA + GDN_HEADS
_OFF_UB = _OFF_ALPHA + GDN_HEADS
_OFF_QKV_C = _OFF_UB + S5_WIDTH
_OFF_GATE = _OFF_QKV_C + 3 * MOBA_WIDTH
_IN_COLS = _OFF_GATE + N_BRANCH * D_MODEL


def _cparams(*sem):
    return pltpu.CompilerParams(dimension_semantics=sem, vmem_limit_bytes=VMEM_LIMIT_BYTES)


def _resident(shape):
    nd = len(shape)
    return pl.BlockSpec(shape, lambda *_: (0,) * nd)


def _ada_kernel(c_ref, w_ref, b_ref, o_ref):
    c = c_ref[...]
    sc = c * jax.nn.sigmoid(c)
    o_ref[...] = jnp.dot(sc, w_ref[...], precision=HIGHEST, preferred_element_type=F32) + b_ref[...]


def _ada_call(c, ada_w, ada_b):
    depth, d, d6 = ada_w.shape
    bt = c.shape[0]
    nj = d6 // d
    return pl.pallas_call(
        _ada_kernel,
        grid=(depth, nj),
        in_specs=[
            pl.BlockSpec((bt, d), lambda l, j: (0, 0)),
            pl.BlockSpec((None, d, d), lambda l, j: (l, 0, j)),
            pl.BlockSpec((None, 1, d), lambda l, j: (l, 0, j)),
        ],
        out_specs=pl.BlockSpec((None, bt, d), lambda l, j: (l, 0, j)),
        out_shape=jax.ShapeDtypeStruct((depth, bt, d6), F32),
        compiler_params=_cparams("parallel", "parallel"),
        name="ada_mod",
    )(c, ada_w, ada_b.reshape(depth, 1, d6))


def _norm_mod(x, nw, sc, sh):
    y = x * lax.rsqrt(jnp.mean(x * x, axis=-1, keepdims=True) + RMS_EPS)
    return (y * nw) * (1.0 + sc) + sh


_IN_SPLITS = (4 * GDN_WIDTH, S5_WIDTH, 3 * MOBA_WIDTH, N_BRANCH * D_MODEL, LANES)


def _in_proj_kernel(x_ref, nw_ref, sc_ref, sh_ref, w_ref, oa_ref, ob_ref, oc_ref, og_ref, oba_ref):
    h = _norm_mod(x_ref[...], nw_ref[...], sc_ref[...], sh_ref[...]).astype(BF16)
    off = 0
    for o_ref, width in zip((oa_ref, ob_ref, oc_ref, og_ref, oba_ref), _IN_SPLITS):
        o_ref[...] = jnp.dot(h, w_ref[:, off:off + width], preferred_element_type=F32)
        off += width


def _in_proj_call(x2d, nw, sc, sh, w_r, seq, tm=256):
    n, d = x2d.shape
    per_b = seq // tm
    wcols = w_r.shape[1]
    outs = [jax.ShapeDtypeStruct((n, wd), F32) for wd in _IN_SPLITS]
    return pl.pallas_call(
        _in_proj_kernel,
        grid=(n // tm,),
        in_specs=[
            pl.BlockSpec((tm, d), lambda i: (i, 0)),
            _resident((1, d)),
            pl.BlockSpec((None, 1, d), lambda i: (i // per_b, 0, 0)),
            pl.BlockSpec((None, 1, d), lambda i: (i // per_b, 0, 0)),
            _resident((d, wcols)),
        ],
        out_specs=[pl.BlockSpec((tm, wd), lambda i: (i, 0)) for wd in _IN_SPLITS],
        out_shape=outs,
        compiler_params=_cparams("parallel"),
        name="in_proj",
    )(x2d, nw, sc, sh, w_r)


def _arrange_w_in(w_in_l):
    pad = jnp.zeros((w_in_l.shape[0], LANES - 2 * GDN_HEADS), w_in_l.dtype)
    return jnp.concatenate([
        w_in_l[:, _OFF_QKV_A:_OFF_BETA],
        w_in_l[:, _OFF_UB:_OFF_QKV_C],
        w_in_l[:, _OFF_QKV_C:_OFF_GATE],
        w_in_l[:, _OFF_GATE:_IN_COLS],
        w_in_l[:, _OFF_BETA:_OFF_UB], pad,
    ], axis=1).astype(BF16)


def _merge_kernel(oa_ref, ob_ref, oc_ref, gt_ref, x_ref, g1_ref, wa_ref, wb_ref, wc_ref, wo_ref, o_ref):
    d = D_MODEL
    ya = jnp.dot(oa_ref[...].astype(BF16), wa_ref[...], preferred_element_type=F32)
    yb = jnp.dot(ob_ref[...].astype(BF16), wb_ref[...], preferred_element_type=F32)
    yc = jnp.dot(oc_ref[...].astype(BF16), wc_ref[...], preferred_element_type=F32)
    merged = (jax.nn.sigmoid(gt_ref[:, 0:d]) * ya + jax.nn.sigmoid(gt_ref[:, d:2 * d]) * yb
              + jax.nn.sigmoid(gt_ref[:, 2 * d:3 * d]) * yc)
    y = jnp.dot(merged.astype(BF16), wo_ref[...], preferred_element_type=F32)
    o_ref[...] = x_ref[...] + g1_ref[...] * y


def _merge_call(o_a, o_b, o_c, p_gate, x2d, g1, wa, wb, wc, wo, seq, tm=512):
    n, d = x2d.shape
    per_b = seq // tm
    row = lambda wd: pl.BlockSpec((tm, wd), lambda i: (i, 0))
    return pl.pallas_call(
        _merge_kernel,
        grid=(n // tm,),
        in_specs=[row(GDN_WIDTH), row(S5_WIDTH), row(MOBA_WIDTH), row(N_BRANCH * d), row(d),
                  pl.BlockSpec((None, 1, d), lambda i: (i // per_b, 0, 0)),
                  _resident(wa.shape), _resident(wb.shape), _resident(wc.shape), _resident(wo.shape)],
        out_specs=row(d),
        out_shape=jax.ShapeDtypeStruct((n, d), F32),
        compiler_params=_cparams("parallel"),
        name="merge_out",
    )(o_a, o_b, o_c, p_gate, x2d, g1, wa, wb, wc, wo)


def _candidate_tables():
    k = PEER_TOPK
    pairs = [(a, b) for a in range(k) for b in range(k) if (a + 1) * (b + 1) <= k]
    rows = -(-len(pairs) // SUBLANES) * SUBLANES
    sel = np.zeros((rows, 2 * k), np.float32)
    sel_id = np.zeros((rows, 2 * k), np.float32)
    bias = np.zeros((rows, 1), np.float32)
    order = np.zeros((rows, 1), np.float32)
    for r, (a, b) in enumerate(pairs):
        sel[r, a] = sel[r, k + b] = 1.0
        sel_id[r, a] = float(PEER_NKEYS)
        sel_id[r, k + b] = 1.0
        order[r, 0] = a * k + b
    for r in range(len(pairs), rows):
        bias[r, 0] = -np.inf
        order[r, 0] = k * k + r
    return sel, sel_id, bias, order


_CAND_SEL, _CAND_SEL_ID, _CAND_BIAS, _CAND_ORDER = _candidate_tables()
_CAND_ROWS = _CAND_SEL.shape[0]


def _extract_max(s, order, big, payload=None):
    m = jnp.max(s, axis=0, keepdims=True)
    pos = jnp.min(jnp.where(s == m, order, big), axis=0, keepdims=True)
    hit = order == pos
    tag = pos if payload is None else jnp.sum(jnp.where(hit, payload, 0.0), axis=0, keepdims=True)
    return m, tag, jnp.where(hit, -jnp.inf, s)


def _peer_query_kernel(x_ref, nw_ref, sc_ref, sh_ref, wqt_ref, k1_ref, k2_ref, sel_ref, selid_ref, cb_ref, co_ref,
                       h_ref, idx_ref, gate_ref, q_scr, s1_scr, s2_scr, v_scr, i_scr, cs_scr, ci_scr, ts_scr, te_scr):
    h = _norm_mod(x_ref[...], nw_ref[...], sc_ref[...], sh_ref[...])
    h_ref[...] = h
    q_scr[...] = lax.dot_general(wqt_ref[...], h.astype(BF16), (((1,), (1,)), ((), ())),
                                 preferred_element_type=F32)
    half = PEER_QDIM // 2
    k = PEER_TOPK
    key_iota = lax.broadcasted_iota(jnp.int32, s1_scr.shape, 0).astype(F32)
    cand_order = jnp.broadcast_to(co_ref[...], cs_scr.shape)

    def head_body(hd, carry):
        base = pl.multiple_of(hd * PEER_QDIM, PEER_QDIM)
        s1_scr[...] = jnp.dot(k1_ref[hd], q_scr[pl.ds(base, half), :], precision=HIGHEST,
                              preferred_element_type=F32)
        s2_scr[...] = jnp.dot(k2_ref[hd], q_scr[pl.ds(base + half, half), :], precision=HIGHEST,
                              preferred_element_type=F32)

        def keys_body(j, c):
            m1, p1, s1 = _extract_max(s1_scr[...], key_iota, float(PEER_NKEYS))
            m2, p2, s2 = _extract_max(s2_scr[...], key_iota, float(PEER_NKEYS))
            s1_scr[...] = s1
            s2_scr[...] = s2
            v_scr[pl.ds(j, 1), :] = m1
            i_scr[pl.ds(j, 1), :] = p1
            v_scr[pl.ds(j + k, 1), :] = m2
            i_scr[pl.ds(j + k, 1), :] = p2
            return c

        lax.fori_loop(0, k, keys_body, 0)
        cs_scr[...] = jnp.dot(sel_ref[...], v_scr[...], precision=HIGHEST, preferred_element_type=F32) + cb_ref[...]
        ci_scr[...] = jnp.dot(selid_ref[...], i_scr[...].astype(BF16), preferred_element_type=F32)

        def cand_body(j, c):
            m, e, s = _extract_max(cs_scr[...], cand_order, float(4 * k * k), payload=ci_scr[...])
            cs_scr[...] = s
            ts_scr[pl.ds(j, 1), :] = m
            te_scr[pl.ds(j, 1), :] = e
            return c

        lax.fori_loop(0, k, cand_body, 0)
        ts = ts_scr[...]
        e = jnp.exp(ts - jnp.max(ts, axis=0, keepdims=True))
        out_base = pl.multiple_of(hd * k, k)
        gate_ref[pl.ds(out_base, k), :] = e / jnp.sum(e, axis=0, keepdims=True)
        idx_ref[pl.ds(out_base, k), :] = te_scr[...].astype(jnp.int32)
        return carry

    lax.fori_loop(0, PEER_HEADS, head_body, 0)


def _peer_query_call(x2d, nw, sc, sh, wqt, k1, k2, seq, tm=256):
    n, d = x2d.shape
    per_b = seq // tm
    nq = PEER_HEADS * PEER_QDIM
    k = PEER_TOPK
    scratch = [pltpu.VMEM((nq, tm), F32), pltpu.VMEM((PEER_NKEYS, tm), F32), pltpu.VMEM((PEER_NKEYS, tm), F32),
               pltpu.VMEM((2 * k, tm), F32), pltpu.VMEM((2 * k, tm), F32),
               pltpu.VMEM((_CAND_ROWS, tm), F32), pltpu.VMEM((_CAND_ROWS, tm), F32),
               pltpu.VMEM((k, tm), F32), pltpu.VMEM((k, tm), F32)]
    cand = [jnp.asarray(_CAND_SEL), jnp.asarray(_CAND_SEL_ID, dtype=BF16), jnp.asarray(_CAND_BIAS),
            jnp.asarray(_CAND_ORDER)]
    return pl.pallas_call(
        _peer_query_kernel,
        grid=(n // tm,),
        in_specs=[
            pl.BlockSpec((tm, d), lambda i: (i, 0)),
            _resident((1, d)),
            pl.BlockSpec((None, 1, d), lambda i: (i // per_b, 0, 0)),
            pl.BlockSpec((None, 1, d), lambda i: (i // per_b, 0, 0)),
            _resident(wqt.shape), _resident(k1.shape), _resident(k2.shape),
        ] + [_resident(t.shape) for t in cand],
        out_specs=[pl.BlockSpec((tm, d), lambda i: (i, 0)),
                   pl.BlockSpec((PEER_SEL, tm), lambda i: (0, i)),
                   pl.BlockSpec((PEER_SEL, tm), lambda i: (0, i))],
        out_shape=[jax.ShapeDtypeStruct((n, d), F32),
                   jax.ShapeDtypeStruct((PEER_SEL, n), jnp.int32),
                   jax.ShapeDtypeStruct((PEER_SEL, n), F32)],
        scratch_shapes=scratch,
        compiler_params=_cparams("parallel"),
        name="peer_query",
    )(x2d, nw, sc, sh, wqt, k1, k2, *cand)


PEER_TOKENS_PER_STEP = 8
_ROW_SUB = 2 * SUBLANES


def _fold_pair(a, b, shift, keep_a):
    fa = a + pltpu.roll(a, shift, axis=0)
    fb = b + pltpu.roll(b, SUBLANES - shift, axis=0)
    return jnp.where(keep_a, fa, fb)


def _slot_order():
    tiles = [[k] * SUBLANES for k in range(PEER_SEL)]
    for shift, keep in ((4, [j < 4 for j in range(8)]), (2, [(j % 4) >= 2 for j in range(8)]),
                        (1, [(j % 2) == 1 for j in range(8)])):
        tiles = [[tiles[2 * m][j] if keep[j] else tiles[2 * m + 1][j] for j in range(8)]
                 for m in range(len(tiles) // 2)]
    return [s for t in tiles for s in t]


_SLOT_AT = _slot_order()
_POS_OF = [0] * PEER_SEL
for _p, _s in enumerate(_SLOT_AT):
    _POS_OF[_s] = _p


def _peer_gather_kernel(idx0_ref, idxc_ref, idxn_ref, g_ref, h_ref, x_ref, g2_ref, uv_hbm, o_ref,
                        buf0, buf1, wb_scr, sems):
    tb = PEER_TOKENS_PER_STEP
    blk = tb * PEER_SEL
    i = pl.program_id(0)
    bufs = (buf0, buf1)

    def row_copy(e, half, row, t):
        return pltpu.make_async_copy(uv_hbm.at[e], bufs[half].at[row], sems.at[half * tb + t])

    def slot_copy(half, t):
        return pltpu.make_async_copy(uv_hbm.at[pl.ds(0, PEER_SEL)], bufs[half].at[pl.ds(t * PEER_SEL, PEER_SEL)],
                                     sems.at[half * tb + t])

    @pl.when(i == 0)
    def _():
        def issue0(t, c):
            for k in range(PEER_SEL):
                row_copy(idx0_ref[0, t * PEER_SEL + k], 0, t * PEER_SEL + k, t).start()
            return c
        lax.fori_loop(0, tb, issue0, 0)

    sub = lax.broadcasted_iota(jnp.int32, (SUBLANES, LANES), 0)
    keep4 = sub < 4
    keep2 = (sub % 4) >= 2
    keep1 = (sub % 2) == 1
    eye = (lax.broadcasted_iota(jnp.int32, (PEER_SEL, LANES), 0)
           == lax.broadcasted_iota(jnp.int32, (PEER_SEL, LANES), 1))

    for half in range(2):
        nidx_ref, noff = (idxc_ref, blk) if half == 0 else (idxn_ref, 0)
        buf = bufs[half]
        for t in range(tb):
            tt = half * tb + t
            slot_copy(half, t).wait()
            for k in range(PEER_SEL):
                row_copy(nidx_ref[0, noff + t * PEER_SEL + k], 1 - half, t * PEER_SEL + k, t).start()
            base = t * PEER_SEL
            ht = h_ref[tt]
            tiles = [buf[base + k, 0:SUBLANES, :] * ht for k in range(PEER_SEL)]
            tiles = [_fold_pair(tiles[2 * m], tiles[2 * m + 1], 4, keep4) for m in range(PEER_SEL // 2)]
            tiles = [_fold_pair(tiles[2 * m], tiles[2 * m + 1], 2, keep2) for m in range(PEER_SEL // 4)]
            tiles = [_fold_pair(tiles[2 * m], tiles[2 * m + 1], 1, keep1) for m in range(PEER_SEL // 8)]
            part = jnp.concatenate(tiles, axis=0)
            s = jnp.sum(part, axis=-1, keepdims=True)
            act = 0.5 * s * (1.0 + lax.erf(s * (2.0 ** -0.5)))
            grow = jnp.broadcast_to(g_ref[tt:tt + 1, :], (PEER_SEL, LANES))
            gcol = jnp.sum(jnp.where(eye, grow, 0.0), axis=-1, keepdims=True)
            wbase = tt * PEER_SEL
            wb_scr[wbase:wbase + PEER_SEL, :] = jnp.broadcast_to(gcol * act, (PEER_SEL, LANES))
            accs = [jnp.zeros((SUBLANES, LANES), F32) for _ in range(4)]
            for k in range(PEER_SEL):
                p = wbase + _POS_OF[k]
                wk = jnp.broadcast_to(wb_scr[p:p + 1, :], (SUBLANES, LANES))
                accs[k % 4] = accs[k % 4] + wk * buf[base + k, SUBLANES:_ROW_SUB, :]
            y = (accs[0] + accs[1]) + (accs[2] + accs[3])
            o_ref[tt] = x_ref[tt] + g2_ref[...] * y

    @pl.when(i == pl.num_programs(0) - 1)
    def _():
        for t in range(tb):
            slot_copy(0, t).wait()


def _peer_gather_call(idx, gate_p, h3, x3, g2_3, uv, seq):
    n = idx.shape[0]
    tb = PEER_TOKENS_PER_STEP
    ns = n // (2 * tb)
    per_b = seq // (2 * tb)
    idx3 = idx.reshape(ns, 1, 2 * tb * PEER_SEL)
    smem_blk = lambda fn: pl.BlockSpec((None, 1, 2 * tb * PEER_SEL), fn, memory_space=pltpu.SMEM)
    tok3 = pl.BlockSpec((2 * tb, SUBLANES, LANES), lambda i: (i, 0, 0))
    return pl.pallas_call(
        _peer_gather_kernel,
        grid=(ns,),
        in_specs=[
            smem_blk(lambda i: (0, 0, 0)),
            smem_blk(lambda i: (i, 0, 0)),
            smem_blk(lambda i: (jnp.minimum(i + 1, ns - 1), 0, 0)),
            pl.BlockSpec((2 * tb, PEER_SEL), lambda i: (i, 0)),
            tok3, tok3,
            pl.BlockSpec((None, SUBLANES, LANES), lambda i: (i // per_b, 0, 0)),
            pl.BlockSpec(memory_space=pl.ANY),
        ],
        out_specs=tok3,
        out_shape=jax.ShapeDtypeStruct((n, SUBLANES, LANES), F32),
        scratch_shapes=[pltpu.VMEM((tb * PEER_SEL, _ROW_SUB, LANES), F32),
                        pltpu.VMEM((tb * PEER_SEL, _ROW_SUB, LANES), F32),
                        pltpu.VMEM((2 * tb * PEER_SEL, LANES), F32),
                        pltpu.SemaphoreType.DMA((2 * tb,))],
        compiler_params=_cparams("arbitrary"),
        name="peer_gather",
    )(idx3, idx3, idx3, gate_p, h3, x3, g2_3, uv)


def _final_norm_kernel(x_ref, w_ref, o_ref):
    x = x_ref[...]
    o_ref[...] = x * lax.rsqrt(jnp.mean(x * x, axis=-1, keepdims=True) + RMS_EPS) * w_ref[...]


def _final_norm_call(x2d, w, tm=1024):
    n, d = x2d.shape
    return pl.pallas_call(
        _final_norm_kernel,
        grid=(n // tm,),
        in_specs=[pl.BlockSpec((tm, d), lambda i: (i, 0)), _resident((1, d))],
        out_specs=pl.BlockSpec((tm, d), lambda i: (i, 0)),
        out_shape=jax.ShapeDtypeStruct((n, d), F32),
        compiler_params=_cparams("parallel"),
        name="final_norm",
    )(x2d, w)


S5_LANES = S5_GROUPS * S5_STATE


def _s5_disc_kernel(are_ref, aim_ref, ldt_ref, bre_ref, bim_ref, ar_ref, ai_ref, br_ref, bi_ref):
    a_re = are_ref[...]
    a_im = aim_ref[...]
    dt = jnp.exp(ldt_ref[...])
    mag = jnp.exp(a_re * dt)
    ar = mag * jnp.cos(a_im * dt)
    ai = mag * jnp.sin(a_im * dt)
    den = a_re * a_re + a_im * a_im
    cr = ((ar - 1.0) * a_re + ai * a_im) / den
    ci = (ai * a_re - (ar - 1.0) * a_im) / den
    ar_ref[...] = ar
    ai_ref[...] = ai
    br_ref[...] = cr * bre_ref[...] - ci * bim_ref[...]
    bi_ref[...] = cr * bim_ref[...] + ci * bre_ref[...]


def _s5_disc_call(a_re, a_im, log_dt, b_re, b_im):
    rows = S5_LANES
    col = lambda t: t.reshape(rows, 1)
    ldt = jnp.broadcast_to(log_dt[:, None], (S5_GROUPS, S5_STATE))
    out = [jax.ShapeDtypeStruct((rows, 1), F32)] * 2 + [jax.ShapeDtypeStruct((rows, S5_GROUP), F32)] * 2
    return pl.pallas_call(_s5_disc_kernel, out_shape=out, name="s5_disc")(
        col(a_re), col(a_im), col(ldt), b_re.reshape(rows, S5_GROUP), b_im.reshape(rows, S5_GROUP))


def _s5_kernel(u_ref, bcat_ref, ar_ref, ai_ref, ccat_ref, d_ref, gw_ref, gb_ref, o_ref, st_scr, bu_scr, *, steps, bt):
    @pl.when(pl.program_id(0) == 0)
    def _():
        st_scr[...] = jnp.zeros_like(st_scr)

    u = u_ref[...]
    bu_scr[...] = jnp.dot(u, bcat_ref[...], precision=HIGHEST, preferred_element_type=F32)
    a_re = jnp.broadcast_to(ar_ref[...], (bt, S5_LANES))
    a_im = jnp.broadcast_to(ai_ref[...], (bt, S5_LANES))

    def step(t, carry):
        s_re, s_im = carry
        r = pl.multiple_of(t * bt, bt)
        n_re = a_re * s_re - a_im * s_im + bu_scr[pl.ds(r, bt), 0:S5_LANES]
        n_im = a_re * s_im + a_im * s_re + bu_scr[pl.ds(r, bt), S5_LANES:2 * S5_LANES]
        bu_scr[pl.ds(r, bt), 0:S5_LANES] = n_re
        bu_scr[pl.ds(r, bt), S5_LANES:2 * S5_LANES] = n_im
        return n_re, n_im

    s_re, s_im = lax.fori_loop(0, steps, step, (st_scr[:, 0:S5_LANES], st_scr[:, S5_LANES:2 * S5_LANES]))
    st_scr[:, 0:S5_LANES] = s_re
    st_scr[:, S5_LANES:2 * S5_LANES] = s_im
    y = jnp.dot(bu_scr[...], ccat_ref[...], precision=HIGHEST, preferred_element_type=F32) + d_ref[...] * u
    zg = 0.5 * y * (1.0 + lax.erf(y * (2.0 ** -0.5)))
    zz = jnp.dot(zg.astype(BF16), gw_ref[...], preferred_element_type=F32) + gb_ref[...]
    o_ref[...] = zz[:, 0:S5_WIDTH] * jax.nn.sigmoid(zz[:, S5_WIDTH:2 * S5_WIDTH])


def _s5_call(u_tb, bcat, abar_re, abar_im, ccat, d, glu_w, glu_b, bt, steps=64):
    rows = u_tb.shape[0]
    blk = steps * bt
    return pl.pallas_call(
        functools.partial(_s5_kernel, steps=steps, bt=bt),
        grid=(rows // blk,),
        in_specs=[pl.BlockSpec((blk, S5_WIDTH), lambda i: (i, 0)),
                  _resident(bcat.shape), _resident(abar_re.shape), _resident(abar_im.shape), _resident(ccat.shape),
                  _resident(d.shape), _resident(glu_w.shape), _resident(glu_b.shape)],
        out_specs=pl.BlockSpec((blk, S5_WIDTH), lambda i: (i, 0)),
        out_shape=jax.ShapeDtypeStruct((rows, S5_WIDTH), F32),
        scratch_shapes=[pltpu.VMEM((bt, 2 * S5_LANES), F32), pltpu.VMEM((blk, 2 * S5_LANES), F32)],
        compiler_params=_cparams("arbitrary"),
        name="s5_scan",
    )(u_tb, bcat, abar_re, abar_im, ccat, d, glu_w, glu_b)


def _s5_branch_pallas(p_ub, bt, seq, a_re, a_im, b_re, b_im, c_re, c_im, d, log_dt, glu_w, glu_b):
    ar, ai, br, bi = _s5_disc_call(a_re, a_im, log_dt, b_re, b_im)
    g_ix = jnp.arange(S5_GROUPS)
    def blockdiag_in(b):
        b = b.reshape(S5_GROUPS, S5_STATE, S5_GROUP)
        full = jnp.zeros((S5_GROUPS, S5_GROUP, S5_GROUPS, S5_STATE), F32)
        return full.at[g_ix, :, g_ix, :].set(b.transpose(0, 2, 1)).reshape(S5_WIDTH, S5_LANES)

    def blockdiag_out(c):
        full = jnp.zeros((S5_GROUPS, S5_STATE, S5_GROUPS, S5_GROUP), F32)
        return full.at[g_ix, :, g_ix, :].set(c.transpose(0, 2, 1)).reshape(S5_LANES, S5_WIDTH)

    bcat = jnp.concatenate([blockdiag_in(br), blockdiag_in(bi)], axis=1)
    ccat = jnp.concatenate([blockdiag_out(c_re), -blockdiag_out(c_im)], axis=0)
    u_tb = p_ub.reshape(bt, seq, S5_WIDTH).transpose(1, 0, 2).reshape(seq * bt, S5_WIDTH)
    o_tb = _s5_call(u_tb, bcat, ar.reshape(1, S5_LANES), ai.reshape(1, S5_LANES), ccat, d.reshape(1, S5_WIDTH),
                    glu_w.astype(BF16), glu_b.reshape(1, 2 * S5_WIDTH), bt)
    return o_tb.reshape(seq, bt, S5_WIDTH).transpose(1, 0, 2).reshape(bt * seq, S5_WIDTH)


def _mm(a, b):
    return jnp.dot(a.astype(BF16), b.astype(BF16), preferred_element_type=F32)


def _mm_nt(a, b, precision=None):
    if precision is None:
        a, b = a.astype(BF16), b.astype(BF16)
    return lax.dot_general(a, b, (((1,), (1,)), ((), ())), precision=precision, preferred_element_type=F32)


def _mm_tn(a, b):
    return lax.dot_general(a.astype(BF16), b.astype(BF16), (((0,), (0,)), ((), ())), preferred_element_type=F32)


def _softplus(x):
    return jnp.maximum(x, 0.0) + jnp.log1p(jnp.exp(-jnp.abs(x)))


def _gdn_kernel(pa_ref, ba_ref, cw_ref, alog_ref, dtb_ref, nw_ref, o_ref,
                st_scr, carry_scr, q_scr, k_scr, v_scr, g_scr, b_scr, u_scr, w_scr, qk_scr, gi_scr):
    j = pl.program_id(1)
    t_blk = pa_ref.shape[0]
    c_len, wid, dh = GDN_CHUNK, GDN_WIDTH, GDN_HEAD_DIM

    @pl.when(j == 0)
    def _():
        st_scr[...] = jnp.zeros_like(st_scr)
        carry_scr[...] = jnp.zeros_like(carry_scr)

    x = pa_ref[:, 0:3 * wid]
    xx = jnp.concatenate([carry_scr[...], x], axis=0)
    conv = cw_ref[GDN_CONV - 1:GDN_CONV, :] * x
    for k in range(1, GDN_CONV):
        conv = conv + cw_ref[GDN_CONV - 1 - k:GDN_CONV - k, :] * pltpu.roll(xx, k, axis=0)[SUBLANES:, :]
    carry_scr[...] = x[t_blk - SUBLANES:t_blk, :]
    act = conv * jax.nn.sigmoid(conv)
    for h in range(GDN_HEADS):
        qh = act[:, h * dh:(h + 1) * dh]
        kh = act[:, wid + h * dh:wid + (h + 1) * dh]
        q_scr[:, h * dh:(h + 1) * dh] = qh * lax.rsqrt(jnp.sum(qh * qh, axis=-1, keepdims=True) + 1e-6) * (dh ** -0.5)
        k_scr[:, h * dh:(h + 1) * dh] = kh * lax.rsqrt(jnp.sum(kh * kh, axis=-1, keepdims=True) + 1e-6)
    v_scr[...] = act[:, 2 * wid:3 * wid]
    ba = ba_ref[...]
    g_scr[...] = -jnp.exp(alog_ref[...]) * _softplus(ba + dtb_ref[...])
    b_scr[...] = jax.nn.sigmoid(ba)

    row = lax.broadcasted_iota(jnp.int32, (t_blk, t_blk), 0)
    col = lax.broadcasted_iota(jnp.int32, (t_blk, t_blk), 1)
    same = (row // c_len) == (col // c_len)
    tril = same & (row >= col)
    strict = same & (row > col)
    eye = jnp.where(row == col, 1.0, 0.0)
    lane = lax.broadcasted_iota(jnp.int32, (t_blk, LANES), 1)
    nw = nw_ref[...]
    gcum = jnp.dot(jnp.where(tril, 1.0, 0.0), g_scr[...], precision=HIGHEST, preferred_element_type=F32)
    beta_all = b_scr[...]
    for h in range(GDN_HEADS):
        hs = slice(h * dh, (h + 1) * dh)
        gi = gcum[:, GDN_HEADS + h:GDN_HEADS + h + 1]
        beta = beta_all[:, h:h + 1]
        q_h, k_h, v_h = q_scr[:, hs], k_scr[:, hs], v_scr[:, hs]
        g1 = jnp.where(lane == 0, gi, jnp.where(lane == 1, 1.0, 0.0))
        g2 = jnp.where(lane == 0, 1.0, jnp.where(lane == 1, -gi, 0.0))
        diff = _mm_nt(g1, g2, precision=HIGHEST)
        decay = jnp.where(tril, jnp.exp(jnp.where(tril, diff, 0.0)), 0.0)
        kb = k_h * beta
        a = jnp.where(strict, _mm_nt(kb, k_h) * decay, 0.0)
        p = -a
        t = eye + p
        for _ in range(int(math.log2(c_len)) - 1):
            p = _mm(p, p)
            t = t + _mm(t, p)
        e_g = jnp.exp(gi)
        u_scr[:, hs] = _mm(t, v_h * beta)
        w_scr[:, hs] = _mm(t, kb * e_g)
        qk_scr[h] = jnp.where(tril, _mm_nt(q_h, k_h) * decay, 0.0)
        q_scr[:, hs] = q_h * e_g
        gi_scr[:, h:h + 1] = gi

    for c in range(t_blk // c_len):
        rows = slice(c * c_len, (c + 1) * c_len)
        for h in range(GDN_HEADS):
            hs = slice(h * dh, (h + 1) * dh)
            gi = gi_scr[rows, h:h + 1]
            state = st_scr[h]
            v_new = u_scr[rows, hs] - _mm(w_scr[rows, hs], state)
            o = _mm(q_scr[rows, hs], state) + _mm(qk_scr[h, rows, rows], v_new)
            g_last = gi[c_len - 1:c_len, :]
            st_scr[h] = state * jnp.exp(g_last) + _mm_tn(k_scr[rows, hs] * jnp.exp(g_last - gi), v_new)
            z = pa_ref[rows, 3 * wid + h * dh:3 * wid + (h + 1) * dh]
            o_ref[rows, hs] = (o * lax.rsqrt(jnp.mean(o * o, axis=-1, keepdims=True) + RMS_EPS) * nw
                               * (z * jax.nn.sigmoid(z)))


def _gdn_branch_pallas(p_a, p_ba, conv_w, a_log, dt_bias, norm_w, bt, seq, t_blk=256):
    n = p_a.shape[0]
    nj = seq // t_blk
    wid = GDN_WIDTH
    lane_pad = lambda v: jnp.zeros((1, LANES), F32).at[0, GDN_HEADS:2 * GDN_HEADS].set(v)
    rows = lambda wd: pl.BlockSpec((t_blk, wd), lambda b, j: (b * nj + j, 0))
    return pl.pallas_call(
        _gdn_kernel,
        grid=(bt, nj),
        in_specs=[rows(4 * wid), rows(LANES), _resident((GDN_CONV, 3 * wid)), _resident((1, LANES)),
                  _resident((1, LANES)), _resident((1, GDN_HEAD_DIM))],
        out_specs=rows(wid),
        out_shape=jax.ShapeDtypeStruct((n, wid), F32),
        scratch_shapes=[pltpu.VMEM((GDN_HEADS, GDN_HEAD_DIM, GDN_HEAD_DIM), F32), pltpu.VMEM((SUBLANES, 3 * wid), F32),
                        pltpu.VMEM((t_blk, wid), F32), pltpu.VMEM((t_blk, wid), F32), pltpu.VMEM((t_blk, wid), F32),
                        pltpu.VMEM((t_blk, LANES), F32), pltpu.VMEM((t_blk, LANES), F32),
                        pltpu.VMEM((t_blk, wid), F32), pltpu.VMEM((t_blk, wid), F32),
                        pltpu.VMEM((GDN_HEADS, t_blk, t_blk), F32), pltpu.VMEM((t_blk, LANES), F32)],
        compiler_params=_cparams("parallel", "arbitrary"),
        name="gdn_delta",
    )(p_a, p_ba, conv_w, lane_pad(a_log), lane_pad(dt_bias), norm_w.reshape(1, GDN_HEAD_DIM))


def _moba_kernel(qt_ref, k_ref, vt_ref, o_ref, km_scr, sel_scr):
    i = pl.program_id(2)
    nblk = k_ref.shape[0]

    @pl.when(i == 0)
    def _():
        for n in range(nblk):
            km_scr[n:n + 1, :] = jnp.mean(k_ref[n], axis=0, keepdims=True)

    qt = qt_ref[...]
    gate = jnp.dot(km_scr[...], qt, precision=HIGHEST, preferred_element_type=F32)
    blk = lax.broadcasted_iota(jnp.int32, gate.shape, 0)
    cnt = jnp.zeros(gate.shape, F32)
    for m in range(nblk):
        gm = gate[m:m + 1, :]
        beats = (gm > gate) | ((gm == gate) & (m < blk))
        cnt = cnt + jnp.where(beats & (m < i), 1.0, 0.0)
    sel_scr[...] = jnp.where((blk < i) & (cnt < float(MOBA_TOPK)), 1.0, 0.0)

    scale = MOBA_HEAD_DIM ** -0.5
    qb = qt.astype(BF16)
    s = jnp.dot(k_ref[i].astype(BF16), qb, preferred_element_type=F32) * scale
    kpos = lax.broadcasted_iota(jnp.int32, s.shape, 0)
    qpos = lax.broadcasted_iota(jnp.int32, s.shape, 1)
    s = jnp.where(kpos <= qpos, s, NEG_INF)
    m0 = jnp.max(s, axis=0, keepdims=True)
    p = jnp.exp(s - m0)
    l0 = jnp.sum(p, axis=0, keepdims=True)
    acc0 = jnp.dot(vt_ref[i].astype(BF16), p.astype(BF16), preferred_element_type=F32)

    def body(n, carry):
        m, l, acc = carry
        s = jnp.dot(k_ref[n].astype(BF16), qb, preferred_element_type=F32) * scale
        s = jnp.where(sel_scr[pl.ds(n, 1), :] > 0.5, s, NEG_INF)
        m_new = jnp.maximum(m, jnp.max(s, axis=0, keepdims=True))
        p = jnp.exp(s - m_new)
        alpha = jnp.exp(m - m_new)
        l = alpha * l + jnp.sum(p, axis=0, keepdims=True)
        acc = alpha * acc + jnp.dot(vt_ref[n].astype(BF16), p.astype(BF16), preferred_element_type=F32)
        return m_new, l, acc

    m, l, acc = lax.fori_loop(0, i, body, (m0, l0, acc0))
    o_ref[...] = acc / l


def _moba_branch_pallas(p_c, bt, seq):
    nh, dh, bs = MOBA_HEADS, MOBA_HEAD_DIM, MOBA_BLOCK
    assert seq % bs == 0
    nblk = seq // bs
    qkv = p_c.reshape(bt, seq, 3, nh, dh)
    qt = qkv[:, :, 0].transpose(0, 2, 3, 1)
    k5 = qkv[:, :, 1].reshape(bt, nblk, bs, nh, dh).transpose(0, 3, 1, 2, 4)
    vt5 = qkv[:, :, 2].reshape(bt, nblk, bs, nh, dh).transpose(0, 3, 1, 4, 2)
    ot = pl.pallas_call(
        _moba_kernel,
        grid=(bt, nh, nblk),
        in_specs=[pl.BlockSpec((None, None, dh, bs), lambda b, h, i: (b, h, 0, i)),
                  pl.BlockSpec((None, None, nblk, bs, dh), lambda b, h, i: (b, h, 0, 0, 0)),
                  pl.BlockSpec((None, None, nblk, dh, bs), lambda b, h, i: (b, h, 0, 0, 0))],
        out_specs=pl.BlockSpec((None, None, dh, bs), lambda b, h, i: (b, h, 0, i)),
        out_shape=jax.ShapeDtypeStruct((bt, nh, dh, seq), F32),
        scratch_shapes=[pltpu.VMEM((nblk, dh), F32), pltpu.VMEM((nblk, bs), F32)],
        compiler_params=_cparams("parallel", "parallel", "arbitrary"),
        name="moba_attn",
    )(qt, k5, vt5)
    return ot.transpose(0, 3, 1, 2).reshape(bt * seq, nh * dh)


def kernel(x, c, ada_w, ada_b, norm1_w, w_in, gdn_conv_w, gdn_a_log, gdn_dt_bias, gdn_norm_w, s5_a_re, s5_a_im, s5_b_re, s5_b_im, s5_c_re, s5_c_im, s5_d, s5_log_dt, s5_glu_w, s5_glu_b, w_branch_a, w_branch_b, w_branch_c, w_out, norm2_w, peer_wq, peer_k1, peer_k2, peer_u, peer_v, final_norm_w):
    bt, seq, d = x.shape
    n = bt * seq
    depth = ada_w.shape[0]
    x2d = x.reshape(n, d)
    mod = _ada_call(c, ada_w, ada_b)
    gate_perm = np.asarray(_SLOT_AT, np.int32)
    for l in range(depth):
        sh1, sc1, g1, sh2, sc2, g2 = (mod[l, :, j * d:(j + 1) * d].reshape(bt, 1, d) for j in range(6))
        p_a, p_ub, p_c, p_gate, p_ba = _in_proj_call(
            x2d, norm1_w[l].reshape(1, d), sc1, sh1, _arrange_w_in(w_in[l]), seq)
        o_a = _gdn_branch_pallas(p_a, p_ba, gdn_conv_w[l], gdn_a_log[l], gdn_dt_bias[l], gdn_norm_w[l], bt, seq)
        o_b = _s5_branch_pallas(p_ub, bt, seq, s5_a_re[l], s5_a_im[l], s5_b_re[l], s5_b_im[l], s5_c_re[l],
                                s5_c_im[l], s5_d[l], s5_log_dt[l], s5_glu_w[l], s5_glu_b[l])
        o_c = _moba_branch_pallas(p_c, bt, seq)
        x2d = _merge_call(o_a, o_b, o_c, p_gate, x2d, g1, w_branch_a[l].astype(BF16), w_branch_b[l].astype(BF16),
                          w_branch_c[l].astype(BF16), w_out[l].astype(BF16), seq)
        h2, idx_t, gate_t = _peer_query_call(x2d, norm2_w[l].reshape(1, d), sc2, sh2,
                                             peer_wq[l].T.astype(BF16), peer_k1[l], peer_k2[l], seq)
        uv = jnp.concatenate([peer_u[l].reshape(-1, SUBLANES, LANES), peer_v[l].reshape(-1, SUBLANES, LANES)], axis=1)
        x3 = _peer_gather_call(idx_t.T, gate_t.T[:, gate_perm], h2.reshape(n, SUBLANES, LANES),
                               x2d.reshape(n, SUBLANES, LANES), g2.reshape(bt, SUBLANES, LANES), uv, seq)
        x2d = x3.reshape(n, d)
    return _final_norm_call(x2d, final_norm_w.reshape(1, d)).reshape(bt, seq, d)
```

```python
import functools
import math

import jax
import jax.numpy as jnp
import numpy as np
from jax import lax
from jax.experimental import pallas as pl
from jax.experimental.pallas import tpu as pltpu

F32 = jnp.float32
BF16 = jnp.bfloat16
HIGHEST = lax.Precision.HIGHEST

D_MODEL = 1024
GDN_HEADS = 4
GDN_HEAD_DIM = 128
GDN_WIDTH = GDN_HEADS * GDN_HEAD_DIM
GDN_CONV = 4
GDN_CHUNK = 64
S5_GROUP = 16
S5_GROUPS = 16
S5_WIDTH = S5_GROUPS * S5_GROUP
S5_STATE = 64
MOBA_HEADS = 4
MOBA_HEAD_DIM = 64
MOBA_WIDTH = MOBA_HEADS * MOBA_HEAD_DIM
MOBA_BLOCK = 256
MOBA_TOPK = 3
MOBA_Q_CHUNK = 64
N_BRANCH = 3
PEER_HEADS = 8
PEER_NKEYS = 128
PEER_QDIM = 256
PEER_TOPK = 16
PEER_SEL = PEER_HEADS * PEER_TOPK
RMS_EPS = 1e-6
NEG_INF = -1e30

SUBLANES = 8
LANES = 128
VMEM_LIMIT_BYTES = 56 * 1024 * 1024

_OFF_QKV_A = 0
_OFF_Z_A = 3 * GDN_WIDTH
_OFF_BETA = _OFF_Z_A + GDN_WIDTH
_OFF_ALPHA = _OFF_BETA + GDN_HEADS
_OFF_UB = _OFF_ALPHA + GDN_HEADS
_OFF_QKV_C = _OFF_UB + S5_WIDTH
_OFF_GATE = _OFF_QKV_C + 3 * MOBA_WIDTH
_IN_COLS = _OFF_GATE + N_BRANCH * D_MODEL


def _cparams(*sem):
    return pltpu.CompilerParams(dimension_semantics=sem, vmem_limit_bytes=VMEM_LIMIT_BYTES)


def _resident(shape):
    nd = len(shape)
    return pl.BlockSpec(shape, lambda *_: (0,) * nd)


def _ada_kernel(c_ref, w_ref, b_ref, o_ref):
    c = c_ref[...]
    sc = c * jax.nn.sigmoid(c)
    o_ref[...] = jnp.dot(sc, w_ref[...], precision=HIGHEST, preferred_element_type=F32) + b_ref[...]


def _ada_call(c, ada_w, ada_b):
    depth, d, d6 = ada_w.shape
    bt = c.shape[0]
    nj = d6 // d
    return pl.pallas_call(
        _ada_kernel,
        grid=(depth, nj),
        in_specs=[
            pl.BlockSpec((bt, d), lambda l, j: (0, 0)),
            pl.BlockSpec((None, d, d), lambda l, j: (l, 0, j)),
            pl.BlockSpec((None, 1, d), lambda l, j: (l, 0, j)),
        ],
        out_specs=pl.BlockSpec((None, bt, d), lambda l, j: (l, 0, j)),
        out_shape=jax.ShapeDtypeStruct((depth, bt, d6), F32),
        compiler_params=_cparams("parallel", "parallel"),
        name="ada_mod",
    )(c, ada_w, ada_b.reshape(depth, 1, d6))


def _norm_mod(x, nw, sc, sh):
    y = x * lax.rsqrt(jnp.mean(x * x, axis=-1, keepdims=True) + RMS_EPS)
    return (y * nw) * (1.0 + sc) + sh


_IN_SPLITS = (4 * GDN_WIDTH, S5_WIDTH, 3 * MOBA_WIDTH, N_BRANCH * D_MODEL, LANES)


def _in_proj_kernel(x_ref, nw_ref, sc_ref, sh_ref, w_ref, oa_ref, ob_ref, oc_ref, og_ref, oba_ref):
    h = _norm_mod(x_ref[...], nw_ref[...], sc_ref[...], sh_ref[...]).astype(BF16)
    off = 0
    for o_ref, width in zip((oa_ref, ob_ref, oc_ref, og_ref, oba_ref), _IN_SPLITS):
        o_ref[...] = jnp.dot(h, w_ref[:, off:off + width], preferred_element_type=F32)
        off += width


def _in_proj_call(x2d, nw, sc, sh, w_r, seq, tm=256):
    n, d = x2d.shape
    per_b = seq // tm
    wcols = w_r.shape[1]
    outs = [jax.ShapeDtypeStruct((n, wd), F32) for wd in _IN_SPLITS]
    return pl.pallas_call(
        _in_proj_kernel,
        grid=(n // tm,),
        in_specs=[
            pl.BlockSpec((tm, d), lambda i: (i, 0)),
            _resident((1, d)),
            pl.BlockSpec((None, 1, d), lambda i: (i // per_b, 0, 0)),
            pl.BlockSpec((None, 1, d), lambda i: (i // per_b, 0, 0)),
            _resident((d, wcols)),
        ],
        out_specs=[pl.BlockSpec((tm, wd), lambda i: (i, 0)) for wd in _IN_SPLITS],
        out_shape=outs,
        compiler_params=_cparams("parallel"),
        name="in_proj",
    )(x2d, nw, sc, sh, w_r)


def _arrange_w_in(w_in_l):
    pad = jnp.zeros((w_in_l.shape[0], LANES - 2 * GDN_HEADS), w_in_l.dtype)
    return jnp.concatenate([
        w_in_l[:, _OFF_QKV_A:_OFF_BETA],
        w_in_l[:, _OFF_UB:_OFF_QKV_C],
        w_in_l[:, _OFF_QKV_C:_OFF_GATE],
        w_in_l[:, _OFF_GATE:_IN_COLS],
        w_in_l[:, _OFF_BETA:_OFF_UB], pad,
    ], axis=1).astype(BF16)


def _merge_kernel(oa_ref, ob_ref, oc_ref, gt_ref, x_ref, g1_ref, wa_ref, wb_ref, wc_ref, wo_ref, o_ref):
    d = D_MODEL
    ya = jnp.dot(oa_ref[...].astype(BF16), wa_ref[...], preferred_element_type=F32)
    yb = jnp.dot(ob_ref[...].astype(BF16), wb_ref[...], preferred_element_type=F32)
    yc = jnp.dot(oc_ref[...].astype(BF16), wc_ref[...], preferred_element_type=F32)
    merged = (jax.nn.sigmoid(gt_ref[:, 0:d]) * ya + jax.nn.sigmoid(gt_ref[:, d:2 * d]) * yb
              + jax.nn.sigmoid(gt_ref[:, 2 * d:3 * d]) * yc)
    y = jnp.dot(merged.astype(BF16), wo_ref[...], preferred_element_type=F32)
    o_ref[...] = x_ref[...] + g1_ref[...] * y


def _merge_call(o_a, o_b, o_c, p_gate, x2d, g1, wa, wb, wc, wo, seq, tm=512):
    n, d = x2d.shape
    per_b = seq // tm
    row = lambda wd: pl.BlockSpec((tm, wd), lambda i: (i, 0))
    return pl.pallas_call(
        _merge_kernel,
        grid=(n // tm,),
        in_specs=[row(GDN_WIDTH), row(S5_WIDTH), row(MOBA_WIDTH), row(N_BRANCH * d), row(d),
                  pl.BlockSpec((None, 1, d), lambda i: (i // per_b, 0, 0)),
                  _resident(wa.shape), _resident(wb.shape), _resident(wc.shape), _resident(wo.shape)],
        out_specs=row(d),
        out_shape=jax.ShapeDtypeStruct((n, d), F32),
        compiler_params=_cparams("parallel"),
        name="merge_out",
    )(o_a, o_b, o_c, p_gate, x2d, g1, wa, wb, wc, wo)


def _candidate_tables():
    k = PEER_TOPK
    pairs = [(a, b) for a in range(k) for b in range(k) if (a + 1) * (b + 1) <= k]
    rows = -(-len(pairs) // SUBLANES) * SUBLANES
    sel = np.zeros((rows, 2 * k), np.float32)
    sel_id = np.zeros((rows, 2 * k), np.float32)
    bias = np.zeros((rows, 1), np.float32)
    order = np.zeros((rows, 1), np.float32)
    for r, (a, b) in enumerate(pairs):
        sel[r, a] = sel[r, k + b] = 1.0
        sel_id[r, a] = float(PEER_NKEYS)
        sel_id[r, k + b] = 1.0
        order[r, 0] = a * k + b
    for r in range(len(pairs), rows):
        bias[r, 0] = -np.inf
        order[r, 0] = k * k + r
    return sel, sel_id, bias, order


_CAND_SEL, _CAND_SEL_ID, _CAND_BIAS, _CAND_ORDER = _candidate_tables()
_CAND_ROWS = _CAND_SEL.shape[0]


def _extract_max(s, order, big, payload=None):
    m = jnp.max(s, axis=0, keepdims=True)
    pos = jnp.min(jnp.where(s == m, order, big), axis=0, keepdims=True)
    hit = order == pos
    tag = pos if payload is None else jnp.sum(jnp.where(hit, payload, 0.0), axis=0, keepdims=True)
    return m, tag, jnp.where(hit, -jnp.inf, s)


def _peer_query_kernel(x_ref, nw_ref, sc_ref, sh_ref, wqt_ref, k1_ref, k2_ref, sel_ref, selid_ref, cb_ref, co_ref,
                       h_ref, idx_ref, gate_ref, q_scr, s1_scr, s2_scr, v_scr, i_scr, cs_scr, ci_scr, ts_scr, te_scr):
    h = _norm_mod(x_ref[...], nw_ref[...], sc_ref[...], sh_ref[...])
    h_ref[...] = h
    q_scr[...] = lax.dot_general(wqt_ref[...], h.astype(BF16), (((1,), (1,)), ((), ())),
                                 preferred_element_type=F32)
    half = PEER_QDIM // 2
    k = PEER_TOPK
    key_iota = lax.broadcasted_iota(jnp.int32, s1_scr.shape, 0).astype(F32)
    cand_order = jnp.broadcast_to(co_ref[...], cs_scr.shape)

    def head_body(hd, carry):
        base = pl.multiple_of(hd * PEER_QDIM, PEER_QDIM)
        s1_scr[...] = jnp.dot(k1_ref[hd], q_scr[pl.ds(base, half), :], precision=HIGHEST,
                              preferred_element_type=F32)
        s2_scr[...] = jnp.dot(k2_ref[hd], q_scr[pl.ds(base + half, half), :], precision=HIGHEST,
                              preferred_element_type=F32)

        def keys_body(j, c):
            m1, p1, s1 = _extract_max(s1_scr[...], key_iota, float(PEER_NKEYS))
            m2, p2, s2 = _extract_max(s2_scr[...], key_iota, float(PEER_NKEYS))
            s1_scr[...] = s1
            s2_scr[...] = s2
            v_scr[pl.ds(j, 1), :] = m1
            i_scr[pl.ds(j, 1), :] = p1
            v_scr[pl.ds(j + k, 1), :] = m2
            i_scr[pl.ds(j + k, 1), :] = p2
            return c

        lax.fori_loop(0, k, keys_body, 0)
        cs_scr[...] = jnp.dot(sel_ref[...], v_scr[...], precision=HIGHEST, preferred_element_type=F32) + cb_ref[...]
        ci_scr[...] = jnp.dot(selid_ref[...], i_scr[...].astype(BF16), preferred_element_type=F32)

        def cand_body(j, c):
            m, e, s = _extract_max(cs_scr[...], cand_order, float(4 * k * k), payload=ci_scr[...])
            cs_scr[...] = s
            ts_scr[pl.ds(j, 1), :] = m
            te_scr[pl.ds(j, 1), :] = e
            return c

        lax.fori_loop(0, k, cand_body, 0)
        ts = ts_scr[...]
        e = jnp.exp(ts - jnp.max(ts, axis=0, keepdims=True))
        out_base = pl.multiple_of(hd * k, k)
        gate_ref[pl.ds(out_base, k), :] = e / jnp.sum(e, axis=0, keepdims=True)
        idx_ref[pl.ds(out_base, k), :] = te_scr[...].astype(jnp.int32)
        return carry

    lax.fori_loop(0, PEER_HEADS, head_body, 0)


def _peer_query_call(x2d, nw, sc, sh, wqt, k1, k2, seq, tm=256):
    n, d = x2d.shape
    per_b = seq // tm
    nq = PEER_HEADS * PEER_QDIM
    k = PEER_TOPK
    scratch = [pltpu.VMEM((nq, tm), F32), pltpu.VMEM((PEER_NKEYS, tm), F32), pltpu.VMEM((PEER_NKEYS, tm), F32),
               pltpu.VMEM((2 * k, tm), F32), pltpu.VMEM((2 * k, tm), F32),
               pltpu.VMEM((_CAND_ROWS, tm), F32), pltpu.VMEM((_CAND_ROWS, tm), F32),
               pltpu.VMEM((k, tm), F32), pltpu.VMEM((k, tm), F32)]
    cand = [jnp.asarray(_CAND_SEL), jnp.asarray(_CAND_SEL_ID, dtype=BF16), jnp.asarray(_CAND_BIAS),
            jnp.asarray(_CAND_ORDER)]
    return pl.pallas_call(
        _peer_query_kernel,
        grid=(n // tm,),
        in_specs=[
            pl.BlockSpec((tm, d), lambda i: (i, 0)),
            _resident((1, d)),
            pl.BlockSpec((None, 1, d), lambda i: (i // per_b, 0, 0)),
            pl.BlockSpec((None, 1, d), lambda i: (i // per_b, 0, 0)),
            _resident(wqt.shape), _resident(k1.shape), _resident(k2.shape),
        ] + [_resident(t.shape) for t in cand],
        out_specs=[pl.BlockSpec((tm, d), lambda i: (i, 0)),
                   pl.BlockSpec((PEER_SEL, tm), lambda i: (0, i)),
                   pl.BlockSpec((PEER_SEL, tm), lambda i: (0, i))],
        out_shape=[jax.ShapeDtypeStruct((n, d), F32),
                   jax.ShapeDtypeStruct((PEER_SEL, n), jnp.int32),
                   jax.ShapeDtypeStruct((PEER_SEL, n), F32)],
        scratch_shapes=scratch,
        compiler_params=_cparams("parallel"),
        name="peer_query",
    )(x2d, nw, sc, sh, wqt, k1, k2, *cand)


PEER_TOKENS_PER_STEP = 8
PEER_DMA_THREADS = 2
_ROW_SUB = 2 * SUBLANES


def _fold_pair(a, b, shift, keep_a):
    fa = a + pltpu.roll(a, shift, axis=0)
    fb = b + pltpu.roll(b, SUBLANES - shift, axis=0)
    return jnp.where(keep_a, fa, fb)


def _slot_order():
    tiles = [[k] * SUBLANES for k in range(PEER_SEL)]
    for shift, keep in ((4, [j < 4 for j in range(8)]), (2, [(j % 4) >= 2 for j in range(8)]),
                        (1, [(j % 2) == 1 for j in range(8)])):
        tiles = [[tiles[2 * m][j] if keep[j] else tiles[2 * m + 1][j] for j in range(8)]
                 for m in range(len(tiles) // 2)]
    return [s for t in tiles for s in t]


_SLOT_AT = _slot_order()
_POS_OF = [0] * PEER_SEL
for _p, _s in enumerate(_SLOT_AT):
    _POS_OF[_s] = _p


def _peer_gather_kernel(idx0_ref, idxc_ref, idxn_ref, g_ref, h_ref, x_ref, g2_ref, uv_hbm, o_ref,
                        buf0, buf1, wb_scr, sems):
    tb = PEER_TOKENS_PER_STEP
    blk = tb * PEER_SEL
    i = pl.program_id(0)
    bufs = (buf0, buf1)

    def row_copy(e, half, row, t):
        return pltpu.make_async_copy(uv_hbm.at[e], bufs[half].at[row], sems.at[half * tb + t])

    def slot_copy(half, t):
        return pltpu.make_async_copy(uv_hbm.at[pl.ds(0, PEER_SEL)], bufs[half].at[pl.ds(t * PEER_SEL, PEER_SEL)],
                                     sems.at[half * tb + t])

    @pl.when(i == 0)
    def _():
        def issue0(t, c):
            for k in range(PEER_SEL):
                row_copy(idx0_ref[0, t * PEER_SEL + k], 0, t * PEER_SEL + k, t).start()
            return c
        lax.fori_loop(0, tb, issue0, 0)

    sub = lax.broadcasted_iota(jnp.int32, (SUBLANES, LANES), 0)
    keep4 = sub < 4
    keep2 = (sub % 4) >= 2
    keep1 = (sub % 2) == 1
    eye = (lax.broadcasted_iota(jnp.int32, (PEER_SEL, LANES), 0)
           == lax.broadcasted_iota(jnp.int32, (PEER_SEL, LANES), 1))

    for half in range(2):
        nidx_ref, noff = (idxc_ref, blk) if half == 0 else (idxn_ref, 0)
        buf = bufs[half]
        for t in range(tb):
            tt = half * tb + t
            slot_copy(half, t).wait()
            for k in range(PEER_SEL):
                row_copy(nidx_ref[0, noff + t * PEER_SEL + k], 1 - half, t * PEER_SEL + k, t).start(
                    priority=k % PEER_DMA_THREADS)
            base = t * PEER_SEL
            ht = h_ref[tt]
            tiles = [buf[base + k, 0:SUBLANES, :] * ht for k in range(PEER_SEL)]
            tiles = [_fold_pair(tiles[2 * m], tiles[2 * m + 1], 4, keep4) for m in range(PEER_SEL // 2)]
            tiles = [_fold_pair(tiles[2 * m], tiles[2 * m + 1], 2, keep2) for m in range(PEER_SEL // 4)]
            tiles = [_fold_pair(tiles[2 * m], tiles[2 * m + 1], 1, keep1) for m in range(PEER_SEL // 8)]
            part = jnp.concatenate(tiles, axis=0)
            s = jnp.sum(part, axis=-1, keepdims=True)
            act = 0.5 * s * (1.0 + lax.erf(s * (2.0 ** -0.5)))
            grow = jnp.broadcast_to(g_ref[tt:tt + 1, :], (PEER_SEL, LANES))
            gcol = jnp.sum(jnp.where(eye, grow, 0.0), axis=-1, keepdims=True)
            wbase = tt * PEER_SEL
            wb_scr[wbase:wbase + PEER_SEL, :] = jnp.broadcast_to(gcol * act, (PEER_SEL, LANES))
            accs = [jnp.zeros((SUBLANES, LANES), F32) for _ in range(4)]
            for k in range(PEER_SEL):
                p = wbase + _POS_OF[k]
                wk = jnp.broadcast_to(wb_scr[p:p + 1, :], (SUBLANES, LANES))
                accs[k % 4] = accs[k % 4] + wk * buf[base + k, SUBLANES:_ROW_SUB, :]
            y = (accs[0] + accs[1]) + (accs[2] + accs[3])
            o_ref[tt] = x_ref[tt] + g2_ref[...] * y

    @pl.when(i == pl.num_programs(0) - 1)
    def _():
        for t in range(tb):
            slot_copy(0, t).wait()


def _peer_gather_call(idx, gate_p, h3, x3, g2_3, uv, seq):
    n = idx.shape[0]
    tb = PEER_TOKENS_PER_STEP
    ns = n // (2 * tb)
    per_b = seq // (2 * tb)
    idx3 = idx.reshape(ns, 1, 2 * tb * PEER_SEL)
    smem_blk = lambda fn: pl.BlockSpec((None, 1, 2 * tb * PEER_SEL), fn, memory_space=pltpu.SMEM)
    tok3 = pl.BlockSpec((2 * tb, SUBLANES, LANES), lambda i: (i, 0, 0))
    return pl.pallas_call(
        _peer_gather_kernel,
        grid=(ns,),
        in_specs=[
            smem_blk(lambda i: (0, 0, 0)),
            smem_blk(lambda i: (i, 0, 0)),
            smem_blk(lambda i: (jnp.minimum(i + 1, ns - 1), 0, 0)),
            pl.BlockSpec((2 * tb, PEER_SEL), lambda i: (i, 0)),
            tok3, tok3,
            pl.BlockSpec((None, SUBLANES, LANES), lambda i: (i // per_b, 0, 0)),
            pl.BlockSpec(memory_space=pl.ANY),
        ],
        out_specs=tok3,
        out_shape=jax.ShapeDtypeStruct((n, SUBLANES, LANES), F32),
        scratch_shapes=[pltpu.VMEM((tb * PEER_SEL, _ROW_SUB, LANES), F32),
                        pltpu.VMEM((tb * PEER_SEL, _ROW_SUB, LANES), F32),
                        pltpu.VMEM((2 * tb * PEER_SEL, LANES), F32),
                        pltpu.SemaphoreType.DMA((2 * tb,))],
        compiler_params=_cparams("arbitrary"),
        name="peer_gather",
    )(idx3, idx3, idx3, gate_p, h3, x3, g2_3, uv)


def _final_norm_kernel(x_ref, w_ref, o_ref):
    x = x_ref[...]
    o_ref[...] = x * lax.rsqrt(jnp.mean(x * x, axis=-1, keepdims=True) + RMS_EPS) * w_ref[...]


def _final_norm_call(x2d, w, tm=1024):
    n, d = x2d.shape
    return pl.pallas_call(
        _final_norm_kernel,
        grid=(n // tm,),
        in_specs=[pl.BlockSpec((tm, d), lambda i: (i, 0)), _resident((1, d))],
        out_specs=pl.BlockSpec((tm, d), lambda i: (i, 0)),
        out_shape=jax.ShapeDtypeStruct((n, d), F32),
        compiler_params=_cparams("parallel"),
        name="final_norm",
    )(x2d, w)


S5_LANES = S5_GROUPS * S5_STATE


def _s5_disc_kernel(are_ref, aim_ref, ldt_ref, bre_ref, bim_ref, ar_ref, ai_ref, br_ref, bi_ref):
    a_re = are_ref[...]
    a_im = aim_ref[...]
    dt = jnp.exp(ldt_ref[...])
    mag = jnp.exp(a_re * dt)
    ar = mag * jnp.cos(a_im * dt)
    ai = mag * jnp.sin(a_im * dt)
    den = a_re * a_re + a_im * a_im
    cr = ((ar - 1.0) * a_re + ai * a_im) / den
    ci = (ai * a_re - (ar - 1.0) * a_im) / den
    ar_ref[...] = ar
    ai_ref[...] = ai
    br_ref[...] = cr * bre_ref[...] - ci * bim_ref[...]
    bi_ref[...] = cr * bim_ref[...] + ci * bre_ref[...]


def _s5_disc_call(a_re, a_im, log_dt, b_re, b_im):
    rows = S5_LANES
    col = lambda t: t.reshape(rows, 1)
    ldt = jnp.broadcast_to(log_dt[:, None], (S5_GROUPS, S5_STATE))
    out = [jax.ShapeDtypeStruct((rows, 1), F32)] * 2 + [jax.ShapeDtypeStruct((rows, S5_GROUP), F32)] * 2
    return pl.pallas_call(_s5_disc_kernel, out_shape=out, name="s5_disc")(
        col(a_re), col(a_im), col(ldt), b_re.reshape(rows, S5_GROUP), b_im.reshape(rows, S5_GROUP))


def _s5_kernel(u_ref, bcat_ref, ar_ref, ai_ref, ccat_ref, d_ref, gw_ref, gb_ref, o_ref, st_scr, bu_scr, *, steps, bt):
    @pl.when(pl.program_id(0) == 0)
    def _():
        st_scr[...] = jnp.zeros_like(st_scr)

    u = u_ref[...]
    bu_scr[...] = jnp.dot(u, bcat_ref[...], precision=HIGHEST, preferred_element_type=F32)
    a_re = jnp.broadcast_to(ar_ref[...], (bt, S5_LANES))
    a_im = jnp.broadcast_to(ai_ref[...], (bt, S5_LANES))

    def step(t, carry):
        s_re, s_im = carry
        r = pl.multiple_of(t * bt, bt)
        n_re = a_re * s_re - a_im * s_im + bu_scr[pl.ds(r, bt), 0:S5_LANES]
        n_im = a_re * s_im + a_im * s_re + bu_scr[pl.ds(r, bt), S5_LANES:2 * S5_LANES]
        bu_scr[pl.ds(r, bt), 0:S5_LANES] = n_re
        bu_scr[pl.ds(r, bt), S5_LANES:2 * S5_LANES] = n_im
        return n_re, n_im

    s_re, s_im = lax.fori_loop(0, steps, step, (st_scr[:, 0:S5_LANES], st_scr[:, S5_LANES:2 * S5_LANES]))
    st_scr[:, 0:S5_LANES] = s_re
    st_scr[:, S5_LANES:2 * S5_LANES] = s_im
    y = jnp.dot(bu_scr[...], ccat_ref[...], precision=HIGHEST, preferred_element_type=F32) + d_ref[...] * u
    zg = 0.5 * y * (1.0 + lax.erf(y * (2.0 ** -0.5)))
    zz = jnp.dot(zg.astype(BF16), gw_ref[...], preferred_element_type=F32) + gb_ref[...]
    o_ref[...] = zz[:, 0:S5_WIDTH] * jax.nn.sigmoid(zz[:, S5_WIDTH:2 * S5_WIDTH])


def _s5_call(u_tb, bcat, abar_re, abar_im, ccat, d, glu_w, glu_b, bt, steps=64):
    rows = u_tb.shape[0]
    blk = steps * bt
    return pl.pallas_call(
        functools.partial(_s5_kernel, steps=steps, bt=bt),
        grid=(rows // blk,),
        in_specs=[pl.BlockSpec((blk, S5_WIDTH), lambda i: (i, 0)),
                  _resident(bcat.shape), _resident(abar_re.shape), _resident(abar_im.shape), _resident(ccat.shape),
                  _resident(d.shape), _resident(glu_w.shape), _resident(glu_b.shape)],
        out_specs=pl.BlockSpec((blk, S5_WIDTH), lambda i: (i, 0)),
        out_shape=jax.ShapeDtypeStruct((rows, S5_WIDTH), F32),
        scratch_shapes=[pltpu.VMEM((bt, 2 * S5_LANES), F32), pltpu.VMEM((blk, 2 * S5_LANES), F32)],
        compiler_params=_cparams("arbitrary"),
        name="s5_scan",
    )(u_tb, bcat, abar_re, abar_im, ccat, d, glu_w, glu_b)


def _s5_branch_pallas(p_ub, bt, seq, a_re, a_im, b_re, b_im, c_re, c_im, d, log_dt, glu_w, glu_b):
    ar, ai, br, bi = _s5_disc_call(a_re, a_im, log_dt, b_re, b_im)
    g_ix = jnp.arange(S5_GROUPS)
    def blockdiag_in(b):
        b = b.reshape(S5_GROUPS, S5_STATE, S5_GROUP)
        full = jnp.zeros((S5_GROUPS, S5_GROUP, S5_GROUPS, S5_STATE), F32)
        return full.at[g_ix, :, g_ix, :].set(b.transpose(0, 2, 1)).reshape(S5_WIDTH, S5_LANES)

    def blockdiag_out(c):
        full = jnp.zeros((S5_GROUPS, S5_STATE, S5_GROUPS, S5_GROUP), F32)
        return full.at[g_ix, :, g_ix, :].set(c.transpose(0, 2, 1)).reshape(S5_LANES, S5_WIDTH)

    bcat = jnp.concatenate([blockdiag_in(br), blockdiag_in(bi)], axis=1)
    ccat = jnp.concatenate([blockdiag_out(c_re), -blockdiag_out(c_im)], axis=0)
    u_tb = p_ub.reshape(bt, seq, S5_WIDTH).transpose(1, 0, 2).reshape(seq * bt, S5_WIDTH)
    o_tb = _s5_call(u_tb, bcat, ar.reshape(1, S5_LANES), ai.reshape(1, S5_LANES), ccat, d.reshape(1, S5_WIDTH),
                    glu_w.astype(BF16), glu_b.reshape(1, 2 * S5_WIDTH), bt)
    return o_tb.reshape(seq, bt, S5_WIDTH).transpose(1, 0, 2).reshape(bt * seq, S5_WIDTH)


def _mm(a, b):
    return jnp.dot(a.astype(BF16), b.astype(BF16), preferred_element_type=F32)


def _mm_nt(a, b, precision=None):
    if precision is None:
        a, b = a.astype(BF16), b.astype(BF16)
    return lax.dot_general(a, b, (((1,), (1,)), ((), ())), precision=precision, preferred_element_type=F32)


def _mm_tn(a, b):
    return lax.dot_general(a.astype(BF16), b.astype(BF16), (((0,), (0,)), ((), ())), preferred_element_type=F32)


def _softplus(x):
    return jnp.maximum(x, 0.0) + jnp.log1p(jnp.exp(-jnp.abs(x)))


def _gdn_kernel(pa_ref, ba_ref, cw_ref, alog_ref, dtb_ref, nw_ref, o_ref,
                st_scr, carry_scr, q_scr, k_scr, v_scr, g_scr, b_scr, u_scr, w_scr, qk_scr, gi_scr):
    j = pl.program_id(1)
    t_blk = pa_ref.shape[0]
    c_len, wid, dh = GDN_CHUNK, GDN_WIDTH, GDN_HEAD_DIM

    @pl.when(j == 0)
    def _():
        st_scr[...] = jnp.zeros_like(st_scr)
        carry_scr[...] = jnp.zeros_like(carry_scr)

    x = pa_ref[:, 0:3 * wid]
    xx = jnp.concatenate([carry_scr[...], x], axis=0)
    conv = cw_ref[GDN_CONV - 1:GDN_CONV, :] * x
    for k in range(1, GDN_CONV):
        conv = conv + cw_ref[GDN_CONV - 1 - k:GDN_CONV - k, :] * pltpu.roll(xx, k, axis=0)[SUBLANES:, :]
    carry_scr[...] = x[t_blk - SUBLANES:t_blk, :]
    act = conv * jax.nn.sigmoid(conv)
    for h in range(GDN_HEADS):
        qh = act[:, h * dh:(h + 1) * dh]
        kh = act[:, wid + h * dh:wid + (h + 1) * dh]
        q_scr[:, h * dh:(h + 1) * dh] = qh * lax.rsqrt(jnp.sum(qh * qh, axis=-1, keepdims=True) + 1e-6) * (dh ** -0.5)
        k_scr[:, h * dh:(h + 1) * dh] = kh * lax.rsqrt(jnp.sum(kh * kh, axis=-1, keepdims=True) + 1e-6)
    v_scr[...] = act[:, 2 * wid:3 * wid]
    ba = ba_ref[...]
    g_scr[...] = -jnp.exp(alog_ref[...]) * _softplus(ba + dtb_ref[...])
    b_scr[...] = jax.nn.sigmoid(ba)

    row = lax.broadcasted_iota(jnp.int32, (t_blk, t_blk), 0)
    col = lax.broadcasted_iota(jnp.int32, (t_blk, t_blk), 1)
    same = (row // c_len) == (col // c_len)
    tril = same & (row >= col)
    strict = same & (row > col)
    eye = jnp.where(row == col, 1.0, 0.0)
    lane = lax.broadcasted_iota(jnp.int32, (t_blk, LANES), 1)
    nw = nw_ref[...]
    gcum = jnp.dot(jnp.where(tril, 1.0, 0.0), g_scr[...], precision=HIGHEST, preferred_element_type=F32)
    beta_all = b_scr[...]
    for h in range(GDN_HEADS):
        hs = slice(h * dh, (h + 1) * dh)
        gi = gcum[:, GDN_HEADS + h:GDN_HEADS + h + 1]
        beta = beta_all[:, h:h + 1]
        q_h, k_h, v_h = q_scr[:, hs], k_scr[:, hs], v_scr[:, hs]
        g1 = jnp.where(lane == 0, gi, jnp.where(lane == 1, 1.0, 0.0))
        g2 = jnp.where(lane == 0, 1.0, jnp.where(lane == 1, -gi, 0.0))
        diff = _mm_nt(g1, g2, precision=HIGHEST)
        decay = jnp.where(tril, jnp.exp(jnp.where(tril, diff, 0.0)), 0.0)
        kb = k_h * beta
        a = jnp.where(strict, _mm_nt(kb, k_h) * decay, 0.0)
        p = -a
        t = eye + p
        for _ in range(int(math.log2(c_len)) - 1):
            p = _mm(p, p)
            t = t + _mm(t, p)
        e_g = jnp.exp(gi)
        u_scr[:, hs] = _mm(t, v_h * beta)
        w_scr[:, hs] = _mm(t, kb * e_g)
        qk_scr[h] = jnp.where(tril, _mm_nt(q_h, k_h) * decay, 0.0)
        q_scr[:, hs] = q_h * e_g
        gi_scr[:, h:h + 1] = gi

    for c in range(t_blk // c_len):
        rows = slice(c * c_len, (c + 1) * c_len)
        for h in range(GDN_HEADS):
            hs = slice(h * dh, (h + 1) * dh)
            gi = gi_scr[rows, h:h + 1]
            state = st_scr[h]
            v_new = u_scr[rows, hs] - _mm(w_scr[rows, hs], state)
            o = _mm(q_scr[rows, hs], state) + _mm(qk_scr[h, rows, rows], v_new)
            g_last = gi[c_len - 1:c_len, :]
            st_scr[h] = state * jnp.exp(g_last) + _mm_tn(k_scr[rows, hs] * jnp.exp(g_last - gi), v_new)
            z = pa_ref[rows, 3 * wid + h * dh:3 * wid + (h + 1) * dh]
            o_ref[rows, hs] = (o * lax.rsqrt(jnp.mean(o * o, axis=-1, keepdims=True) + RMS_EPS) * nw
                               * (z * jax.nn.sigmoid(z)))


def _gdn_branch_pallas(p_a, p_ba, conv_w, a_log, dt_bias, norm_w, bt, seq, t_blk=256):
    n = p_a.shape[0]
    nj = seq // t_blk
    wid = GDN_WIDTH
    lane_pad = lambda v: jnp.zeros((1, LANES), F32).at[0, GDN_HEADS:2 * GDN_HEADS].set(v)
    rows = lambda wd: pl.BlockSpec((t_blk, wd), lambda b, j: (b * nj + j, 0))
    return pl.pallas_call(
        _gdn_kernel,
        grid=(bt, nj),
        in_specs=[rows(4 * wid), rows(LANES), _resident((GDN_CONV, 3 * wid)), _resident((1, LANES)),
                  _resident((1, LANES)), _resident((1, GDN_HEAD_DIM))],
        out_specs=rows(wid),
        out_shape=jax.ShapeDtypeStruct((n, wid), F32),
        scratch_shapes=[pltpu.VMEM((GDN_HEADS, GDN_HEAD_DIM, GDN_HEAD_DIM), F32), pltpu.VMEM((SUBLANES, 3 * wid), F32),
                        pltpu.VMEM((t_blk, wid), F32), pltpu.VMEM((t_blk, wid), F32), pltpu.VMEM((t_blk, wid), F32),
                        pltpu.VMEM((t_blk, LANES), F32), pltpu.VMEM((t_blk, LANES), F32),
                        pltpu.VMEM((t_blk, wid), F32), pltpu.VMEM((t_blk, wid), F32),
                        pltpu.VMEM((GDN_HEADS, t_blk, t_blk), F32), pltpu.VMEM((t_blk, LANES), F32)],
        compiler_params=_cparams("parallel", "arbitrary"),
        name="gdn_delta",
    )(p_a, p_ba, conv_w, lane_pad(a_log), lane_pad(dt_bias), norm_w.reshape(1, GDN_HEAD_DIM))


def _moba_kernel(qt_ref, k_ref, vt_ref, o_ref, km_scr, sel_scr):
    i = pl.program_id(2)
    nblk = k_ref.shape[0]

    @pl.when(i == 0)
    def _():
        for n in range(nblk):
            km_scr[n:n + 1, :] = jnp.mean(k_ref[n], axis=0, keepdims=True)

    qt = qt_ref[...]
    gate = jnp.dot(km_scr[...], qt, precision=HIGHEST, preferred_element_type=F32)
    blk = lax.broadcasted_iota(jnp.int32, gate.shape, 0)
    cnt = jnp.zeros(gate.shape, F32)
    for m in range(nblk):
        gm = gate[m:m + 1, :]
        beats = (gm > gate) | ((gm == gate) & (m < blk))
        cnt = cnt + jnp.where(beats & (m < i), 1.0, 0.0)
    sel_scr[...] = jnp.where((blk < i) & (cnt < float(MOBA_TOPK)), 1.0, 0.0)

    scale = MOBA_HEAD_DIM ** -0.5
    qb = qt.astype(BF16)
    s = jnp.dot(k_ref[i].astype(BF16), qb, preferred_element_type=F32) * scale
    kpos = lax.broadcasted_iota(jnp.int32, s.shape, 0)
    qpos = lax.broadcasted_iota(jnp.int32, s.shape, 1)
    s = jnp.where(kpos <= qpos, s, NEG_INF)
    m0 = jnp.max(s, axis=0, keepdims=True)
    p = jnp.exp(s - m0)
    l0 = jnp.sum(p, axis=0, keepdims=True)
    acc0 = jnp.dot(vt_ref[i].astype(BF16), p.astype(BF16), preferred_element_type=F32)

    def body(n, carry):
        m, l, acc = carry
        s = jnp.dot(k_ref[n].astype(BF16), qb, preferred_element_type=F32) * scale
        s = jnp.where(sel_scr[pl.ds(n, 1), :] > 0.5, s, NEG_INF)
        m_new = jnp.maximum(m, jnp.max(s, axis=0, keepdims=True))
        p = jnp.exp(s - m_new)
        alpha = jnp.exp(m - m_new)
        l = alpha * l + jnp.sum(p, axis=0, keepdims=True)
        acc = alpha * acc + jnp.dot(vt_ref[n].astype(BF16), p.astype(BF16), preferred_element_type=F32)
        return m_new, l, acc

    m, l, acc = lax.fori_loop(0, i, body, (m0, l0, acc0))
    o_ref[...] = acc / l


def _moba_branch_pallas(p_c, bt, seq):
    nh, dh, bs = MOBA_HEADS, MOBA_HEAD_DIM, MOBA_BLOCK
    assert seq % bs == 0
    nblk = seq // bs
    qkv = p_c.reshape(bt, seq, 3, nh, dh)
    qt = qkv[:, :, 0].transpose(0, 2, 3, 1)
    k5 = qkv[:, :, 1].reshape(bt, nblk, bs, nh, dh).transpose(0, 3, 1, 2, 4)
    vt5 = qkv[:, :, 2].reshape(bt, nblk, bs, nh, dh).transpose(0, 3, 1, 4, 2)
    ot = pl.pallas_call(
        _moba_kernel,
        grid=(bt, nh, nblk),
        in_specs=[pl.BlockSpec((None, None, dh, bs), lambda b, h, i: (b, h, 0, i)),
                  pl.BlockSpec((None, None, nblk, bs, dh), lambda b, h, i: (b, h, 0, 0, 0)),
                  pl.BlockSpec((None, None, nblk, dh, bs), lambda b, h, i: (b, h, 0, 0, 0))],
        out_specs=pl.BlockSpec((None, None, dh, bs), lambda b, h, i: (b, h, 0, i)),
        out_shape=jax.ShapeDtypeStruct((bt, nh, dh, seq), F32),
        scratch_shapes=[pltpu.VMEM((nblk, dh), F32), pltpu.VMEM((nblk, bs), F32)],
        compiler_params=_cparams("parallel", "parallel", "arbitrary"),
        name="moba_attn",
    )(qt, k5, vt5)
    return ot.transpose(0, 3, 1, 2).reshape(bt * seq, nh * dh)


def kernel(x, c, ada_w, ada_b, norm1_w, w_in, gdn_conv_w, gdn_a_log, gdn_dt_bias, gdn_norm_w, s5_a_re, s5_a_im, s5_b_re, s5_b_im, s5_c_re, s5_c_im, s5_d, s5_log_dt, s5_glu_w, s5_glu_b, w_branch_a, w_branch_b, w_branch_c, w_out, norm2_w, peer_wq, peer_k1, peer_k2, peer_u, peer_v, final_norm_w):
    bt, seq, d = x.shape
    n = bt * seq
    depth = ada_w.shape[0]
    x2d = x.reshape(n, d)
    mod = _ada_call(c, ada_w, ada_b)
    gate_perm = np.asarray(_SLOT_AT, np.int32)
    for l in range(depth):
        sh1, sc1, g1, sh2, sc2, g2 = (mod[l, :, j * d:(j + 1) * d].reshape(bt, 1, d) for j in range(6))
        p_a, p_ub, p_c, p_gate, p_ba = _in_proj_call(
            x2d, norm1_w[l].reshape(1, d), sc1, sh1, _arrange_w_in(w_in[l]), seq)
        o_a = _gdn_branch_pallas(p_a, p_ba, gdn_conv_w[l], gdn_a_log[l], gdn_dt_bias[l], gdn_norm_w[l], bt, seq)
        o_b = _s5_branch_pallas(p_ub, bt, seq, s5_a_re[l], s5_a_im[l], s5_b_re[l], s5_b_im[l], s5_c_re[l],
                                s5_c_im[l], s5_d[l], s5_log_dt[l], s5_glu_w[l], s5_glu_b[l])
        o_c = _moba_branch_pallas(p_c, bt, seq)
        x2d = _merge_call(o_a, o_b, o_c, p_gate, x2d, g1, w_branch_a[l].astype(BF16), w_branch_b[l].astype(BF16),
                          w_branch_c[l].astype(BF16), w_out[l].astype(BF16), seq)
        h2, idx_t, gate_t = _peer_query_call(x2d, norm2_w[l].reshape(1, d), sc2, sh2,
                                             peer_wq[l].T.astype(BF16), peer_k1[l], peer_k2[l], seq)
        uv = jnp.concatenate([peer_u[l].reshape(-1, SUBLANES, LANES), peer_v[l].reshape(-1, SUBLANES, LANES)], axis=1)
        x3 = _peer_gather_call(idx_t.T, gate_t.T[:, gate_perm], h2.reshape(n, SUBLANES, LANES),
                               x2d.reshape(n, SUBLANES, LANES), g2.reshape(bt, SUBLANES, LANES), uv, seq)
        x2d = x3.reshape(n, d)
    return _final_norm_call(x2d, final_norm_w.reshape(1, d)).reshape(bt, seq, d)
```

```python
import functools
import math

import jax
import jax.numpy as jnp
import numpy as np
from jax import lax
from jax.experimental import pallas as pl
from jax.experimental.pallas import tpu as pltpu

F32 = jnp.float32
BF16 = jnp.bfloat16
HIGHEST = lax.Precision.HIGHEST

D_MODEL = 1024
GDN_HEADS = 4
GDN_HEAD_DIM = 128
GDN_WIDTH = GDN_HEADS * GDN_HEAD_DIM
GDN_CONV = 4
GDN_CHUNK = 64
S5_GROUP = 16
S5_GROUPS = 16
S5_WIDTH = S5_GROUPS * S5_GROUP
S5_STATE = 64
MOBA_HEADS = 4
MOBA_HEAD_DIM = 64
MOBA_WIDTH = MOBA_HEADS * MOBA_HEAD_DIM
MOBA_BLOCK = 256
MOBA_TOPK = 3
MOBA_Q_CHUNK = 64
N_BRANCH = 3
PEER_HEADS = 8
PEER_NKEYS = 128
PEER_QDIM = 256
PEER_TOPK = 16
PEER_SEL = PEER_HEADS * PEER_TOPK
RMS_EPS = 1e-6
NEG_INF = -1e30

SUBLANES = 8
LANES = 128
VMEM_LIMIT_BYTES = 56 * 1024 * 1024

_OFF_QKV_A = 0
_OFF_Z_A = 3 * GDN_WIDTH
_OFF_BETA = _OFF_Z_A + GDN_WIDTH
_OFF_ALPHA = _OFF_BETA + GDN_HEADS
_OFF_UB = _OFF_ALPHA + GDN_HEADS
_OFF_QKV_C = _OFF_UB + S5_WIDTH
_OFF_GATE = _OFF_QKV_C + 3 * MOBA_WIDTH
_IN_COLS = _OFF_GATE + N_BRANCH * D_MODEL


def _cparams(*sem):
    return pltpu.CompilerParams(dimension_semantics=sem, vmem_limit_bytes=VMEM_LIMIT_BYTES)


def _resident(shape):
    nd = len(shape)
    return pl.BlockSpec(shape, lambda *_: (0,) * nd)


def _ada_kernel(c_ref, w_ref, b_ref, o_ref):
    c = c_ref[...]
    sc = c * jax.nn.sigmoid(c)
    o_ref[...] = jnp.dot(sc, w_ref[...], precision=HIGHEST, preferred_element_type=F32) + b_ref[...]


def _ada_call(c, ada_w, ada_b):
    depth, d, d6 = ada_w.shape
    bt = c.shape[0]
    nj = d6 // d
    return pl.pallas_call(
        _ada_kernel,
        grid=(depth, nj),
        in_specs=[
            pl.BlockSpec((bt, d), lambda l, j: (0, 0)),
            pl.BlockSpec((None, d, d), lambda l, j: (l, 0, j)),
            pl.BlockSpec((None, 1, d), lambda l, j: (l, 0, j)),
        ],
        out_specs=pl.BlockSpec((None, bt, d), lambda l, j: (l, 0, j)),
        out_shape=jax.ShapeDtypeStruct((depth, bt, d6), F32),
        compiler_params=_cparams("parallel", "parallel"),
        name="ada_mod",
    )(c, ada_w, ada_b.reshape(depth, 1, d6))


def _norm_mod(x, nw, sc, sh):
    y = x * lax.rsqrt(jnp.mean(x * x, axis=-1, keepdims=True) + RMS_EPS)
    return (y * nw) * (1.0 + sc) + sh


_IN_SPLITS = (4 * GDN_WIDTH, S5_WIDTH, 3 * MOBA_WIDTH, N_BRANCH * D_MODEL, LANES)


def _in_proj_kernel(x_ref, nw_ref, sc_ref, sh_ref, w_ref, oa_ref, ob_ref, oc_ref, og_ref, oba_ref):
    h = _norm_mod(x_ref[...], nw_ref[...], sc_ref[...], sh_ref[...]).astype(BF16)
    off = 0
    for o_ref, width in zip((oa_ref, ob_ref, oc_ref, og_ref, oba_ref), _IN_SPLITS):
        o_ref[...] = jnp.dot(h, w_ref[:, off:off + width], preferred_element_type=F32)
        off += width


def _in_proj_call(x2d, nw, sc, sh, w_r, seq, tm=256):
    n, d = x2d.shape
    per_b = seq // tm
    wcols = w_r.shape[1]
    outs = [jax.ShapeDtypeStruct((n, wd), F32) for wd in _IN_SPLITS]
    return pl.pallas_call(
        _in_proj_kernel,
        grid=(n // tm,),
        in_specs=[
            pl.BlockSpec((tm, d), lambda i: (i, 0)),
            _resident((1, d)),
            pl.BlockSpec((None, 1, d), lambda i: (i // per_b, 0, 0)),
            pl.BlockSpec((None, 1, d), lambda i: (i // per_b, 0, 0)),
            _resident((d, wcols)),
        ],
        out_specs=[pl.BlockSpec((tm, wd), lambda i: (i, 0)) for wd in _IN_SPLITS],
        out_shape=outs,
        compiler_params=_cparams("parallel"),
        name="in_proj",
    )(x2d, nw, sc, sh, w_r)


def _arrange_w_in(w_in_l):
    pad = jnp.zeros((w_in_l.shape[0], LANES - 2 * GDN_HEADS), w_in_l.dtype)
    return jnp.concatenate([
        w_in_l[:, _OFF_QKV_A:_OFF_BETA],
        w_in_l[:, _OFF_UB:_OFF_QKV_C],
        w_in_l[:, _OFF_QKV_C:_OFF_GATE],
        w_in_l[:, _OFF_GATE:_IN_COLS],
        w_in_l[:, _OFF_BETA:_OFF_UB], pad,
    ], axis=1).astype(BF16)


def _merge_kernel(oa_ref, ob_ref, oc_ref, gt_ref, x_ref, g1_ref, wa_ref, wb_ref, wc_ref, wo_ref, o_ref):
    d = D_MODEL
    ya = jnp.dot(oa_ref[...].astype(BF16), wa_ref[...], preferred_element_type=F32)
    yb = jnp.dot(ob_ref[...].astype(BF16), wb_ref[...], preferred_element_type=F32)
    yc = jnp.dot(oc_ref[...].astype(BF16), wc_ref[...], preferred_element_type=F32)
    merged = (jax.nn.sigmoid(gt_ref[:, 0:d]) * ya + jax.nn.sigmoid(gt_ref[:, d:2 * d]) * yb
              + jax.nn.sigmoid(gt_ref[:, 2 * d:3 * d]) * yc)
    y = jnp.dot(merged.astype(BF16), wo_ref[...], preferred_element_type=F32)
    o_ref[...] = x_ref[...] + g1_ref[...] * y


def _merge_call(o_a, o_b, o_c, p_gate, x2d, g1, wa, wb, wc, wo, seq, tm=512):
    n, d = x2d.shape
    per_b = seq // tm
    row = lambda wd: pl.BlockSpec((tm, wd), lambda i: (i, 0))
    return pl.pallas_call(
        _merge_kernel,
        grid=(n // tm,),
        in_specs=[row(GDN_WIDTH), row(S5_WIDTH), row(MOBA_WIDTH), row(N_BRANCH * d), row(d),
                  pl.BlockSpec((None, 1, d), lambda i: (i // per_b, 0, 0)),
                  _resident(wa.shape), _resident(wb.shape), _resident(wc.shape), _resident(wo.shape)],
        out_specs=row(d),
        out_shape=jax.ShapeDtypeStruct((n, d), F32),
        compiler_params=_cparams("parallel"),
        name="merge_out",
    )(o_a, o_b, o_c, p_gate, x2d, g1, wa, wb, wc, wo)


def _candidate_tables():
    k = PEER_TOPK
    pairs = [(a, b) for a in range(k) for b in range(k) if (a + 1) * (b + 1) <= k]
    rows = -(-len(pairs) // SUBLANES) * SUBLANES
    sel = np.zeros((rows, 2 * k), np.float32)
    sel_id = np.zeros((rows, 2 * k), np.float32)
    bias = np.zeros((rows, 1), np.float32)
    order = np.zeros((rows, 1), np.float32)
    for r, (a, b) in enumerate(pairs):
        sel[r, a] = sel[r, k + b] = 1.0
        sel_id[r, a] = float(PEER_NKEYS)
        sel_id[r, k + b] = 1.0
        order[r, 0] = a * k + b
    for r in range(len(pairs), rows):
        bias[r, 0] = -np.inf
        order[r, 0] = k * k + r
    return sel, sel_id, bias, order


_CAND_SEL, _CAND_SEL_ID, _CAND_BIAS, _CAND_ORDER = _candidate_tables()
_CAND_ROWS = _CAND_SEL.shape[0]


def _extract_max(s, order, big, payload=None):
    m = jnp.max(s, axis=0, keepdims=True)
    pos = jnp.min(jnp.where(s == m, order, big), axis=0, keepdims=True)
    hit = order == pos
    tag = pos if payload is None else jnp.sum(jnp.where(hit, payload, 0.0), axis=0, keepdims=True)
    return m, tag, jnp.where(hit, -jnp.inf, s)


def _peer_query_kernel(x_ref, nw_ref, sc_ref, sh_ref, wqt_ref, k1_ref, k2_ref, sel_ref, selid_ref, cb_ref, co_ref,
                       h_ref, idx_ref, gate_ref, q_scr, s1_scr, s2_scr, v_scr, i_scr, cs_scr, ci_scr, ts_scr, te_scr):
    h = _norm_mod(x_ref[...], nw_ref[...], sc_ref[...], sh_ref[...])
    h_ref[...] = h
    q_scr[...] = lax.dot_general(wqt_ref[...], h.astype(BF16), (((1,), (1,)), ((), ())),
                                 preferred_element_type=F32)
    half = PEER_QDIM // 2
    k = PEER_TOPK
    key_iota = lax.broadcasted_iota(jnp.int32, s1_scr.shape, 0).astype(F32)
    cand_order = jnp.broadcast_to(co_ref[...], cs_scr.shape)

    def head_body(hd, carry):
        base = pl.multiple_of(hd * PEER_QDIM, PEER_QDIM)
        s1_scr[...] = jnp.dot(k1_ref[hd], q_scr[pl.ds(base, half), :], precision=HIGHEST,
                              preferred_element_type=F32)
        s2_scr[...] = jnp.dot(k2_ref[hd], q_scr[pl.ds(base + half, half), :], precision=HIGHEST,
                              preferred_element_type=F32)

        def keys_body(j, c):
            m1, p1, s1 = _extract_max(s1_scr[...], key_iota, float(PEER_NKEYS))
            m2, p2, s2 = _extract_max(s2_scr[...], key_iota, float(PEER_NKEYS))
            s1_scr[...] = s1
            s2_scr[...] = s2
            v_scr[pl.ds(j, 1), :] = m1
            i_scr[pl.ds(j, 1), :] = p1
            v_scr[pl.ds(j + k, 1), :] = m2
            i_scr[pl.ds(j + k, 1), :] = p2
            return c

        lax.fori_loop(0, k, keys_body, 0)
        cs_scr[...] = jnp.dot(sel_ref[...], v_scr[...], precision=HIGHEST, preferred_element_type=F32) + cb_ref[...]
        ci_scr[...] = jnp.dot(selid_ref[...], i_scr[...].astype(BF16), preferred_element_type=F32)

        def cand_body(j, c):
            m, e, s = _extract_max(cs_scr[...], cand_order, float(4 * k * k), payload=ci_scr[...])
            cs_scr[...] = s
            ts_scr[pl.ds(j, 1), :] = m
            te_scr[pl.ds(j, 1), :] = e
            return c

        lax.fori_loop(0, k, cand_body, 0)
        ts = ts_scr[...]
        e = jnp.exp(ts - jnp.max(ts, axis=0, keepdims=True))
        out_base = pl.multiple_of(hd * k, k)
        gate_ref[pl.ds(out_base, k), :] = e / jnp.sum(e, axis=0, keepdims=True)
        idx_ref[pl.ds(out_base, k), :] = te_scr[...].astype(jnp.int32)
        return carry

    lax.fori_loop(0, PEER_HEADS, head_body, 0)


def _peer_query_call(x2d, nw, sc, sh, wqt, k1, k2, seq, tm=256):
    n, d = x2d.shape
    per_b = seq // tm
    nq = PEER_HEADS * PEER_QDIM
    k = PEER_TOPK
    scratch = [pltpu.VMEM((nq, tm), F32), pltpu.VMEM((PEER_NKEYS, tm), F32), pltpu.VMEM((PEER_NKEYS, tm), F32),
               pltpu.VMEM((2 * k, tm), F32), pltpu.VMEM((2 * k, tm), F32),
               pltpu.VMEM((_CAND_ROWS, tm), F32), pltpu.VMEM((_CAND_ROWS, tm), F32),
               pltpu.VMEM((k, tm), F32), pltpu.VMEM((k, tm), F32)]
    cand = [jnp.asarray(_CAND_SEL), jnp.asarray(_CAND_SEL_ID, dtype=BF16), jnp.asarray(_CAND_BIAS),
            jnp.asarray(_CAND_ORDER)]
    return pl.pallas_call(
        _peer_query_kernel,
        grid=(n // tm,),
        in_specs=[
            pl.BlockSpec((tm, d), lambda i: (i, 0)),
            _resident((1, d)),
            pl.BlockSpec((None, 1, d), lambda i: (i // per_b, 0, 0)),
            pl.BlockSpec((None, 1, d), lambda i: (i // per_b, 0, 0)),
            _resident(wqt.shape), _resident(k1.shape), _resident(k2.shape),
        ] + [_resident(t.shape) for t in cand],
        out_specs=[pl.BlockSpec((tm, d), lambda i: (i, 0)),
                   pl.BlockSpec((PEER_SEL, tm), lambda i: (0, i)),
                   pl.BlockSpec((PEER_SEL, tm), lambda i: (0, i))],
        out_shape=[jax.ShapeDtypeStruct((n, d), F32),
                   jax.ShapeDtypeStruct((PEER_SEL, n), jnp.int32),
                   jax.ShapeDtypeStruct((PEER_SEL, n), F32)],
        scratch_shapes=scratch,
        compiler_params=_cparams("parallel"),
        name="peer_query",
    )(x2d, nw, sc, sh, wqt, k1, k2, *cand)


PEER_TOKENS_PER_STEP = 8
PEER_DMA_THREADS = 2
_ROW_SUB = 2 * SUBLANES


def _fold_pair(a, b, shift, keep_a):
    fa = a + pltpu.roll(a, shift, axis=0)
    fb = b + pltpu.roll(b, SUBLANES - shift, axis=0)
    return jnp.where(keep_a, fa, fb)


def _slot_order():
    tiles = [[k] * SUBLANES for k in range(PEER_SEL)]
    for shift, keep in ((4, [j < 4 for j in range(8)]), (2, [(j % 4) >= 2 for j in range(8)]),
                        (1, [(j % 2) == 1 for j in range(8)])):
        tiles = [[tiles[2 * m][j] if keep[j] else tiles[2 * m + 1][j] for j in range(8)]
                 for m in range(len(tiles) // 2)]
    return [s for t in tiles for s in t]


_SLOT_AT = _slot_order()
_POS_OF = [0] * PEER_SEL
for _p, _s in enumerate(_SLOT_AT):
    _POS_OF[_s] = _p


def _peer_gather_kernel(idx0_ref, idxc_ref, idxn_ref, g_ref, h_ref, x_ref, g2_ref, uv_hbm, o_ref,
                        buf0, buf1, wb_scr, sems):
    tb = PEER_TOKENS_PER_STEP
    blk = tb * PEER_SEL
    i = pl.program_id(0)
    bufs = (buf0, buf1)

    def row_copy(e, half, row, t):
        return pltpu.make_async_copy(uv_hbm.at[e], bufs[half].at[row], sems.at[half * tb + t])

    def slot_copy(half, t):
        return pltpu.make_async_copy(uv_hbm.at[pl.ds(0, PEER_SEL)], bufs[half].at[pl.ds(t * PEER_SEL, PEER_SEL)],
                                     sems.at[half * tb + t])

    @pl.when(i == 0)
    def _():
        def issue0(t, c):
            for k in range(PEER_SEL):
                row_copy(idx0_ref[0, t * PEER_SEL + k], 0, t * PEER_SEL + k, t).start()
            return c
        lax.fori_loop(0, tb, issue0, 0)

    sub = lax.broadcasted_iota(jnp.int32, (SUBLANES, LANES), 0)
    keep4 = sub < 4
    keep2 = (sub % 4) >= 2
    keep1 = (sub % 2) == 1
    eye = (lax.broadcasted_iota(jnp.int32, (PEER_SEL, LANES), 0)
           == lax.broadcasted_iota(jnp.int32, (PEER_SEL, LANES), 1))

    for half in range(2):
        nidx_ref, noff = (idxc_ref, blk) if half == 0 else (idxn_ref, 0)
        buf = bufs[half]
        for t in range(tb):
            tt = half * tb + t
            slot_copy(half, t).wait()
            for k in range(PEER_SEL):
                row_copy(nidx_ref[0, noff + t * PEER_SEL + k], 1 - half, t * PEER_SEL + k, t).start(
                    priority=k % PEER_DMA_THREADS)
            base = t * PEER_SEL
            ht = h_ref[tt]
            tiles = [buf[base + k, 0:SUBLANES, :] * ht for k in range(PEER_SEL)]
            tiles = [_fold_pair(tiles[2 * m], tiles[2 * m + 1], 4, keep4) for m in range(PEER_SEL // 2)]
            tiles = [_fold_pair(tiles[2 * m], tiles[2 * m + 1], 2, keep2) for m in range(PEER_SEL // 4)]
            tiles = [_fold_pair(tiles[2 * m], tiles[2 * m + 1], 1, keep1) for m in range(PEER_SEL // 8)]
            part = jnp.concatenate(tiles, axis=0)
            s = jnp.sum(part, axis=-1, keepdims=True)
            act = 0.5 * s * (1.0 + lax.erf(s * (2.0 ** -0.5)))
            grow = jnp.broadcast_to(g_ref[tt:tt + 1, :], (PEER_SEL, LANES))
            gcol = jnp.sum(jnp.where(eye, grow, 0.0), axis=-1, keepdims=True)
            wbase = tt * PEER_SEL
            wb_scr[wbase:wbase + PEER_SEL, :] = jnp.broadcast_to(gcol * act, (PEER_SEL, LANES))
            accs = [jnp.zeros((SUBLANES, LANES), F32) for _ in range(4)]
            for k in range(PEER_SEL):
                p = wbase + _POS_OF[k]
                wk = jnp.broadcast_to(wb_scr[p:p + 1, :], (SUBLANES, LANES))
                accs[k % 4] = accs[k % 4] + wk * buf[base + k, SUBLANES:_ROW_SUB, :]
            y = (accs[0] + accs[1]) + (accs[2] + accs[3])
            o_ref[tt] = x_ref[tt] + g2_ref[...] * y

    @pl.when(i == pl.num_programs(0) - 1)
    def _():
        for t in range(tb):
            slot_copy(0, t).wait()


def _peer_gather_call(idx, gate_p, h3, x3, g2_3, uv, seq):
    n = idx.shape[0]
    tb = PEER_TOKENS_PER_STEP
    ns = n // (2 * tb)
    per_b = seq // (2 * tb)
    idx3 = idx.reshape(ns, 1, 2 * tb * PEER_SEL)
    smem_blk = lambda fn: pl.BlockSpec((None, 1, 2 * tb * PEER_SEL), fn, memory_space=pltpu.SMEM)
    tok3 = pl.BlockSpec((2 * tb, SUBLANES, LANES), lambda i: (i, 0, 0))
    return pl.pallas_call(
        _peer_gather_kernel,
        grid=(ns,),
        in_specs=[
            smem_blk(lambda i: (0, 0, 0)),
            smem_blk(lambda i: (i, 0, 0)),
            smem_blk(lambda i: (jnp.minimum(i + 1, ns - 1), 0, 0)),
            pl.BlockSpec((2 * tb, PEER_SEL), lambda i: (i, 0)),
            tok3, tok3,
            pl.BlockSpec((None, SUBLANES, LANES), lambda i: (i // per_b, 0, 0)),
            pl.BlockSpec(memory_space=pl.ANY),
        ],
        out_specs=tok3,
        out_shape=jax.ShapeDtypeStruct((n, SUBLANES, LANES), F32),
        scratch_shapes=[pltpu.VMEM((tb * PEER_SEL, _ROW_SUB, LANES), F32),
                        pltpu.VMEM((tb * PEER_SEL, _ROW_SUB, LANES), F32),
                        pltpu.VMEM((2 * tb * PEER_SEL, LANES), F32),
                        pltpu.SemaphoreType.DMA((2 * tb,))],
        compiler_params=_cparams("arbitrary"),
        name="peer_gather",
    )(idx3, idx3, idx3, gate_p, h3, x3, g2_3, uv)


def _final_norm_kernel(x_ref, w_ref, o_ref):
    x = x_ref[...]
    o_ref[...] = x * lax.rsqrt(jnp.mean(x * x, axis=-1, keepdims=True) + RMS_EPS) * w_ref[...]


def _final_norm_call(x2d, w, tm=1024):
    n, d = x2d.shape
    return pl.pallas_call(
        _final_norm_kernel,
        grid=(n // tm,),
        in_specs=[pl.BlockSpec((tm, d), lambda i: (i, 0)), _resident((1, d))],
        out_specs=pl.BlockSpec((tm, d), lambda i: (i, 0)),
        out_shape=jax.ShapeDtypeStruct((n, d), F32),
        compiler_params=_cparams("parallel"),
        name="final_norm",
    )(x2d, w)


S5_LANES = S5_GROUPS * S5_STATE


def _s5_disc_kernel(are_ref, aim_ref, ldt_ref, bre_ref, bim_ref, ar_ref, ai_ref, br_ref, bi_ref):
    a_re = are_ref[...]
    a_im = aim_ref[...]
    dt = jnp.exp(ldt_ref[...])
    mag = jnp.exp(a_re * dt)
    ar = mag * jnp.cos(a_im * dt)
    ai = mag * jnp.sin(a_im * dt)
    den = a_re * a_re + a_im * a_im
    cr = ((ar - 1.0) * a_re + ai * a_im) / den
    ci = (ai * a_re - (ar - 1.0) * a_im) / den
    ar_ref[...] = ar
    ai_ref[...] = ai
    br_ref[...] = cr * bre_ref[...] - ci * bim_ref[...]
    bi_ref[...] = cr * bim_ref[...] + ci * bre_ref[...]


def _s5_disc_call(a_re, a_im, log_dt, b_re, b_im):
    rows = S5_LANES
    col = lambda t: t.reshape(rows, 1)
    ldt = jnp.broadcast_to(log_dt[:, None], (S5_GROUPS, S5_STATE))
    out = [jax.ShapeDtypeStruct((rows, 1), F32)] * 2 + [jax.ShapeDtypeStruct((rows, S5_GROUP), F32)] * 2
    return pl.pallas_call(_s5_disc_kernel, out_shape=out, name="s5_disc")(
        col(a_re), col(a_im), col(ldt), b_re.reshape(rows, S5_GROUP), b_im.reshape(rows, S5_GROUP))


def _s5_kernel(u_ref, bcat_ref, ar_ref, ai_ref, ccat_ref, d_ref, gw_ref, gb_ref, o_ref, st_scr, bu_scr, *, steps, bt):
    @pl.when(pl.program_id(0) == 0)
    def _():
        st_scr[...] = jnp.zeros_like(st_scr)

    u = u_ref[...]
    bu_scr[...] = jnp.dot(u.astype(BF16), bcat_ref[...], preferred_element_type=F32)
    a_re = jnp.broadcast_to(ar_ref[...], (bt, S5_LANES))
    a_im = jnp.broadcast_to(ai_ref[...], (bt, S5_LANES))

    def step(t, carry):
        s_re, s_im = carry
        r = pl.multiple_of(t * bt, bt)
        n_re = a_re * s_re - a_im * s_im + bu_scr[pl.ds(r, bt), 0:S5_LANES]
        n_im = a_re * s_im + a_im * s_re + bu_scr[pl.ds(r, bt), S5_LANES:2 * S5_LANES]
        bu_scr[pl.ds(r, bt), 0:S5_LANES] = n_re
        bu_scr[pl.ds(r, bt), S5_LANES:2 * S5_LANES] = n_im
        return n_re, n_im

    s_re, s_im = lax.fori_loop(0, steps, step, (st_scr[:, 0:S5_LANES], st_scr[:, S5_LANES:2 * S5_LANES]))
    st_scr[:, 0:S5_LANES] = s_re
    st_scr[:, S5_LANES:2 * S5_LANES] = s_im
    y = jnp.dot(bu_scr[...].astype(BF16), ccat_ref[...], preferred_element_type=F32) + d_ref[...] * u
    zg = 0.5 * y * (1.0 + lax.erf(y * (2.0 ** -0.5)))
    zz = jnp.dot(zg.astype(BF16), gw_ref[...], preferred_element_type=F32) + gb_ref[...]
    o_ref[...] = zz[:, 0:S5_WIDTH] * jax.nn.sigmoid(zz[:, S5_WIDTH:2 * S5_WIDTH])


def _s5_call(u_tb, bcat, abar_re, abar_im, ccat, d, glu_w, glu_b, bt, steps=64):
    rows = u_tb.shape[0]
    blk = steps * bt
    return pl.pallas_call(
        functools.partial(_s5_kernel, steps=steps, bt=bt),
        grid=(rows // blk,),
        in_specs=[pl.BlockSpec((blk, S5_WIDTH), lambda i: (i, 0)),
                  _resident(bcat.shape), _resident(abar_re.shape), _resident(abar_im.shape), _resident(ccat.shape),
                  _resident(d.shape), _resident(glu_w.shape), _resident(glu_b.shape)],
        out_specs=pl.BlockSpec((blk, S5_WIDTH), lambda i: (i, 0)),
        out_shape=jax.ShapeDtypeStruct((rows, S5_WIDTH), F32),
        scratch_shapes=[pltpu.VMEM((bt, 2 * S5_LANES), F32), pltpu.VMEM((blk, 2 * S5_LANES), F32)],
        compiler_params=_cparams("arbitrary"),
        name="s5_scan",
    )(u_tb, bcat, abar_re, abar_im, ccat, d, glu_w, glu_b)


def _s5_branch_pallas(p_ub, bt, seq, a_re, a_im, b_re, b_im, c_re, c_im, d, log_dt, glu_w, glu_b):
    ar, ai, br, bi = _s5_disc_call(a_re, a_im, log_dt, b_re, b_im)
    g_ix = jnp.arange(S5_GROUPS)
    def blockdiag_in(b):
        b = b.reshape(S5_GROUPS, S5_STATE, S5_GROUP)
        full = jnp.zeros((S5_GROUPS, S5_GROUP, S5_GROUPS, S5_STATE), F32)
        return full.at[g_ix, :, g_ix, :].set(b.transpose(0, 2, 1)).reshape(S5_WIDTH, S5_LANES)

    def blockdiag_out(c):
        full = jnp.zeros((S5_GROUPS, S5_STATE, S5_GROUPS, S5_GROUP), F32)
        return full.at[g_ix, :, g_ix, :].set(c.transpose(0, 2, 1)).reshape(S5_LANES, S5_WIDTH)

    bcat = jnp.concatenate([blockdiag_in(br), blockdiag_in(bi)], axis=1).astype(BF16)
    ccat = jnp.concatenate([blockdiag_out(c_re), -blockdiag_out(c_im)], axis=0).astype(BF16)
    u_tb = p_ub.reshape(bt, seq, S5_WIDTH).transpose(1, 0, 2).reshape(seq * bt, S5_WIDTH)
    o_tb = _s5_call(u_tb, bcat, ar.reshape(1, S5_LANES), ai.reshape(1, S5_LANES), ccat, d.reshape(1, S5_WIDTH),
                    glu_w.astype(BF16), glu_b.reshape(1, 2 * S5_WIDTH), bt)
    return o_tb.reshape(seq, bt, S5_WIDTH).transpose(1, 0, 2).reshape(bt * seq, S5_WIDTH)


def _mm(a, b):
    return jnp.dot(a.astype(BF16), b.astype(BF16), preferred_element_type=F32)


def _mm_nt(a, b, precision=None):
    if precision is None:
        a, b = a.astype(BF16), b.astype(BF16)
    return lax.dot_general(a, b, (((1,), (1,)), ((), ())), precision=precision, preferred_element_type=F32)


def _mm_tn(a, b):
    return lax.dot_general(a.astype(BF16), b.astype(BF16), (((0,), (0,)), ((), ())), preferred_element_type=F32)


def _softplus(x):
    return jnp.maximum(x, 0.0) + jnp.log1p(jnp.exp(-jnp.abs(x)))


def _gdn_kernel(pa_ref, ba_ref, cw_ref, alog_ref, dtb_ref, nw_ref, o_ref,
                st_scr, carry_scr, q_scr, k_scr, v_scr, g_scr, b_scr, u_scr, w_scr, qk_scr, gi_scr):
    j = pl.program_id(1)
    t_blk = pa_ref.shape[0]
    c_len, wid, dh = GDN_CHUNK, GDN_WIDTH, GDN_HEAD_DIM

    @pl.when(j == 0)
    def _():
        st_scr[...] = jnp.zeros_like(st_scr)
        carry_scr[...] = jnp.zeros_like(carry_scr)

    x = pa_ref[:, 0:3 * wid]
    xx = jnp.concatenate([carry_scr[...], x], axis=0)
    conv = cw_ref[GDN_CONV - 1:GDN_CONV, :] * x
    for k in range(1, GDN_CONV):
        conv = conv + cw_ref[GDN_CONV - 1 - k:GDN_CONV - k, :] * pltpu.roll(xx, k, axis=0)[SUBLANES:, :]
    carry_scr[...] = x[t_blk - SUBLANES:t_blk, :]
    act = conv * jax.nn.sigmoid(conv)
    for h in range(GDN_HEADS):
        qh = act[:, h * dh:(h + 1) * dh]
        kh = act[:, wid + h * dh:wid + (h + 1) * dh]
        q_scr[:, h * dh:(h + 1) * dh] = qh * lax.rsqrt(jnp.sum(qh * qh, axis=-1, keepdims=True) + 1e-6) * (dh ** -0.5)
        k_scr[:, h * dh:(h + 1) * dh] = kh * lax.rsqrt(jnp.sum(kh * kh, axis=-1, keepdims=True) + 1e-6)
    v_scr[...] = act[:, 2 * wid:3 * wid]
    ba = ba_ref[...]
    g_scr[...] = -jnp.exp(alog_ref[...]) * _softplus(ba + dtb_ref[...])
    b_scr[...] = jax.nn.sigmoid(ba)

    row = lax.broadcasted_iota(jnp.int32, (t_blk, t_blk), 0)
    col = lax.broadcasted_iota(jnp.int32, (t_blk, t_blk), 1)
    same = (row // c_len) == (col // c_len)
    tril = same & (row >= col)
    strict = same & (row > col)
    eye = jnp.where(row == col, 1.0, 0.0)
    lane = lax.broadcasted_iota(jnp.int32, (t_blk, LANES), 1)
    nw = nw_ref[...]
    gcum = jnp.dot(jnp.where(tril, 1.0, 0.0), g_scr[...], precision=HIGHEST, preferred_element_type=F32)
    beta_all = b_scr[...]
    heads = range(GDN_HEADS)
    hsl = [slice(h * dh, (h + 1) * dh) for h in heads]
    gis = [gcum[:, GDN_HEADS + h:GDN_HEADS + h + 1] for h in heads]
    betas = [beta_all[:, h:h + 1] for h in heads]
    decays, ps, ts = [], [], []
    for h in heads:
        g1 = jnp.where(lane == 0, gis[h], jnp.where(lane == 1, 1.0, 0.0))
        g2 = jnp.where(lane == 0, 1.0, jnp.where(lane == 1, -gis[h], 0.0))
        diff = _mm_nt(g1, g2, precision=HIGHEST)
        decays.append(jnp.where(tril, jnp.exp(jnp.where(tril, diff, 0.0)), 0.0))
    for h in heads:
        k_h = k_scr[:, hsl[h]]
        a = jnp.where(strict, _mm_nt(k_h * betas[h], k_h) * decays[h], 0.0)
        ps.append(-a)
        ts.append(eye - a)
    for _ in range(int(math.log2(c_len)) - 1):
        ps = [_mm(p, p) for p in ps]
        ts = [t + _mm(t, p) for t, p in zip(ts, ps)]
    for h in heads:
        k_h, v_h, q_h = k_scr[:, hsl[h]], v_scr[:, hsl[h]], q_scr[:, hsl[h]]
        e_g = jnp.exp(gis[h])
        u_scr[:, hsl[h]] = _mm(ts[h], v_h * betas[h])
        w_scr[:, hsl[h]] = _mm(ts[h], k_h * betas[h] * e_g)
        qk_scr[h] = jnp.where(tril, _mm_nt(q_h, k_h) * decays[h], 0.0)
        q_scr[:, hsl[h]] = q_h * e_g
        gi_scr[:, h:h + 1] = gis[h]

    for c in range(t_blk // c_len):
        rows = slice(c * c_len, (c + 1) * c_len)
        states = [st_scr[h] for h in heads]
        v_news = [u_scr[rows, hsl[h]] - _mm(w_scr[rows, hsl[h]], states[h]) for h in heads]
        o_state = [_mm(q_scr[rows, hsl[h]], states[h]) for h in heads]
        outs = [o_state[h] + _mm(qk_scr[h, rows, rows], v_news[h]) for h in heads]
        for h in heads:
            gi = gi_scr[rows, h:h + 1]
            g_last = gi[c_len - 1:c_len, :]
            st_scr[h] = states[h] * jnp.exp(g_last) + _mm_tn(k_scr[rows, hsl[h]] * jnp.exp(g_last - gi), v_news[h])
        for h in heads:
            o = outs[h]
            z = pa_ref[rows, 3 * wid + h * dh:3 * wid + (h + 1) * dh]
            o_ref[rows, hsl[h]] = (o * lax.rsqrt(jnp.mean(o * o, axis=-1, keepdims=True) + RMS_EPS) * nw
                                   * (z * jax.nn.sigmoid(z)))


def _gdn_branch_pallas(p_a, p_ba, conv_w, a_log, dt_bias, norm_w, bt, seq, t_blk=256):
    n = p_a.shape[0]
    nj = seq // t_blk
    wid = GDN_WIDTH
    lane_pad = lambda v: jnp.zeros((1, LANES), F32).at[0, GDN_HEADS:2 * GDN_HEADS].set(v)
    rows = lambda wd: pl.BlockSpec((t_blk, wd), lambda b, j: (b * nj + j, 0))
    return pl.pallas_call(
        _gdn_kernel,
        grid=(bt, nj),
        in_specs=[rows(4 * wid), rows(LANES), _resident((GDN_CONV, 3 * wid)), _resident((1, LANES)),
                  _resident((1, LANES)), _resident((1, GDN_HEAD_DIM))],
        out_specs=rows(wid),
        out_shape=jax.ShapeDtypeStruct((n, wid), F32),
        scratch_shapes=[pltpu.VMEM((GDN_HEADS, GDN_HEAD_DIM, GDN_HEAD_DIM), F32), pltpu.VMEM((SUBLANES, 3 * wid), F32),
                        pltpu.VMEM((t_blk, wid), F32), pltpu.VMEM((t_blk, wid), F32), pltpu.VMEM((t_blk, wid), F32),
                        pltpu.VMEM((t_blk, LANES), F32), pltpu.VMEM((t_blk, LANES), F32),
                        pltpu.VMEM((t_blk, wid), F32), pltpu.VMEM((t_blk, wid), F32),
                        pltpu.VMEM((GDN_HEADS, t_blk, t_blk), F32), pltpu.VMEM((t_blk, LANES), F32)],
        compiler_params=_cparams("parallel", "arbitrary"),
        name="gdn_delta",
    )(p_a, p_ba, conv_w, lane_pad(a_log), lane_pad(dt_bias), norm_w.reshape(1, GDN_HEAD_DIM))


def _moba_kernel(qt_ref, k_ref, vt_ref, o_ref, km_scr, sel_scr, qb_scr, *head_scr):
    i = pl.program_id(1)
    nh, nblk = k_ref.shape[0], k_ref.shape[1]
    scale = MOBA_HEAD_DIM ** -0.5
    m_scr, l_scr, acc_scr = head_scr[0:nh], head_scr[nh:2 * nh], head_scr[2 * nh:3 * nh]

    @pl.when(i == 0)
    def _():
        for h in range(nh):
            for n in range(nblk):
                km_scr[h, n:n + 1, :] = jnp.mean(k_ref[h, n], axis=0, keepdims=True)

    blk = lax.broadcasted_iota(jnp.int32, (nblk, MOBA_BLOCK), 0)
    kpos = lax.broadcasted_iota(jnp.int32, (MOBA_BLOCK, MOBA_BLOCK), 0)
    qpos = lax.broadcasted_iota(jnp.int32, (MOBA_BLOCK, MOBA_BLOCK), 1)
    for h in range(nh):
        qt = qt_ref[h]
        gate = jnp.dot(km_scr[h], qt, precision=HIGHEST, preferred_element_type=F32)
        cnt = jnp.zeros(gate.shape, F32)
        for m in range(nblk):
            gm = gate[m:m + 1, :]
            beats = (gm > gate) | ((gm == gate) & (m < blk))
            cnt = cnt + jnp.where(beats & (m < i), 1.0, 0.0)
        sel_scr[h] = jnp.where((blk < i) & (cnt < float(MOBA_TOPK)), 1.0, 0.0)
        qb = qt.astype(BF16)
        qb_scr[h] = qb
        s = jnp.dot(k_ref[h, i].astype(BF16), qb, preferred_element_type=F32) * scale
        s = jnp.where(kpos <= qpos, s, NEG_INF)
        m0 = jnp.max(s, axis=0, keepdims=True)
        p = jnp.exp(s - m0)
        m_scr[h][...] = m0
        l_scr[h][...] = jnp.sum(p, axis=0, keepdims=True)
        acc_scr[h][...] = jnp.dot(vt_ref[h, i].astype(BF16), p.astype(BF16), preferred_element_type=F32)

    def body(n, carry):
        scores = [jnp.dot(k_ref[h, n].astype(BF16), qb_scr[h], preferred_element_type=F32) for h in range(nh)]
        probs, alphas = [], []
        for h in range(nh):
            m = m_scr[h][...]
            s = jnp.where(sel_scr[h, pl.ds(n, 1), :] > 0.5, scores[h] * scale, NEG_INF)
            m_new = jnp.maximum(m, jnp.max(s, axis=0, keepdims=True))
            p = jnp.exp(s - m_new)
            alpha = jnp.exp(m - m_new)
            m_scr[h][...] = m_new
            l_scr[h][...] = alpha * l_scr[h][...] + jnp.sum(p, axis=0, keepdims=True)
            probs.append(p.astype(BF16))
            alphas.append(alpha)
        for h in range(nh):
            acc_scr[h][...] = alphas[h] * acc_scr[h][...] + jnp.dot(vt_ref[h, n].astype(BF16), probs[h],
                                                                    preferred_element_type=F32)
        return carry

    lax.fori_loop(0, i, body, 0)
    for h in range(nh):
        o_ref[h] = acc_scr[h][...] / l_scr[h][...]


def _moba_branch_pallas(p_c, bt, seq):
    nh, dh, bs = MOBA_HEADS, MOBA_HEAD_DIM, MOBA_BLOCK
    assert seq % bs == 0
    nblk = seq // bs
    qkv = p_c.reshape(bt, seq, 3, nh, dh)
    qt = qkv[:, :, 0].transpose(0, 2, 3, 1)
    k5 = qkv[:, :, 1].reshape(bt, nblk, bs, nh, dh).transpose(0, 3, 1, 2, 4)
    vt5 = qkv[:, :, 2].reshape(bt, nblk, bs, nh, dh).transpose(0, 3, 1, 4, 2)
    ot = pl.pallas_call(
        _moba_kernel,
        grid=(bt, nblk),
        in_specs=[pl.BlockSpec((None, nh, dh, bs), lambda b, i: (b, 0, 0, i)),
                  pl.BlockSpec((None, nh, nblk, bs, dh), lambda b, i: (b, 0, 0, 0, 0)),
                  pl.BlockSpec((None, nh, nblk, dh, bs), lambda b, i: (b, 0, 0, 0, 0))],
        out_specs=pl.BlockSpec((None, nh, dh, bs), lambda b, i: (b, 0, 0, i)),
        out_shape=jax.ShapeDtypeStruct((bt, nh, dh, seq), F32),
        scratch_shapes=[pltpu.VMEM((nh, nblk, dh), F32), pltpu.VMEM((nh, nblk, bs), F32),
                        pltpu.VMEM((nh, dh, bs), BF16)]
        + [pltpu.VMEM((1, bs), F32)] * (2 * nh) + [pltpu.VMEM((dh, bs), F32)] * nh,
        compiler_params=_cparams("parallel", "arbitrary"),
        name="moba_attn",
    )(qt, k5, vt5)
    return ot.transpose(0, 3, 1, 2).reshape(bt * seq, nh * dh)


def kernel(x, c, ada_w, ada_b, norm1_w, w_in, gdn_conv_w, gdn_a_log, gdn_dt_bias, gdn_norm_w, s5_a_re, s5_a_im, s5_b_re, s5_b_im, s5_c_re, s5_c_im, s5_d, s5_log_dt, s5_glu_w, s5_glu_b, w_branch_a, w_branch_b, w_branch_c, w_out, norm2_w, peer_wq, peer_k1, peer_k2, peer_u, peer_v, final_norm_w):
    bt, seq, d = x.shape
    n = bt * seq
    depth = ada_w.shape[0]
    x2d = x.reshape(n, d)
    mod = _ada_call(c, ada_w, ada_b)
    gate_perm = np.asarray(_SLOT_AT, np.int32)
    for l in range(depth):
        sh1, sc1, g1, sh2, sc2, g2 = (mod[l, :, j * d:(j + 1) * d].reshape(bt, 1, d) for j in range(6))
        p_a, p_ub, p_c, p_gate, p_ba = _in_proj_call(
            x2d, norm1_w[l].reshape(1, d), sc1, sh1, _arrange_w_in(w_in[l]), seq)
        o_a = _gdn_branch_pallas(p_a, p_ba, gdn_conv_w[l], gdn_a_log[l], gdn_dt_bias[l], gdn_norm_w[l], bt, seq)
        o_b = _s5_branch_pallas(p_ub, bt, seq, s5_a_re[l], s5_a_im[l], s5_b_re[l], s5_b_im[l], s5_c_re[l],
                                s5_c_im[l], s5_d[l], s5_log_dt[l], s5_glu_w[l], s5_glu_b[l])
        o_c = _moba_branch_pallas(p_c, bt, seq)
        x2d = _merge_call(o_a, o_b, o_c, p_gate, x2d, g1, w_branch_a[l].astype(BF16), w_branch_b[l].astype(BF16),
                          w_branch_c[l].astype(BF16), w_out[l].astype(BF16), seq)
        h2, idx_t, gate_t = _peer_query_call(x2d, norm2_w[l].reshape(1, d), sc2, sh2,
                                             peer_wq[l].T.astype(BF16), peer_k1[l], peer_k2[l], seq)
        uv = jnp.concatenate([peer_u[l].reshape(-1, SUBLANES, LANES), peer_v[l].reshape(-1, SUBLANES, LANES)], axis=1)
        x3 = _peer_gather_call(idx_t.T, gate_t.T[:, gate_perm], h2.reshape(n, SUBLANES, LANES),
                               x2d.reshape(n, SUBLANES, LANES), g2.reshape(bt, SUBLANES, LANES), uv, seq)
        x2d = x3.reshape(n, d)
    return _final_norm_call(x2d, final_norm_w.reshape(1, d)).reshape(bt, seq, d)
```

```python
import functools
import math

import jax
import jax.numpy as jnp
import numpy as np
from jax import lax
from jax.experimental import pallas as pl
from jax.experimental.pallas import tpu as pltpu

F32 = jnp.float32
BF16 = jnp.bfloat16
HIGHEST = lax.Precision.HIGHEST

D_MODEL = 1024
GDN_HEADS = 4
GDN_HEAD_DIM = 128
GDN_WIDTH = GDN_HEADS * GDN_HEAD_DIM
GDN_CONV = 4
GDN_CHUNK = 64
S5_GROUP = 16
S5_GROUPS = 16
S5_WIDTH = S5_GROUPS * S5_GROUP
S5_STATE = 64
MOBA_HEADS = 4
MOBA_HEAD_DIM = 64
MOBA_WIDTH = MOBA_HEADS * MOBA_HEAD_DIM
MOBA_BLOCK = 256
MOBA_TOPK = 3
MOBA_Q_CHUNK = 64
N_BRANCH = 3
PEER_HEADS = 8
PEER_NKEYS = 128
PEER_QDIM = 256
PEER_TOPK = 16
PEER_SEL = PEER_HEADS * PEER_TOPK
RMS_EPS = 1e-6
NEG_INF = -1e30

SUBLANES = 8
LANES = 128
VMEM_LIMIT_BYTES = 56 * 1024 * 1024

_OFF_QKV_A = 0
_OFF_Z_A = 3 * GDN_WIDTH
_OFF_BETA = _OFF_Z_A + GDN_WIDTH
_OFF_ALPHA = _OFF_BETA + GDN_HEADS
_OFF_UB = _OFF_ALPHA + GDN_HEADS
_OFF_QKV_C = _OFF_UB + S5_WIDTH
_OFF_GATE = _OFF_QKV_C + 3 * MOBA_WIDTH
_IN_COLS = _OFF_GATE + N_BRANCH * D_MODEL


def _cparams(*sem):
    return pltpu.CompilerParams(dimension_semantics=sem, vmem_limit_bytes=VMEM_LIMIT_BYTES)


def _resident(shape):
    nd = len(shape)
    return pl.BlockSpec(shape, lambda *_: (0,) * nd)


def _ada_kernel(c_ref, w_ref, b_ref, o_ref):
    c = c_ref[...]
    sc = c * jax.nn.sigmoid(c)
    o_ref[...] = jnp.dot(sc, w_ref[...], precision=HIGHEST, preferred_element_type=F32) + b_ref[...]


def _ada_call(c, ada_w, ada_b):
    depth, d, d6 = ada_w.shape
    bt = c.shape[0]
    nj = d6 // d
    return pl.pallas_call(
        _ada_kernel,
        grid=(depth, nj),
        in_specs=[
            pl.BlockSpec((bt, d), lambda l, j: (0, 0)),
            pl.BlockSpec((None, d, d), lambda l, j: (l, 0, j)),
            pl.BlockSpec((None, 1, d), lambda l, j: (l, 0, j)),
        ],
        out_specs=pl.BlockSpec((None, bt, d), lambda l, j: (l, 0, j)),
        out_shape=jax.ShapeDtypeStruct((depth, bt, d6), F32),
        compiler_params=_cparams("parallel", "parallel"),
        name="ada_mod",
    )(c, ada_w, ada_b.reshape(depth, 1, d6))


def _norm_mod(x, nw, sc, sh):
    y = x * lax.rsqrt(jnp.mean(x * x, axis=-1, keepdims=True) + RMS_EPS)
    return (y * nw) * (1.0 + sc) + sh


_IN_SPLITS = (4 * GDN_WIDTH, S5_WIDTH, 3 * MOBA_WIDTH, N_BRANCH * D_MODEL, LANES)


def _in_proj_kernel(x_ref, nw_ref, sc_ref, sh_ref, w_ref, oa_ref, ob_ref, oc_ref, og_ref, oba_ref):
    h = _norm_mod(x_ref[...], nw_ref[...], sc_ref[...], sh_ref[...]).astype(BF16)
    off = 0
    for o_ref, width in zip((oa_ref, ob_ref, oc_ref, og_ref, oba_ref), _IN_SPLITS):
        o_ref[...] = jnp.dot(h, w_ref[:, off:off + width], preferred_element_type=F32)
        off += width


def _in_proj_call(x2d, nw, sc, sh, w_r, seq, tm=256):
    n, d = x2d.shape
    per_b = seq // tm
    wcols = w_r.shape[1]
    outs = [jax.ShapeDtypeStruct((n, wd), F32) for wd in _IN_SPLITS]
    return pl.pallas_call(
        _in_proj_kernel,
        grid=(n // tm,),
        in_specs=[
            pl.BlockSpec((tm, d), lambda i: (i, 0)),
            _resident((1, d)),
            pl.BlockSpec((None, 1, d), lambda i: (i // per_b, 0, 0)),
            pl.BlockSpec((None, 1, d), lambda i: (i // per_b, 0, 0)),
            _resident((d, wcols)),
        ],
        out_specs=[pl.BlockSpec((tm, wd), lambda i: (i, 0)) for wd in _IN_SPLITS],
        out_shape=outs,
        compiler_params=_cparams("parallel"),
        name="in_proj",
    )(x2d, nw, sc, sh, w_r)


def _arrange_w_in(w_in_l):
    pad = jnp.zeros((w_in_l.shape[0], LANES - 2 * GDN_HEADS), w_in_l.dtype)
    return jnp.concatenate([
        w_in_l[:, _OFF_QKV_A:_OFF_BETA],
        w_in_l[:, _OFF_UB:_OFF_QKV_C],
        w_in_l[:, _OFF_QKV_C:_OFF_GATE],
        w_in_l[:, _OFF_GATE:_IN_COLS],
        w_in_l[:, _OFF_BETA:_OFF_UB], pad,
    ], axis=1).astype(BF16)


def _merge_kernel(oa_ref, ob_ref, oc_ref, gt_ref, x_ref, g1_ref, wa_ref, wb_ref, wc_ref, wo_ref, o_ref):
    d = D_MODEL
    ya = jnp.dot(oa_ref[...].astype(BF16), wa_ref[...], preferred_element_type=F32)
    yb = jnp.dot(ob_ref[...].astype(BF16), wb_ref[...], preferred_element_type=F32)
    yc = jnp.dot(oc_ref[...].astype(BF16), wc_ref[...], preferred_element_type=F32)
    merged = (jax.nn.sigmoid(gt_ref[:, 0:d]) * ya + jax.nn.sigmoid(gt_ref[:, d:2 * d]) * yb
              + jax.nn.sigmoid(gt_ref[:, 2 * d:3 * d]) * yc)
    y = jnp.dot(merged.astype(BF16), wo_ref[...], preferred_element_type=F32)
    o_ref[...] = x_ref[...] + g1_ref[...] * y


def _merge_call(o_a, o_b, o_c, p_gate, x2d, g1, wa, wb, wc, wo, seq, tm=512):
    n, d = x2d.shape
    per_b = seq // tm
    row = lambda wd: pl.BlockSpec((tm, wd), lambda i: (i, 0))
    return pl.pallas_call(
        _merge_kernel,
        grid=(n // tm,),
        in_specs=[row(GDN_WIDTH), row(S5_WIDTH), row(MOBA_WIDTH), row(N_BRANCH * d), row(d),
                  pl.BlockSpec((None, 1, d), lambda i: (i // per_b, 0, 0)),
                  _resident(wa.shape), _resident(wb.shape), _resident(wc.shape), _resident(wo.shape)],
        out_specs=row(d),
        out_shape=jax.ShapeDtypeStruct((n, d), F32),
        compiler_params=_cparams("parallel"),
        name="merge_out",
    )(o_a, o_b, o_c, p_gate, x2d, g1, wa, wb, wc, wo)


def _candidate_tables():
    k = PEER_TOPK
    pairs = [(a, b) for a in range(k) for b in range(k) if (a + 1) * (b + 1) <= k]
    rows = -(-len(pairs) // SUBLANES) * SUBLANES
    sel = np.zeros((rows, 2 * k), np.float32)
    sel_id = np.zeros((rows, 2 * k), np.float32)
    bias = np.zeros((rows, 1), np.float32)
    order = np.zeros((rows, 1), np.float32)
    for r, (a, b) in enumerate(pairs):
        sel[r, a] = sel[r, k + b] = 1.0
        sel_id[r, a] = float(PEER_NKEYS)
        sel_id[r, k + b] = 1.0
        order[r, 0] = a * k + b
    for r in range(len(pairs), rows):
        bias[r, 0] = -np.inf
        order[r, 0] = k * k + r
    return sel, sel_id, bias, order


_CAND_SEL, _CAND_SEL_ID, _CAND_BIAS, _CAND_ORDER = _candidate_tables()
_CAND_ROWS = _CAND_SEL.shape[0]


def _extract_max(s, order, big, payload=None):
    m = jnp.max(s, axis=0, keepdims=True)
    pos = jnp.min(jnp.where(s == m, order, big), axis=0, keepdims=True)
    hit = order == pos
    tag = pos if payload is None else jnp.sum(jnp.where(hit, payload, 0.0), axis=0, keepdims=True)
    return m, tag, jnp.where(hit, -jnp.inf, s)


def _peer_query_kernel(x_ref, nw_ref, sc_ref, sh_ref, wqt_ref, k1_ref, k2_ref, sel_ref, selid_ref, cb_ref, co_ref,
                       h_ref, idx_ref, gate_ref, q_scr, s1_scr, s2_scr, v_scr, i_scr, cs_scr, ci_scr, ts_scr, te_scr):
    h = _norm_mod(x_ref[...], nw_ref[...], sc_ref[...], sh_ref[...])
    h_ref[...] = h
    q_scr[...] = lax.dot_general(wqt_ref[...], h.astype(BF16), (((1,), (1,)), ((), ())),
                                 preferred_element_type=F32)
    half = PEER_QDIM // 2
    k = PEER_TOPK
    key_iota = lax.broadcasted_iota(jnp.int32, s1_scr.shape, 0).astype(F32)
    cand_order = jnp.broadcast_to(co_ref[...], cs_scr.shape)

    for hd in range(PEER_HEADS + 1):
        do_keys, do_cand = hd < PEER_HEADS, hd > 0
        if do_keys:
            base = hd * PEER_QDIM
            s1_scr[...] = jnp.dot(k1_ref[hd], q_scr[base:base + half, :], precision=HIGHEST,
                                  preferred_element_type=F32)
            s2_scr[...] = jnp.dot(k2_ref[hd], q_scr[base + half:base + 2 * half, :], precision=HIGHEST,
                                  preferred_element_type=F32)

        def trip(j, c, do_keys=do_keys, do_cand=do_cand):
            if do_keys:
                m1, p1, s1 = _extract_max(s1_scr[...], key_iota, float(PEER_NKEYS))
                m2, p2, s2 = _extract_max(s2_scr[...], key_iota, float(PEER_NKEYS))
                s1_scr[...] = s1
                s2_scr[...] = s2
                v_scr[pl.ds(j, 1), :] = m1
                i_scr[pl.ds(j, 1), :] = p1
                v_scr[pl.ds(j + k, 1), :] = m2
                i_scr[pl.ds(j + k, 1), :] = p2
            if do_cand:
                m, e, s = _extract_max(cs_scr[...], cand_order, float(4 * k * k), payload=ci_scr[...])
                cs_scr[...] = s
                ts_scr[pl.ds(j, 1), :] = m
                te_scr[pl.ds(j, 1), :] = e
            return c

        lax.fori_loop(0, k, trip, 0)
        if do_cand:
            ts = ts_scr[...]
            e = jnp.exp(ts - jnp.max(ts, axis=0, keepdims=True))
            out = (hd - 1) * k
            gate_ref[out:out + k, :] = e / jnp.sum(e, axis=0, keepdims=True)
            idx_ref[out:out + k, :] = te_scr[...].astype(jnp.int32)
        if do_keys:
            cs_scr[...] = (jnp.dot(sel_ref[...], v_scr[...], precision=HIGHEST, preferred_element_type=F32)
                           + cb_ref[...])
            ci_scr[...] = jnp.dot(selid_ref[...], i_scr[...].astype(BF16), preferred_element_type=F32)


def _peer_query_call(x2d, nw, sc, sh, wqt, k1, k2, seq, tm=256):
    n, d = x2d.shape
    per_b = seq // tm
    nq = PEER_HEADS * PEER_QDIM
    k = PEER_TOPK
    scratch = [pltpu.VMEM((nq, tm), F32), pltpu.VMEM((PEER_NKEYS, tm), F32), pltpu.VMEM((PEER_NKEYS, tm), F32),
               pltpu.VMEM((2 * k, tm), F32), pltpu.VMEM((2 * k, tm), F32),
               pltpu.VMEM((_CAND_ROWS, tm), F32), pltpu.VMEM((_CAND_ROWS, tm), F32),
               pltpu.VMEM((k, tm), F32), pltpu.VMEM((k, tm), F32)]
    cand = [jnp.asarray(_CAND_SEL), jnp.asarray(_CAND_SEL_ID, dtype=BF16), jnp.asarray(_CAND_BIAS),
            jnp.asarray(_CAND_ORDER)]
    return pl.pallas_call(
        _peer_query_kernel,
        grid=(n // tm,),
        in_specs=[
            pl.BlockSpec((tm, d), lambda i: (i, 0)),
            _resident((1, d)),
            pl.BlockSpec((None, 1, d), lambda i: (i // per_b, 0, 0)),
            pl.BlockSpec((None, 1, d), lambda i: (i // per_b, 0, 0)),
            _resident(wqt.shape), _resident(k1.shape), _resident(k2.shape),
        ] + [_resident(t.shape) for t in cand],
        out_specs=[pl.BlockSpec((tm, d), lambda i: (i, 0)),
                   pl.BlockSpec((PEER_SEL, tm), lambda i: (0, i)),
                   pl.BlockSpec((PEER_SEL, tm), lambda i: (0, i))],
        out_shape=[jax.ShapeDtypeStruct((n, d), F32),
                   jax.ShapeDtypeStruct((PEER_SEL, n), jnp.int32),
                   jax.ShapeDtypeStruct((PEER_SEL, n), F32)],
        scratch_shapes=scratch,
        compiler_params=_cparams("parallel"),
        name="peer_query",
    )(x2d, nw, sc, sh, wqt, k1, k2, *cand)


PEER_TOKENS_PER_STEP = 8
PEER_DMA_THREADS = 2
_ROW_SUB = 2 * SUBLANES


def _fold_pair(a, b, shift, keep_a):
    fa = a + pltpu.roll(a, shift, axis=0)
    fb = b + pltpu.roll(b, SUBLANES - shift, axis=0)
    return jnp.where(keep_a, fa, fb)


def _slot_order():
    tiles = [[k] * SUBLANES for k in range(PEER_SEL)]
    for shift, keep in ((4, [j < 4 for j in range(8)]), (2, [(j % 4) >= 2 for j in range(8)]),
                        (1, [(j % 2) == 1 for j in range(8)])):
        tiles = [[tiles[2 * m][j] if keep[j] else tiles[2 * m + 1][j] for j in range(8)]
                 for m in range(len(tiles) // 2)]
    return [s for t in tiles for s in t]


_SLOT_AT = _slot_order()
_POS_OF = [0] * PEER_SEL
for _p, _s in enumerate(_SLOT_AT):
    _POS_OF[_s] = _p


def _peer_gather_kernel(idx0_ref, idxc_ref, idxn_ref, g_ref, h_ref, x_ref, g2_ref, uv_hbm, o_ref,
                        buf0, buf1, wb_scr, sems):
    tb = PEER_TOKENS_PER_STEP
    blk = tb * PEER_SEL
    i = pl.program_id(0)
    bufs = (buf0, buf1)

    def row_copy(e, half, row, t):
        return pltpu.make_async_copy(uv_hbm.at[e], bufs[half].at[row], sems.at[half * tb + t])

    def slot_copy(half, t):
        return pltpu.make_async_copy(uv_hbm.at[pl.ds(0, PEER_SEL)], bufs[half].at[pl.ds(t * PEER_SEL, PEER_SEL)],
                                     sems.at[half * tb + t])

    @pl.when(i == 0)
    def _():
        def issue0(t, c):
            for k in range(PEER_SEL):
                row_copy(idx0_ref[0, t * PEER_SEL + k], 0, t * PEER_SEL + k, t).start()
            return c
        lax.fori_loop(0, tb, issue0, 0)

    sub = lax.broadcasted_iota(jnp.int32, (SUBLANES, LANES), 0)
    keep4 = sub < 4
    keep2 = (sub % 4) >= 2
    keep1 = (sub % 2) == 1
    eye = (lax.broadcasted_iota(jnp.int32, (PEER_SEL, LANES), 0)
           == lax.broadcasted_iota(jnp.int32, (PEER_SEL, LANES), 1))

    for half in range(2):
        nidx_ref, noff = (idxc_ref, blk) if half == 0 else (idxn_ref, 0)
        buf = bufs[half]
        for t in range(tb):
            tt = half * tb + t
            slot_copy(half, t).wait()
            for k in range(PEER_SEL):
                row_copy(nidx_ref[0, noff + t * PEER_SEL + k], 1 - half, t * PEER_SEL + k, t).start(
                    priority=k % PEER_DMA_THREADS)
            base = t * PEER_SEL
            ht = h_ref[tt]
            tiles = [buf[base + k, 0:SUBLANES, :] * ht for k in range(PEER_SEL)]
            tiles = [_fold_pair(tiles[2 * m], tiles[2 * m + 1], 4, keep4) for m in range(PEER_SEL // 2)]
            tiles = [_fold_pair(tiles[2 * m], tiles[2 * m + 1], 2, keep2) for m in range(PEER_SEL // 4)]
            tiles = [_fold_pair(tiles[2 * m], tiles[2 * m + 1], 1, keep1) for m in range(PEER_SEL // 8)]
            part = jnp.concatenate(tiles, axis=0)
            s = jnp.sum(part, axis=-1, keepdims=True)
            act = 0.5 * s * (1.0 + lax.erf(s * (2.0 ** -0.5)))
            grow = jnp.broadcast_to(g_ref[tt:tt + 1, :], (PEER_SEL, LANES))
            gcol = jnp.sum(jnp.where(eye, grow, 0.0), axis=-1, keepdims=True)
            wbase = tt * PEER_SEL
            wb_scr[wbase:wbase + PEER_SEL, :] = jnp.broadcast_to(gcol * act, (PEER_SEL, LANES))
            accs = [jnp.zeros((SUBLANES, LANES), F32) for _ in range(4)]
            for k in range(PEER_SEL):
                p = wbase + _POS_OF[k]
                wk = jnp.broadcast_to(wb_scr[p:p + 1, :], (SUBLANES, LANES))
                accs[k % 4] = accs[k % 4] + wk * buf[base + k, SUBLANES:_ROW_SUB, :]
            y = (accs[0] + accs[1]) + (accs[2] + accs[3])
            o_ref[tt] = x_ref[tt] + g2_ref[...] * y

    @pl.when(i == pl.num_programs(0) - 1)
    def _():
        for t in range(tb):
            slot_copy(0, t).wait()


def _peer_gather_call(idx, gate_p, h3, x3, g2_3, uv, seq):
    n = idx.shape[0]
    tb = PEER_TOKENS_PER_STEP
    ns = n // (2 * tb)
    per_b = seq // (2 * tb)
    idx3 = idx.reshape(ns, 1, 2 * tb * PEER_SEL)
    smem_blk = lambda fn: pl.BlockSpec((None, 1, 2 * tb * PEER_SEL), fn, memory_space=pltpu.SMEM)
    tok3 = pl.BlockSpec((2 * tb, SUBLANES, LANES), lambda i: (i, 0, 0))
    return pl.pallas_call(
        _peer_gather_kernel,
        grid=(ns,),
        in_specs=[
            smem_blk(lambda i: (0, 0, 0)),
            smem_blk(lambda i: (i, 0, 0)),
            smem_blk(lambda i: (jnp.minimum(i + 1, ns - 1), 0, 0)),
            pl.BlockSpec((2 * tb, PEER_SEL), lambda i: (i, 0)),
            tok3, tok3,
            pl.BlockSpec((None, SUBLANES, LANES), lambda i: (i // per_b, 0, 0)),
            pl.BlockSpec(memory_space=pl.ANY),
        ],
        out_specs=tok3,
        out_shape=jax.ShapeDtypeStruct((n, SUBLANES, LANES), F32),
        scratch_shapes=[pltpu.VMEM((tb * PEER_SEL, _ROW_SUB, LANES), F32),
                        pltpu.VMEM((tb * PEER_SEL, _ROW_SUB, LANES), F32),
                        pltpu.VMEM((2 * tb * PEER_SEL, LANES), F32),
                        pltpu.SemaphoreType.DMA((2 * tb,))],
        compiler_params=_cparams("arbitrary"),
        name="peer_gather",
    )(idx3, idx3, idx3, gate_p, h3, x3, g2_3, uv)


def _final_norm_kernel(x_ref, w_ref, o_ref):
    x = x_ref[...]
    o_ref[...] = x * lax.rsqrt(jnp.mean(x * x, axis=-1, keepdims=True) + RMS_EPS) * w_ref[...]


def _final_norm_call(x2d, w, tm=1024):
    n, d = x2d.shape
    return pl.pallas_call(
        _final_norm_kernel,
        grid=(n // tm,),
        in_specs=[pl.BlockSpec((tm, d), lambda i: (i, 0)), _resident((1, d))],
        out_specs=pl.BlockSpec((tm, d), lambda i: (i, 0)),
        out_shape=jax.ShapeDtypeStruct((n, d), F32),
        compiler_params=_cparams("parallel"),
        name="final_norm",
    )(x2d, w)


S5_LANES = S5_GROUPS * S5_STATE


def _s5_disc_kernel(are_ref, aim_ref, ldt_ref, bre_ref, bim_ref, ar_ref, ai_ref, br_ref, bi_ref):
    a_re = are_ref[...]
    a_im = aim_ref[...]
    dt = jnp.exp(ldt_ref[...])
    mag = jnp.exp(a_re * dt)
    ar = mag * jnp.cos(a_im * dt)
    ai = mag * jnp.sin(a_im * dt)
    den = a_re * a_re + a_im * a_im
    cr = ((ar - 1.0) * a_re + ai * a_im) / den
    ci = (ai * a_re - (ar - 1.0) * a_im) / den
    ar_ref[...] = ar
    ai_ref[...] = ai
    br_ref[...] = cr * bre_ref[...] - ci * bim_ref[...]
    bi_ref[...] = cr * bim_ref[...] + ci * bre_ref[...]


def _s5_disc_call(a_re, a_im, log_dt, b_re, b_im):
    rows = S5_LANES
    col = lambda t: t.reshape(rows, 1)
    ldt = jnp.broadcast_to(log_dt[:, None], (S5_GROUPS, S5_STATE))
    out = [jax.ShapeDtypeStruct((rows, 1), F32)] * 2 + [jax.ShapeDtypeStruct((rows, S5_GROUP), F32)] * 2
    return pl.pallas_call(_s5_disc_kernel, out_shape=out, name="s5_disc")(
        col(a_re), col(a_im), col(ldt), b_re.reshape(rows, S5_GROUP), b_im.reshape(rows, S5_GROUP))


def _s5_kernel(u_ref, bcat_ref, ar_ref, ai_ref, ccat_ref, d_ref, gw_ref, gb_ref, o_ref, st_scr, bu_scr, *, steps, bt):
    @pl.when(pl.program_id(0) == 0)
    def _():
        st_scr[...] = jnp.zeros_like(st_scr)

    u = u_ref[...]
    bu_scr[...] = jnp.dot(u.astype(BF16), bcat_ref[...], preferred_element_type=F32)
    a_re = jnp.broadcast_to(ar_ref[...], (bt, S5_LANES))
    a_im = jnp.broadcast_to(ai_ref[...], (bt, S5_LANES))

    def step(t, carry):
        s_re, s_im = carry
        r = pl.multiple_of(t * bt, bt)
        n_re = a_re * s_re - a_im * s_im + bu_scr[pl.ds(r, bt), 0:S5_LANES]
        n_im = a_re * s_im + a_im * s_re + bu_scr[pl.ds(r, bt), S5_LANES:2 * S5_LANES]
        bu_scr[pl.ds(r, bt), 0:S5_LANES] = n_re
        bu_scr[pl.ds(r, bt), S5_LANES:2 * S5_LANES] = n_im
        return n_re, n_im

    s_re, s_im = lax.fori_loop(0, steps, step, (st_scr[:, 0:S5_LANES], st_scr[:, S5_LANES:2 * S5_LANES]))
    st_scr[:, 0:S5_LANES] = s_re
    st_scr[:, S5_LANES:2 * S5_LANES] = s_im
    y = jnp.dot(bu_scr[...].astype(BF16), ccat_ref[...], preferred_element_type=F32) + d_ref[...] * u
    zg = 0.5 * y * (1.0 + lax.erf(y * (2.0 ** -0.5)))
    zz = jnp.dot(zg.astype(BF16), gw_ref[...], preferred_element_type=F32) + gb_ref[...]
    o_ref[...] = zz[:, 0:S5_WIDTH] * jax.nn.sigmoid(zz[:, S5_WIDTH:2 * S5_WIDTH])


def _s5_call(u_tb, bcat, abar_re, abar_im, ccat, d, glu_w, glu_b, bt, steps=64):
    rows = u_tb.shape[0]
    blk = steps * bt
    return pl.pallas_call(
        functools.partial(_s5_kernel, steps=steps, bt=bt),
        grid=(rows // blk,),
        in_specs=[pl.BlockSpec((blk, S5_WIDTH), lambda i: (i, 0)),
                  _resident(bcat.shape), _resident(abar_re.shape), _resident(abar_im.shape), _resident(ccat.shape),
                  _resident(d.shape), _resident(glu_w.shape), _resident(glu_b.shape)],
        out_specs=pl.BlockSpec((blk, S5_WIDTH), lambda i: (i, 0)),
        out_shape=jax.ShapeDtypeStruct((rows, S5_WIDTH), F32),
        scratch_shapes=[pltpu.VMEM((bt, 2 * S5_LANES), F32), pltpu.VMEM((blk, 2 * S5_LANES), F32)],
        compiler_params=_cparams("arbitrary"),
        name="s5_scan",
    )(u_tb, bcat, abar_re, abar_im, ccat, d, glu_w, glu_b)


def _s5_branch_pallas(p_ub, bt, seq, a_re, a_im, b_re, b_im, c_re, c_im, d, log_dt, glu_w, glu_b):
    ar, ai, br, bi = _s5_disc_call(a_re, a_im, log_dt, b_re, b_im)
    g_ix = jnp.arange(S5_GROUPS)
    def blockdiag_in(b):
        b = b.reshape(S5_GROUPS, S5_STATE, S5_GROUP)
        full = jnp.zeros((S5_GROUPS, S5_GROUP, S5_GROUPS, S5_STATE), F32)
        return full.at[g_ix, :, g_ix, :].set(b.transpose(0, 2, 1)).reshape(S5_WIDTH, S5_LANES)

    def blockdiag_out(c):
        full = jnp.zeros((S5_GROUPS, S5_STATE, S5_GROUPS, S5_GROUP), F32)
        return full.at[g_ix, :, g_ix, :].set(c.transpose(0, 2, 1)).reshape(S5_LANES, S5_WIDTH)

    bcat = jnp.concatenate([blockdiag_in(br), blockdiag_in(bi)], axis=1).astype(BF16)
    ccat = jnp.concatenate([blockdiag_out(c_re), -blockdiag_out(c_im)], axis=0).astype(BF16)
    u_tb = p_ub.reshape(bt, seq, S5_WIDTH).transpose(1, 0, 2).reshape(seq * bt, S5_WIDTH)
    o_tb = _s5_call(u_tb, bcat, ar.reshape(1, S5_LANES), ai.reshape(1, S5_LANES), ccat, d.reshape(1, S5_WIDTH),
                    glu_w.astype(BF16), glu_b.reshape(1, 2 * S5_WIDTH), bt)
    return o_tb.reshape(seq, bt, S5_WIDTH).transpose(1, 0, 2).reshape(bt * seq, S5_WIDTH)


def _mm(a, b):
    return jnp.dot(a.astype(BF16), b.astype(BF16), preferred_element_type=F32)


def _mm_nt(a, b, precision=None):
    if precision is None:
        a, b = a.astype(BF16), b.astype(BF16)
    return lax.dot_general(a, b, (((1,), (1,)), ((), ())), precision=precision, preferred_element_type=F32)


def _mm_tn(a, b):
    return lax.dot_general(a.astype(BF16), b.astype(BF16), (((0,), (0,)), ((), ())), preferred_element_type=F32)


def _softplus(x):
    return jnp.maximum(x, 0.0) + jnp.log1p(jnp.exp(-jnp.abs(x)))


def _gdn_kernel(pa_ref, ba_ref, cw_ref, alog_ref, dtb_ref, nw_ref, o_ref,
                st_scr, carry_scr, q_scr, k_scr, v_scr, g_scr, b_scr, u_scr, w_scr, qk_scr, gi_scr):
    j = pl.program_id(1)
    t_blk = pa_ref.shape[0]
    c_len, wid, dh = GDN_CHUNK, GDN_WIDTH, GDN_HEAD_DIM

    @pl.when(j == 0)
    def _():
        st_scr[...] = jnp.zeros_like(st_scr)
        carry_scr[...] = jnp.zeros_like(carry_scr)

    x = pa_ref[:, 0:3 * wid]
    xx = jnp.concatenate([carry_scr[...], x], axis=0)
    conv = cw_ref[GDN_CONV - 1:GDN_CONV, :] * x
    for k in range(1, GDN_CONV):
        conv = conv + cw_ref[GDN_CONV - 1 - k:GDN_CONV - k, :] * pltpu.roll(xx, k, axis=0)[SUBLANES:, :]
    carry_scr[...] = x[t_blk - SUBLANES:t_blk, :]
    act = conv * jax.nn.sigmoid(conv)
    for h in range(GDN_HEADS):
        qh = act[:, h * dh:(h + 1) * dh]
        kh = act[:, wid + h * dh:wid + (h + 1) * dh]
        q_scr[:, h * dh:(h + 1) * dh] = qh * lax.rsqrt(jnp.sum(qh * qh, axis=-1, keepdims=True) + 1e-6) * (dh ** -0.5)
        k_scr[:, h * dh:(h + 1) * dh] = kh * lax.rsqrt(jnp.sum(kh * kh, axis=-1, keepdims=True) + 1e-6)
    v_scr[...] = act[:, 2 * wid:3 * wid]
    ba = ba_ref[...]
    g_scr[...] = -jnp.exp(alog_ref[...]) * _softplus(ba + dtb_ref[...])
    b_scr[...] = jax.nn.sigmoid(ba)

    row = lax.broadcasted_iota(jnp.int32, (t_blk, t_blk), 0)
    col = lax.broadcasted_iota(jnp.int32, (t_blk, t_blk), 1)
    same = (row // c_len) == (col // c_len)
    tril = same & (row >= col)
    strict = same & (row > col)
    eye = jnp.where(row == col, 1.0, 0.0)
    lane = lax.broadcasted_iota(jnp.int32, (t_blk, LANES), 1)
    nw = nw_ref[...]
    gcum = jnp.dot(jnp.where(tril, 1.0, 0.0), g_scr[...], precision=HIGHEST, preferred_element_type=F32)
    beta_all = b_scr[...]
    heads = range(GDN_HEADS)
    hsl = [slice(h * dh, (h + 1) * dh) for h in heads]
    gis = [gcum[:, GDN_HEADS + h:GDN_HEADS + h + 1] for h in heads]
    betas = [beta_all[:, h:h + 1] for h in heads]
    decays, ps, ts = [], [], []
    for h in heads:
        g1 = jnp.where(lane == 0, gis[h], jnp.where(lane == 1, 1.0, 0.0))
        g2 = jnp.where(lane == 0, 1.0, jnp.where(lane == 1, -gis[h], 0.0))
        diff = _mm_nt(g1, g2, precision=HIGHEST)
        decays.append(jnp.where(tril, jnp.exp(jnp.where(tril, diff, 0.0)), 0.0))
    for h in heads:
        k_h = k_scr[:, hsl[h]]
        a = jnp.where(strict, _mm_nt(k_h * betas[h], k_h) * decays[h], 0.0)
        ps.append(-a)
        ts.append(eye - a)
    for _ in range(int(math.log2(c_len)) - 1):
        ps = [_mm(p, p) for p in ps]
        ts = [t + _mm(t, p) for t, p in zip(ts, ps)]
    for h in heads:
        k_h, v_h, q_h = k_scr[:, hsl[h]], v_scr[:, hsl[h]], q_scr[:, hsl[h]]
        e_g = jnp.exp(gis[h])
        u_scr[:, hsl[h]] = _mm(ts[h], v_h * betas[h])
        w_scr[:, hsl[h]] = _mm(ts[h], k_h * betas[h] * e_g)
        qk_scr[h] = jnp.where(tril, _mm_nt(q_h, k_h) * decays[h], 0.0)
        q_scr[:, hsl[h]] = q_h * e_g
        gi_scr[:, h:h + 1] = gis[h]

    for c in range(t_blk // c_len):
        rows = slice(c * c_len, (c + 1) * c_len)
        states = [st_scr[h] for h in heads]
        v_news = [u_scr[rows, hsl[h]] - _mm(w_scr[rows, hsl[h]], states[h]) for h in heads]
        o_state = [_mm(q_scr[rows, hsl[h]], states[h]) for h in heads]
        outs = [o_state[h] + _mm(qk_scr[h, rows, rows], v_news[h]) for h in heads]
        for h in heads:
            gi = gi_scr[rows, h:h + 1]
            g_last = gi[c_len - 1:c_len, :]
            st_scr[h] = states[h] * jnp.exp(g_last) + _mm_tn(k_scr[rows, hsl[h]] * jnp.exp(g_last - gi), v_news[h])
        for h in heads:
            o = outs[h]
            z = pa_ref[rows, 3 * wid + h * dh:3 * wid + (h + 1) * dh]
            o_ref[rows, hsl[h]] = (o * lax.rsqrt(jnp.mean(o * o, axis=-1, keepdims=True) + RMS_EPS) * nw
                                   * (z * jax.nn.sigmoid(z)))


def _gdn_branch_pallas(p_a, p_ba, conv_w, a_log, dt_bias, norm_w, bt, seq, t_blk=256):
    n = p_a.shape[0]
    nj = seq // t_blk
    wid = GDN_WIDTH
    lane_pad = lambda v: jnp.zeros((1, LANES), F32).at[0, GDN_HEADS:2 * GDN_HEADS].set(v)
    rows = lambda wd: pl.BlockSpec((t_blk, wd), lambda b, j: (b * nj + j, 0))
    return pl.pallas_call(
        _gdn_kernel,
        grid=(bt, nj),
        in_specs=[rows(4 * wid), rows(LANES), _resident((GDN_CONV, 3 * wid)), _resident((1, LANES)),
                  _resident((1, LANES)), _resident((1, GDN_HEAD_DIM))],
        out_specs=rows(wid),
        out_shape=jax.ShapeDtypeStruct((n, wid), F32),
        scratch_shapes=[pltpu.VMEM((GDN_HEADS, GDN_HEAD_DIM, GDN_HEAD_DIM), F32), pltpu.VMEM((SUBLANES, 3 * wid), F32),
                        pltpu.VMEM((t_blk, wid), F32), pltpu.VMEM((t_blk, wid), F32), pltpu.VMEM((t_blk, wid), F32),
                        pltpu.VMEM((t_blk, LANES), F32), pltpu.VMEM((t_blk, LANES), F32),
                        pltpu.VMEM((t_blk, wid), F32), pltpu.VMEM((t_blk, wid), F32),
                        pltpu.VMEM((GDN_HEADS, t_blk, t_blk), F32), pltpu.VMEM((t_blk, LANES), F32)],
        compiler_params=_cparams("parallel", "arbitrary"),
        name="gdn_delta",
    )(p_a, p_ba, conv_w, lane_pad(a_log), lane_pad(dt_bias), norm_w.reshape(1, GDN_HEAD_DIM))


def _moba_kernel(qt_ref, k_ref, vt_ref, o_ref, km_scr, sel_scr, qb_scr, *head_scr):
    i = pl.program_id(1)
    nh, nblk = k_ref.shape[0], k_ref.shape[1]
    scale = MOBA_HEAD_DIM ** -0.5
    m_scr, l_scr, acc_scr = head_scr[0:nh], head_scr[nh:2 * nh], head_scr[2 * nh:3 * nh]

    @pl.when(i == 0)
    def _():
        for h in range(nh):
            for n in range(nblk):
                km_scr[h, n:n + 1, :] = jnp.mean(k_ref[h, n], axis=0, keepdims=True)

    blk = lax.broadcasted_iota(jnp.int32, (nblk, MOBA_BLOCK), 0)
    kpos = lax.broadcasted_iota(jnp.int32, (MOBA_BLOCK, MOBA_BLOCK), 0)
    qpos = lax.broadcasted_iota(jnp.int32, (MOBA_BLOCK, MOBA_BLOCK), 1)
    for h in range(nh):
        qt = qt_ref[h]
        gate = jnp.dot(km_scr[h], qt, precision=HIGHEST, preferred_element_type=F32)
        cnt = jnp.zeros(gate.shape, F32)
        for m in range(nblk):
            gm = gate[m:m + 1, :]
            beats = (gm > gate) | ((gm == gate) & (m < blk))
            cnt = cnt + jnp.where(beats & (m < i), 1.0, 0.0)
        sel_scr[h] = jnp.where((blk < i) & (cnt < float(MOBA_TOPK)), 1.0, 0.0)
        qb = qt.astype(BF16)
        qb_scr[h] = qb
        s = jnp.dot(k_ref[h, i].astype(BF16), qb, preferred_element_type=F32) * scale
        s = jnp.where(kpos <= qpos, s, NEG_INF)
        m0 = jnp.max(s, axis=0, keepdims=True)
        p = jnp.exp(s - m0)
        m_scr[h][...] = m0
        l_scr[h][...] = jnp.sum(p, axis=0, keepdims=True)
        acc_scr[h][...] = jnp.dot(vt_ref[h, i].astype(BF16), p.astype(BF16), preferred_element_type=F32)

    def body(n, carry):
        scores = [jnp.dot(k_ref[h, n].astype(BF16), qb_scr[h], preferred_element_type=F32) for h in range(nh)]
        probs, alphas = [], []
        for h in range(nh):
            m = m_scr[h][...]
            s = jnp.where(sel_scr[h, pl.ds(n, 1), :] > 0.5, scores[h] * scale, NEG_INF)
            m_new = jnp.maximum(m, jnp.max(s, axis=0, keepdims=True))
            p = jnp.exp(s - m_new)
            alpha = jnp.exp(m - m_new)
            m_scr[h][...] = m_new
            l_scr[h][...] = alpha * l_scr[h][...] + jnp.sum(p, axis=0, keepdims=True)
            probs.append(p.astype(BF16))
            alphas.append(alpha)
        for h in range(nh):
            acc_scr[h][...] = alphas[h] * acc_scr[h][...] + jnp.dot(vt_ref[h, n].astype(BF16), probs[h],
                                                                    preferred_element_type=F32)
        return carry

    lax.fori_loop(0, i, body, 0)
    for h in range(nh):
        o_ref[h] = acc_scr[h][...] / l_scr[h][...]


def _moba_branch_pallas(p_c, bt, seq):
    nh, dh, bs = MOBA_HEADS, MOBA_HEAD_DIM, MOBA_BLOCK
    assert seq % bs == 0
    nblk = seq // bs
    qkv = p_c.reshape(bt, seq, 3, nh, dh)
    qt = qkv[:, :, 0].transpose(0, 2, 3, 1)
    k5 = qkv[:, :, 1].reshape(bt, nblk, bs, nh, dh).transpose(0, 3, 1, 2, 4)
    vt5 = qkv[:, :, 2].reshape(bt, nblk, bs, nh, dh).transpose(0, 3, 1, 4, 2)
    ot = pl.pallas_call(
        _moba_kernel,
        grid=(bt, nblk),
        in_specs=[pl.BlockSpec((None, nh, dh, bs), lambda b, i: (b, 0, 0, i)),
                  pl.BlockSpec((None, nh, nblk, bs, dh), lambda b, i: (b, 0, 0, 0, 0)),
                  pl.BlockSpec((None, nh, nblk, dh, bs), lambda b, i: (b, 0, 0, 0, 0))],
        out_specs=pl.BlockSpec((None, nh, dh, bs), lambda b, i: (b, 0, 0, i)),
        out_shape=jax.ShapeDtypeStruct((bt, nh, dh, seq), F32),
        scratch_shapes=[pltpu.VMEM((nh, nblk, dh), F32), pltpu.VMEM((nh, nblk, bs), F32),
                        pltpu.VMEM((nh, dh, bs), BF16)]
        + [pltpu.VMEM((1, bs), F32)] * (2 * nh) + [pltpu.VMEM((dh, bs), F32)] * nh,
        compiler_params=_cparams("parallel", "arbitrary"),
        name="moba_attn",
    )(qt, k5, vt5)
    return ot.transpose(0, 3, 1, 2).reshape(bt * seq, nh * dh)


def kernel(x, c, ada_w, ada_b, norm1_w, w_in, gdn_conv_w, gdn_a_log, gdn_dt_bias, gdn_norm_w, s5_a_re, s5_a_im, s5_b_re, s5_b_im, s5_c_re, s5_c_im, s5_d, s5_log_dt, s5_glu_w, s5_glu_b, w_branch_a, w_branch_b, w_branch_c, w_out, norm2_w, peer_wq, peer_k1, peer_k2, peer_u, peer_v, final_norm_w):
    bt, seq, d = x.shape
    n = bt * seq
    depth = ada_w.shape[0]
    x2d = x.reshape(n, d)
    mod = _ada_call(c, ada_w, ada_b)
    gate_perm = np.asarray(_SLOT_AT, np.int32)
    for l in range(depth):
        sh1, sc1, g1, sh2, sc2, g2 = (mod[l, :, j * d:(j + 1) * d].reshape(bt, 1, d) for j in range(6))
        p_a, p_ub, p_c, p_gate, p_ba = _in_proj_call(
            x2d, norm1_w[l].reshape(1, d), sc1, sh1, _arrange_w_in(w_in[l]), seq)
        o_a = _gdn_branch_pallas(p_a, p_ba, gdn_conv_w[l], gdn_a_log[l], gdn_dt_bias[l], gdn_norm_w[l], bt, seq)
        o_b = _s5_branch_pallas(p_ub, bt, seq, s5_a_re[l], s5_a_im[l], s5_b_re[l], s5_b_im[l], s5_c_re[l],
                                s5_c_im[l], s5_d[l], s5_log_dt[l], s5_glu_w[l], s5_glu_b[l])
        o_c = _moba_branch_pallas(p_c, bt, seq)
        x2d = _merge_call(o_a, o_b, o_c, p_gate, x2d, g1, w_branch_a[l].astype(BF16), w_branch_b[l].astype(BF16),
                          w_branch_c[l].astype(BF16), w_out[l].astype(BF16), seq)
        h2, idx_t, gate_t = _peer_query_call(x2d, norm2_w[l].reshape(1, d), sc2, sh2,
                                             peer_wq[l].T.astype(BF16), peer_k1[l], peer_k2[l], seq)
        uv = jnp.concatenate([peer_u[l].reshape(-1, SUBLANES, LANES), peer_v[l].reshape(-1, SUBLANES, LANES)], axis=1)
        x3 = _peer_gather_call(idx_t.T, gate_t.T[:, gate_perm], h2.reshape(n, SUBLANES, LANES),
                               x2d.reshape(n, SUBLANES, LANES), g2.reshape(bt, SUBLANES, LANES), uv, seq)
        x2d = x3.reshape(n, d)
    return _final_norm_call(x2d, final_norm_w.reshape(1, d)).reshape(bt, seq, d)
```

```python
import functools
import math

import jax
import jax.numpy as jnp
import numpy as np
from jax import lax
from jax.experimental import pallas as pl
from jax.experimental.pallas import tpu as pltpu

F32 = jnp.float32
BF16 = jnp.bfloat16
HIGHEST = lax.Precision.HIGHEST

D_MODEL = 1024
GDN_HEADS = 4
GDN_HEAD_DIM = 128
GDN_WIDTH = GDN_HEADS * GDN_HEAD_DIM
GDN_CONV = 4
GDN_CHUNK = 64
S5_GROUP = 16
S5_GROUPS = 16
S5_WIDTH = S5_GROUPS * S5_GROUP
S5_STATE = 64
MOBA_HEADS = 4
MOBA_HEAD_DIM = 64
MOBA_WIDTH = MOBA_HEADS * MOBA_HEAD_DIM
MOBA_BLOCK = 256
MOBA_TOPK = 3
MOBA_Q_CHUNK = 64
N_BRANCH = 3
PEER_HEADS = 8
PEER_NKEYS = 128
PEER_QDIM = 256
PEER_TOPK = 16
PEER_SEL = PEER_HEADS * PEER_TOPK
RMS_EPS = 1e-6
NEG_INF = -1e30

SUBLANES = 8
LANES = 128
VMEM_LIMIT_BYTES = 56 * 1024 * 1024

_OFF_QKV_A = 0
_OFF_Z_A = 3 * GDN_WIDTH
_OFF_BETA = _OFF_Z_A + GDN_WIDTH
_OFF_ALPHA = _OFF_BETA + GDN_HEADS
_OFF_UB = _OFF_ALPHA + GDN_HEADS
_OFF_QKV_C = _OFF_UB + S5_WIDTH
_OFF_GATE = _OFF_QKV_C + 3 * MOBA_WIDTH
_IN_COLS = _OFF_GATE + N_BRANCH * D_MODEL


def _cparams(*sem):
    return pltpu.CompilerParams(dimension_semantics=sem, vmem_limit_bytes=VMEM_LIMIT_BYTES)


def _resident(shape):
    nd = len(shape)
    return pl.BlockSpec(shape, lambda *_: (0,) * nd)


def _ada_kernel(c_ref, w_ref, b_ref, o_ref):
    c = c_ref[...]
    sc = c * jax.nn.sigmoid(c)
    o_ref[...] = jnp.dot(sc, w_ref[...], precision=HIGHEST, preferred_element_type=F32) + b_ref[...]


def _ada_call(c, ada_w, ada_b):
    depth, d, d6 = ada_w.shape
    bt = c.shape[0]
    nj = d6 // d
    return pl.pallas_call(
        _ada_kernel,
        grid=(depth, nj),
        in_specs=[
            pl.BlockSpec((bt, d), lambda l, j: (0, 0)),
            pl.BlockSpec((None, d, d), lambda l, j: (l, 0, j)),
            pl.BlockSpec((None, 1, d), lambda l, j: (l, 0, j)),
        ],
        out_specs=pl.BlockSpec((None, bt, d), lambda l, j: (l, 0, j)),
        out_shape=jax.ShapeDtypeStruct((depth, bt, d6), F32),
        compiler_params=_cparams("parallel", "parallel"),
        name="ada_mod",
    )(c, ada_w, ada_b.reshape(depth, 1, d6))


def _norm_mod(x, nw, sc, sh):
    y = x * lax.rsqrt(jnp.mean(x * x, axis=-1, keepdims=True) + RMS_EPS)
    return (y * nw) * (1.0 + sc) + sh


_IN_SPLITS = (4 * GDN_WIDTH, S5_WIDTH, 3 * MOBA_WIDTH, N_BRANCH * D_MODEL, LANES)
_IN_DTYPES = (F32, F32, F32, BF16, F32)


def _in_proj_kernel(x_ref, nw_ref, sc_ref, sh_ref, w_ref, oa_ref, ob_ref, oc_ref, og_ref, oba_ref):
    h = _norm_mod(x_ref[...], nw_ref[...], sc_ref[...], sh_ref[...]).astype(BF16)
    off = 0
    for o_ref, width in zip((oa_ref, ob_ref, oc_ref, og_ref, oba_ref), _IN_SPLITS):
        o_ref[...] = jnp.dot(h, w_ref[:, off:off + width], preferred_element_type=F32).astype(o_ref.dtype)
        off += width


def _in_proj_call(x2d, nw, sc, sh, w_r, seq, tm=256):
    n, d = x2d.shape
    per_b = seq // tm
    wcols = w_r.shape[1]
    outs = [jax.ShapeDtypeStruct((n, wd), dt) for wd, dt in zip(_IN_SPLITS, _IN_DTYPES)]
    return pl.pallas_call(
        _in_proj_kernel,
        grid=(n // tm,),
        in_specs=[
            pl.BlockSpec((tm, d), lambda i: (i, 0)),
            _resident((1, d)),
            pl.BlockSpec((None, 1, d), lambda i: (i // per_b, 0, 0)),
            pl.BlockSpec((None, 1, d), lambda i: (i // per_b, 0, 0)),
            _resident((d, wcols)),
        ],
        out_specs=[pl.BlockSpec((tm, wd), lambda i: (i, 0)) for wd in _IN_SPLITS],
        out_shape=outs,
        compiler_params=_cparams("parallel"),
        name="in_proj",
    )(x2d, nw, sc, sh, w_r)


def _arrange_w_in(w_in_l):
    pad = jnp.zeros((w_in_l.shape[0], LANES - 2 * GDN_HEADS), w_in_l.dtype)
    return jnp.concatenate([
        w_in_l[:, _OFF_QKV_A:_OFF_BETA],
        w_in_l[:, _OFF_UB:_OFF_QKV_C],
        w_in_l[:, _OFF_QKV_C:_OFF_GATE],
        w_in_l[:, _OFF_GATE:_IN_COLS],
        w_in_l[:, _OFF_BETA:_OFF_UB], pad,
    ], axis=1).astype(BF16)


def _merge_kernel(oa_ref, ob_ref, oc_ref, gt_ref, x_ref, g1_ref, wa_ref, wb_ref, wc_ref, wo_ref, o_ref):
    d = D_MODEL
    ya = jnp.dot(oa_ref[...].astype(BF16), wa_ref[...], preferred_element_type=F32)
    yb = jnp.dot(ob_ref[...].astype(BF16), wb_ref[...], preferred_element_type=F32)
    yc = jnp.dot(oc_ref[...].astype(BF16), wc_ref[...], preferred_element_type=F32)
    gate = lambda j: jax.nn.sigmoid(gt_ref[:, j * d:(j + 1) * d].astype(F32))
    merged = gate(0) * ya + gate(1) * yb + gate(2) * yc
    y = jnp.dot(merged.astype(BF16), wo_ref[...], preferred_element_type=F32)
    o_ref[...] = x_ref[...] + g1_ref[...] * y


def _merge_call(o_a, o_b, o_c, p_gate, x2d, g1, wa, wb, wc, wo, seq, tm=512):
    n, d = x2d.shape
    per_b = seq // tm
    row = lambda wd: pl.BlockSpec((tm, wd), lambda i: (i, 0))
    return pl.pallas_call(
        _merge_kernel,
        grid=(n // tm,),
        in_specs=[row(GDN_WIDTH), row(S5_WIDTH), row(MOBA_WIDTH), row(N_BRANCH * d), row(d),
                  pl.BlockSpec((None, 1, d), lambda i: (i // per_b, 0, 0)),
                  _resident(wa.shape), _resident(wb.shape), _resident(wc.shape), _resident(wo.shape)],
        out_specs=row(d),
        out_shape=jax.ShapeDtypeStruct((n, d), F32),
        compiler_params=_cparams("parallel"),
        name="merge_out",
    )(o_a, o_b, o_c, p_gate, x2d, g1, wa, wb, wc, wo)


def _candidate_tables():
    k = PEER_TOPK
    pairs = [(a, b) for a in range(k) for b in range(k) if (a + 1) * (b + 1) <= k]
    rows = -(-len(pairs) // SUBLANES) * SUBLANES
    sel = np.zeros((rows, 2 * k), np.float32)
    sel_id = np.zeros((rows, 2 * k), np.float32)
    bias = np.zeros((rows, 1), np.float32)
    order = np.zeros((rows, 1), np.float32)
    for r, (a, b) in enumerate(pairs):
        sel[r, a] = sel[r, k + b] = 1.0
        sel_id[r, a] = float(PEER_NKEYS)
        sel_id[r, k + b] = 1.0
        order[r, 0] = a * k + b
    for r in range(len(pairs), rows):
        bias[r, 0] = -np.inf
        order[r, 0] = k * k + r
    return sel, sel_id, bias, order


_CAND_SEL, _CAND_SEL_ID, _CAND_BIAS, _CAND_ORDER = _candidate_tables()
_CAND_ROWS = _CAND_SEL.shape[0]


def _extract_max(s, order, big, payload=None):
    m = jnp.max(s, axis=0, keepdims=True)
    pos = jnp.min(jnp.where(s == m, order, big), axis=0, keepdims=True)
    hit = order == pos
    tag = pos if payload is None else jnp.sum(jnp.where(hit, payload, 0.0), axis=0, keepdims=True)
    return m, tag, jnp.where(hit, -jnp.inf, s)


def _peer_query_kernel(x_ref, nw_ref, sc_ref, sh_ref, wqt_ref, k1_ref, k2_ref, sel_ref, selid_ref, cb_ref, co_ref,
                       h_ref, idx_ref, gate_ref, q_scr, s1_scr, s2_scr, v_scr, i_scr, cs_scr, ci_scr, ts_scr, te_scr):
    h = _norm_mod(x_ref[...], nw_ref[...], sc_ref[...], sh_ref[...])
    h_ref[...] = h
    q_scr[...] = lax.dot_general(wqt_ref[...], h.astype(BF16), (((1,), (1,)), ((), ())),
                                 preferred_element_type=F32)
    half = PEER_QDIM // 2
    k = PEER_TOPK
    key_iota = lax.broadcasted_iota(jnp.int32, s1_scr.shape, 0).astype(F32)
    cand_order = jnp.broadcast_to(co_ref[...], cs_scr.shape)

    for hd in range(PEER_HEADS + 1):
        do_keys, do_cand = hd < PEER_HEADS, hd > 0
        if do_keys:
            base = hd * PEER_QDIM
            s1_scr[...] = jnp.dot(k1_ref[hd], q_scr[base:base + half, :], precision=HIGHEST,
                                  preferred_element_type=F32)
            s2_scr[...] = jnp.dot(k2_ref[hd], q_scr[base + half:base + 2 * half, :], precision=HIGHEST,
                                  preferred_element_type=F32)

        def trip(j, c, do_keys=do_keys, do_cand=do_cand):
            if do_keys:
                m1, p1, s1 = _extract_max(s1_scr[...], key_iota, float(PEER_NKEYS))
                m2, p2, s2 = _extract_max(s2_scr[...], key_iota, float(PEER_NKEYS))
                s1_scr[...] = s1
                s2_scr[...] = s2
                v_scr[pl.ds(j, 1), :] = m1
                i_scr[pl.ds(j, 1), :] = p1
                v_scr[pl.ds(j + k, 1), :] = m2
                i_scr[pl.ds(j + k, 1), :] = p2
            if do_cand:
                m, e, s = _extract_max(cs_scr[...], cand_order, float(4 * k * k), payload=ci_scr[...])
                cs_scr[...] = s
                ts_scr[pl.ds(j, 1), :] = m
                te_scr[pl.ds(j, 1), :] = e
            return c

        lax.fori_loop(0, k, trip, 0)
        if do_cand:
            ts = ts_scr[...]
            e = jnp.exp(ts - jnp.max(ts, axis=0, keepdims=True))
            out = (hd - 1) * k
            gate_ref[out:out + k, :] = e / jnp.sum(e, axis=0, keepdims=True)
            idx_ref[out:out + k, :] = te_scr[...].astype(jnp.int32)
        if do_keys:
            cs_scr[...] = (jnp.dot(sel_ref[...], v_scr[...], precision=HIGHEST, preferred_element_type=F32)
                           + cb_ref[...])
            ci_scr[...] = jnp.dot(selid_ref[...], i_scr[...].astype(BF16), preferred_element_type=F32)


def _peer_query_call(x2d, nw, sc, sh, wqt, k1, k2, seq, tm=512):
    n, d = x2d.shape
    per_b = seq // tm
    nq = PEER_HEADS * PEER_QDIM
    k = PEER_TOPK
    scratch = [pltpu.VMEM((nq, tm), F32), pltpu.VMEM((PEER_NKEYS, tm), F32), pltpu.VMEM((PEER_NKEYS, tm), F32),
               pltpu.VMEM((2 * k, tm), F32), pltpu.VMEM((2 * k, tm), F32),
               pltpu.VMEM((_CAND_ROWS, tm), F32), pltpu.VMEM((_CAND_ROWS, tm), F32),
               pltpu.VMEM((k, tm), F32), pltpu.VMEM((k, tm), F32)]
    cand = [jnp.asarray(_CAND_SEL), jnp.asarray(_CAND_SEL_ID, dtype=BF16), jnp.asarray(_CAND_BIAS),
            jnp.asarray(_CAND_ORDER)]
    return pl.pallas_call(
        _peer_query_kernel,
        grid=(n // tm,),
        in_specs=[
            pl.BlockSpec((tm, d), lambda i: (i, 0)),
            _resident((1, d)),
            pl.BlockSpec((None, 1, d), lambda i: (i // per_b, 0, 0)),
            pl.BlockSpec((None, 1, d), lambda i: (i // per_b, 0, 0)),
            _resident(wqt.shape), _resident(k1.shape), _resident(k2.shape),
        ] + [_resident(t.shape) for t in cand],
        out_specs=[pl.BlockSpec((tm, d), lambda i: (i, 0)),
                   pl.BlockSpec((PEER_SEL, tm), lambda i: (0, i)),
                   pl.BlockSpec((PEER_SEL, tm), lambda i: (0, i))],
        out_shape=[jax.ShapeDtypeStruct((n, d), F32),
                   jax.ShapeDtypeStruct((PEER_SEL, n), jnp.int32),
                   jax.ShapeDtypeStruct((PEER_SEL, n), F32)],
        scratch_shapes=scratch,
        compiler_params=_cparams("parallel"),
        name="peer_query",
    )(x2d, nw, sc, sh, wqt, k1, k2, *cand)


PEER_TOKENS_PER_STEP = 8
PEER_DMA_THREADS = 2
_ROW_SUB = 2 * SUBLANES


def _fold_pair(a, b, shift, keep_a):
    fa = a + pltpu.roll(a, shift, axis=0)
    fb = b + pltpu.roll(b, SUBLANES - shift, axis=0)
    return jnp.where(keep_a, fa, fb)


def _slot_order():
    tiles = [[k] * SUBLANES for k in range(PEER_SEL)]
    for shift, keep in ((4, [j < 4 for j in range(8)]), (2, [(j % 4) >= 2 for j in range(8)]),
                        (1, [(j % 2) == 1 for j in range(8)])):
        tiles = [[tiles[2 * m][j] if keep[j] else tiles[2 * m + 1][j] for j in range(8)]
                 for m in range(len(tiles) // 2)]
    return [s for t in tiles for s in t]


_SLOT_AT = _slot_order()
_POS_OF = [0] * PEER_SEL
for _p, _s in enumerate(_SLOT_AT):
    _POS_OF[_s] = _p


def _peer_gather_kernel(idx0_ref, idxc_ref, idxn_ref, g_ref, h_ref, x_ref, g2_ref, uv_hbm, o_ref,
                        buf0, buf1, wb_scr, sems):
    tb = PEER_TOKENS_PER_STEP
    blk = tb * PEER_SEL
    i = pl.program_id(0)
    bufs = (buf0, buf1)

    def row_copy(e, half, row, t):
        return pltpu.make_async_copy(uv_hbm.at[e], bufs[half].at[row], sems.at[half * tb + t])

    def slot_copy(half, t):
        return pltpu.make_async_copy(uv_hbm.at[pl.ds(0, PEER_SEL)], bufs[half].at[pl.ds(t * PEER_SEL, PEER_SEL)],
                                     sems.at[half * tb + t])

    @pl.when(i == 0)
    def _():
        def issue0(t, c):
            for k in range(PEER_SEL):
                row_copy(idx0_ref[0, t * PEER_SEL + k], 0, t * PEER_SEL + k, t).start()
            return c
        lax.fori_loop(0, tb, issue0, 0)

    sub = lax.broadcasted_iota(jnp.int32, (SUBLANES, LANES), 0)
    keep4 = sub < 4
    keep2 = (sub % 4) >= 2
    keep1 = (sub % 2) == 1
    eye = (lax.broadcasted_iota(jnp.int32, (PEER_SEL, LANES), 0)
           == lax.broadcasted_iota(jnp.int32, (PEER_SEL, LANES), 1))

    for half in range(2):
        nidx_ref, noff = (idxc_ref, blk) if half == 0 else (idxn_ref, 0)
        buf = bufs[half]
        for t in range(tb):
            tt = half * tb + t
            slot_copy(half, t).wait()
            for k in range(PEER_SEL):
                row_copy(nidx_ref[0, noff + t * PEER_SEL + k], 1 - half, t * PEER_SEL + k, t).start(
                    priority=k % PEER_DMA_THREADS)
            base = t * PEER_SEL
            ht = h_ref[tt]
            tiles = [buf[base + k, 0:SUBLANES, :] * ht for k in range(PEER_SEL)]
            tiles = [_fold_pair(tiles[2 * m], tiles[2 * m + 1], 4, keep4) for m in range(PEER_SEL // 2)]
            tiles = [_fold_pair(tiles[2 * m], tiles[2 * m + 1], 2, keep2) for m in range(PEER_SEL // 4)]
            tiles = [_fold_pair(tiles[2 * m], tiles[2 * m + 1], 1, keep1) for m in range(PEER_SEL // 8)]
            part = jnp.concatenate(tiles, axis=0)
            s = jnp.sum(part, axis=-1, keepdims=True)
            act = 0.5 * s * (1.0 + lax.erf(s * (2.0 ** -0.5)))
            grow = jnp.broadcast_to(g_ref[tt:tt + 1, :], (PEER_SEL, LANES))
            gcol = jnp.sum(jnp.where(eye, grow, 0.0), axis=-1, keepdims=True)
            wbase = tt * PEER_SEL
            wb_scr[wbase:wbase + PEER_SEL, :] = jnp.broadcast_to(gcol * act, (PEER_SEL, LANES))
            accs = [jnp.zeros((SUBLANES, LANES), F32) for _ in range(4)]
            for k in range(PEER_SEL):
                p = wbase + _POS_OF[k]
                wk = jnp.broadcast_to(wb_scr[p:p + 1, :], (SUBLANES, LANES))
                accs[k % 4] = accs[k % 4] + wk * buf[base + k, SUBLANES:_ROW_SUB, :]
            y = (accs[0] + accs[1]) + (accs[2] + accs[3])
            o_ref[tt] = x_ref[tt] + g2_ref[...] * y

    @pl.when(i == pl.num_programs(0) - 1)
    def _():
        for t in range(tb):
            slot_copy(0, t).wait()


def _peer_gather_call(idx, gate_p, h3, x3, g2_3, uv, seq):
    n = idx.shape[0]
    tb = PEER_TOKENS_PER_STEP
    ns = n // (2 * tb)
    per_b = seq // (2 * tb)
    idx3 = idx.reshape(ns, 1, 2 * tb * PEER_SEL)
    smem_blk = lambda fn: pl.BlockSpec((None, 1, 2 * tb * PEER_SEL), fn, memory_space=pltpu.SMEM)
    tok3 = pl.BlockSpec((2 * tb, SUBLANES, LANES), lambda i: (i, 0, 0))
    return pl.pallas_call(
        _peer_gather_kernel,
        grid=(ns,),
        in_specs=[
            smem_blk(lambda i: (0, 0, 0)),
            smem_blk(lambda i: (i, 0, 0)),
            smem_blk(lambda i: (jnp.minimum(i + 1, ns - 1), 0, 0)),
            pl.BlockSpec((2 * tb, PEER_SEL), lambda i: (i, 0)),
            tok3, tok3,
            pl.BlockSpec((None, SUBLANES, LANES), lambda i: (i // per_b, 0, 0)),
            pl.BlockSpec(memory_space=pl.ANY),
        ],
        out_specs=tok3,
        out_shape=jax.ShapeDtypeStruct((n, SUBLANES, LANES), F32),
        scratch_shapes=[pltpu.VMEM((tb * PEER_SEL, _ROW_SUB, LANES), F32),
                        pltpu.VMEM((tb * PEER_SEL, _ROW_SUB, LANES), F32),
                        pltpu.VMEM((2 * tb * PEER_SEL, LANES), F32),
                        pltpu.SemaphoreType.DMA((2 * tb,))],
        compiler_params=_cparams("arbitrary"),
        name="peer_gather",
    )(idx3, idx3, idx3, gate_p, h3, x3, g2_3, uv)


def _final_norm_kernel(x_ref, w_ref, o_ref):
    x = x_ref[...]
    o_ref[...] = x * lax.rsqrt(jnp.mean(x * x, axis=-1, keepdims=True) + RMS_EPS) * w_ref[...]


def _final_norm_call(x2d, w, tm=1024):
    n, d = x2d.shape
    return pl.pallas_call(
        _final_norm_kernel,
        grid=(n // tm,),
        in_specs=[pl.BlockSpec((tm, d), lambda i: (i, 0)), _resident((1, d))],
        out_specs=pl.BlockSpec((tm, d), lambda i: (i, 0)),
        out_shape=jax.ShapeDtypeStruct((n, d), F32),
        compiler_params=_cparams("parallel"),
        name="final_norm",
    )(x2d, w)


S5_LANES = S5_GROUPS * S5_STATE


def _s5_disc_kernel(are_ref, aim_ref, ldt_ref, bre_ref, bim_ref, ar_ref, ai_ref, br_ref, bi_ref):
    a_re = are_ref[...]
    a_im = aim_ref[...]
    dt = jnp.exp(ldt_ref[...])
    mag = jnp.exp(a_re * dt)
    ar = mag * jnp.cos(a_im * dt)
    ai = mag * jnp.sin(a_im * dt)
    den = a_re * a_re + a_im * a_im
    cr = ((ar - 1.0) * a_re + ai * a_im) / den
    ci = (ai * a_re - (ar - 1.0) * a_im) / den
    ar_ref[...] = ar
    ai_ref[...] = ai
    br_ref[...] = cr * bre_ref[...] - ci * bim_ref[...]
    bi_ref[...] = cr * bim_ref[...] + ci * bre_ref[...]


def _s5_disc_call(a_re, a_im, log_dt, b_re, b_im):
    rows = S5_LANES
    col = lambda t: t.reshape(rows, 1)
    ldt = jnp.broadcast_to(log_dt[:, None], (S5_GROUPS, S5_STATE))
    out = [jax.ShapeDtypeStruct((rows, 1), F32)] * 2 + [jax.ShapeDtypeStruct((rows, S5_GROUP), F32)] * 2
    return pl.pallas_call(_s5_disc_kernel, out_shape=out, name="s5_disc")(
        col(a_re), col(a_im), col(ldt), b_re.reshape(rows, S5_GROUP), b_im.reshape(rows, S5_GROUP))


def _s5_kernel(u_ref, bcat_ref, ar_ref, ai_ref, ccat_ref, d_ref, gw_ref, gb_ref, o_ref, st_scr, bu_scr, *, steps, bt):
    @pl.when(pl.program_id(0) == 0)
    def _():
        st_scr[...] = jnp.zeros_like(st_scr)

    u = u_ref[...]
    bu_scr[...] = jnp.dot(u.astype(BF16), bcat_ref[...], preferred_element_type=F32)
    a_re = jnp.broadcast_to(ar_ref[...], (bt, S5_LANES))
    a_im = jnp.broadcast_to(ai_ref[...], (bt, S5_LANES))

    def step(t, carry):
        s_re, s_im = carry
        r = pl.multiple_of(t * bt, bt)
        n_re = a_re * s_re - a_im * s_im + bu_scr[pl.ds(r, bt), 0:S5_LANES]
        n_im = a_re * s_im + a_im * s_re + bu_scr[pl.ds(r, bt), S5_LANES:2 * S5_LANES]
        bu_scr[pl.ds(r, bt), 0:S5_LANES] = n_re
        bu_scr[pl.ds(r, bt), S5_LANES:2 * S5_LANES] = n_im
        return n_re, n_im

    s_re, s_im = lax.fori_loop(0, steps, step, (st_scr[:, 0:S5_LANES], st_scr[:, S5_LANES:2 * S5_LANES]))
    st_scr[:, 0:S5_LANES] = s_re
    st_scr[:, S5_LANES:2 * S5_LANES] = s_im
    y = jnp.dot(bu_scr[...].astype(BF16), ccat_ref[...], preferred_element_type=F32) + d_ref[...] * u
    zg = 0.5 * y * (1.0 + lax.erf(y * (2.0 ** -0.5)))
    zz = jnp.dot(zg.astype(BF16), gw_ref[...], preferred_element_type=F32) + gb_ref[...]
    o_ref[...] = zz[:, 0:S5_WIDTH] * jax.nn.sigmoid(zz[:, S5_WIDTH:2 * S5_WIDTH])


def _s5_call(u_tb, bcat, abar_re, abar_im, ccat, d, glu_w, glu_b, bt, steps=64):
    rows = u_tb.shape[0]
    blk = steps * bt
    return pl.pallas_call(
        functools.partial(_s5_kernel, steps=steps, bt=bt),
        grid=(rows // blk,),
        in_specs=[pl.BlockSpec((blk, S5_WIDTH), lambda i: (i, 0)),
                  _resident(bcat.shape), _resident(abar_re.shape), _resident(abar_im.shape), _resident(ccat.shape),
                  _resident(d.shape), _resident(glu_w.shape), _resident(glu_b.shape)],
        out_specs=pl.BlockSpec((blk, S5_WIDTH), lambda i: (i, 0)),
        out_shape=jax.ShapeDtypeStruct((rows, S5_WIDTH), F32),
        scratch_shapes=[pltpu.VMEM((bt, 2 * S5_LANES), F32), pltpu.VMEM((blk, 2 * S5_LANES), F32)],
        compiler_params=_cparams("arbitrary"),
        name="s5_scan",
    )(u_tb, bcat, abar_re, abar_im, ccat, d, glu_w, glu_b)


def _s5_branch_pallas(p_ub, bt, seq, a_re, a_im, b_re, b_im, c_re, c_im, d, log_dt, glu_w, glu_b):
    ar, ai, br, bi = _s5_disc_call(a_re, a_im, log_dt, b_re, b_im)
    g_ix = jnp.arange(S5_GROUPS)
    def blockdiag_in(b):
        b = b.reshape(S5_GROUPS, S5_STATE, S5_GROUP)
        full = jnp.zeros((S5_GROUPS, S5_GROUP, S5_GROUPS, S5_STATE), F32)
        return full.at[g_ix, :, g_ix, :].set(b.transpose(0, 2, 1)).reshape(S5_WIDTH, S5_LANES)

    def blockdiag_out(c):
        full = jnp.zeros((S5_GROUPS, S5_STATE, S5_GROUPS, S5_GROUP), F32)
        return full.at[g_ix, :, g_ix, :].set(c.transpose(0, 2, 1)).reshape(S5_LANES, S5_WIDTH)

    bcat = jnp.concatenate([blockdiag_in(br), blockdiag_in(bi)], axis=1).astype(BF16)
    ccat = jnp.concatenate([blockdiag_out(c_re), -blockdiag_out(c_im)], axis=0).astype(BF16)
    u_tb = p_ub.reshape(bt, seq, S5_WIDTH).transpose(1, 0, 2).reshape(seq * bt, S5_WIDTH)
    o_tb = _s5_call(u_tb, bcat, ar.reshape(1, S5_LANES), ai.reshape(1, S5_LANES), ccat, d.reshape(1, S5_WIDTH),
                    glu_w.astype(BF16), glu_b.reshape(1, 2 * S5_WIDTH), bt)
    return o_tb.reshape(seq, bt, S5_WIDTH).transpose(1, 0, 2).reshape(bt * seq, S5_WIDTH)


def _mm(a, b):
    return jnp.dot(a.astype(BF16), b.astype(BF16), preferred_element_type=F32)


def _mm_nt(a, b, precision=None):
    if precision is None:
        a, b = a.astype(BF16), b.astype(BF16)
    return lax.dot_general(a, b, (((1,), (1,)), ((), ())), precision=precision, preferred_element_type=F32)


def _mm_tn(a, b):
    return lax.dot_general(a.astype(BF16), b.astype(BF16), (((0,), (0,)), ((), ())), preferred_element_type=F32)


def _softplus(x):
    return jnp.maximum(x, 0.0) + jnp.log1p(jnp.exp(-jnp.abs(x)))


def _gdn_kernel(pa_ref, ba_ref, cw_ref, alog_ref, dtb_ref, nw_ref, o_ref,
                st_scr, carry_scr, q_scr, k_scr, v_scr, g_scr, b_scr, u_scr, w_scr, qk_scr, gi_scr):
    j = pl.program_id(1)
    t_blk = pa_ref.shape[0]
    c_len, wid, dh = GDN_CHUNK, GDN_WIDTH, GDN_HEAD_DIM

    @pl.when(j == 0)
    def _():
        st_scr[...] = jnp.zeros_like(st_scr)
        carry_scr[...] = jnp.zeros_like(carry_scr)

    x = pa_ref[:, 0:3 * wid]
    xx = jnp.concatenate([carry_scr[...], x], axis=0)
    conv = cw_ref[GDN_CONV - 1:GDN_CONV, :] * x
    for k in range(1, GDN_CONV):
        conv = conv + cw_ref[GDN_CONV - 1 - k:GDN_CONV - k, :] * pltpu.roll(xx, k, axis=0)[SUBLANES:, :]
    carry_scr[...] = x[t_blk - SUBLANES:t_blk, :]
    act = conv * jax.nn.sigmoid(conv)
    for h in range(GDN_HEADS):
        qh = act[:, h * dh:(h + 1) * dh]
        kh = act[:, wid + h * dh:wid + (h + 1) * dh]
        q_scr[:, h * dh:(h + 1) * dh] = qh * lax.rsqrt(jnp.sum(qh * qh, axis=-1, keepdims=True) + 1e-6) * (dh ** -0.5)
        k_scr[:, h * dh:(h + 1) * dh] = kh * lax.rsqrt(jnp.sum(kh * kh, axis=-1, keepdims=True) + 1e-6)
    v_scr[...] = act[:, 2 * wid:3 * wid]
    ba = ba_ref[...]
    g_scr[...] = -jnp.exp(alog_ref[...]) * _softplus(ba + dtb_ref[...])
    b_scr[...] = jax.nn.sigmoid(ba)

    row = lax.broadcasted_iota(jnp.int32, (t_blk, t_blk), 0)
    col = lax.broadcasted_iota(jnp.int32, (t_blk, t_blk), 1)
    same = (row // c_len) == (col // c_len)
    tril = same & (row >= col)
    strict = same & (row > col)
    eye = jnp.where(row == col, 1.0, 0.0)
    lane = lax.broadcasted_iota(jnp.int32, (t_blk, LANES), 1)
    nw = nw_ref[...]
    gcum = jnp.dot(jnp.where(tril, 1.0, 0.0), g_scr[...], precision=HIGHEST, preferred_element_type=F32)
    beta_all = b_scr[...]
    heads = range(GDN_HEADS)
    hsl = [slice(h * dh, (h + 1) * dh) for h in heads]
    gis = [gcum[:, GDN_HEADS + h:GDN_HEADS + h + 1] for h in heads]
    betas = [beta_all[:, h:h + 1] for h in heads]
    decays, ps, ts = [], [], []
    for h in heads:
        g1 = jnp.where(lane == 0, gis[h], jnp.where(lane == 1, 1.0, 0.0))
        g2 = jnp.where(lane == 0, 1.0, jnp.where(lane == 1, -gis[h], 0.0))
        diff = _mm_nt(g1, g2, precision=HIGHEST)
        decays.append(jnp.where(tril, jnp.exp(jnp.where(tril, diff, 0.0)), 0.0))
    for h in heads:
        k_h = k_scr[:, hsl[h]]
        a = jnp.where(strict, _mm_nt(k_h * betas[h], k_h) * decays[h], 0.0)
        ps.append(-a)
        ts.append(eye - a)
    for _ in range(int(math.log2(c_len)) - 1):
        ps = [_mm(p, p) for p in ps]
        ts = [t + _mm(t, p) for t, p in zip(ts, ps)]
    for h in heads:
        k_h, v_h, q_h = k_scr[:, hsl[h]], v_scr[:, hsl[h]], q_scr[:, hsl[h]]
        e_g = jnp.exp(gis[h])
        u_scr[:, hsl[h]] = _mm(ts[h], v_h * betas[h])
        w_scr[:, hsl[h]] = _mm(ts[h], k_h * betas[h] * e_g)
        qk_scr[h] = jnp.where(tril, _mm_nt(q_h, k_h) * decays[h], 0.0)
        q_scr[:, hsl[h]] = q_h * e_g
        gi_scr[:, h:h + 1] = gis[h]

    for c in range(t_blk // c_len):
        rows = slice(c * c_len, (c + 1) * c_len)
        states = [st_scr[h] for h in heads]
        v_news = [u_scr[rows, hsl[h]] - _mm(w_scr[rows, hsl[h]], states[h]) for h in heads]
        o_state = [_mm(q_scr[rows, hsl[h]], states[h]) for h in heads]
        outs = [o_state[h] + _mm(qk_scr[h, rows, rows], v_news[h]) for h in heads]
        for h in heads:
            gi = gi_scr[rows, h:h + 1]
            g_last = gi[c_len - 1:c_len, :]
            st_scr[h] = states[h] * jnp.exp(g_last) + _mm_tn(k_scr[rows, hsl[h]] * jnp.exp(g_last - gi), v_news[h])
        for h in heads:
            o = outs[h]
            z = pa_ref[rows, 3 * wid + h * dh:3 * wid + (h + 1) * dh]
            o_ref[rows, hsl[h]] = (o * lax.rsqrt(jnp.mean(o * o, axis=-1, keepdims=True) + RMS_EPS) * nw
                                   * (z * jax.nn.sigmoid(z)))


def _gdn_branch_pallas(p_a, p_ba, conv_w, a_log, dt_bias, norm_w, bt, seq, t_blk=256):
    n = p_a.shape[0]
    nj = seq // t_blk
    wid = GDN_WIDTH
    lane_pad = lambda v: jnp.zeros((1, LANES), F32).at[0, GDN_HEADS:2 * GDN_HEADS].set(v)
    rows = lambda wd: pl.BlockSpec((t_blk, wd), lambda b, j: (b * nj + j, 0))
    return pl.pallas_call(
        _gdn_kernel,
        grid=(bt, nj),
        in_specs=[rows(4 * wid), rows(LANES), _resident((GDN_CONV, 3 * wid)), _resident((1, LANES)),
                  _resident((1, LANES)), _resident((1, GDN_HEAD_DIM))],
        out_specs=rows(wid),
        out_shape=jax.ShapeDtypeStruct((n, wid), F32),
        scratch_shapes=[pltpu.VMEM((GDN_HEADS, GDN_HEAD_DIM, GDN_HEAD_DIM), F32), pltpu.VMEM((SUBLANES, 3 * wid), F32),
                        pltpu.VMEM((t_blk, wid), F32), pltpu.VMEM((t_blk, wid), F32), pltpu.VMEM((t_blk, wid), F32),
                        pltpu.VMEM((t_blk, LANES), F32), pltpu.VMEM((t_blk, LANES), F32),
                        pltpu.VMEM((t_blk, wid), F32), pltpu.VMEM((t_blk, wid), F32),
                        pltpu.VMEM((GDN_HEADS, t_blk, t_blk), F32), pltpu.VMEM((t_blk, LANES), F32)],
        compiler_params=_cparams("parallel", "arbitrary"),
        name="gdn_delta",
    )(p_a, p_ba, conv_w, lane_pad(a_log), lane_pad(dt_bias), norm_w.reshape(1, GDN_HEAD_DIM))


def _moba_kernel(qt_ref, k_ref, vt_ref, o_ref, km_scr, sel_scr, qb_scr, *head_scr):
    i = pl.program_id(1)
    nh, nblk = k_ref.shape[0], k_ref.shape[1]
    scale = MOBA_HEAD_DIM ** -0.5
    m_scr, l_scr, acc_scr = head_scr[0:nh], head_scr[nh:2 * nh], head_scr[2 * nh:3 * nh]

    @pl.when(i == 0)
    def _():
        for h in range(nh):
            for n in range(nblk):
                km_scr[h, n:n + 1, :] = jnp.mean(k_ref[h, n], axis=0, keepdims=True)

    blk = lax.broadcasted_iota(jnp.int32, (nblk, MOBA_BLOCK), 0)
    kpos = lax.broadcasted_iota(jnp.int32, (MOBA_BLOCK, MOBA_BLOCK), 0)
    qpos = lax.broadcasted_iota(jnp.int32, (MOBA_BLOCK, MOBA_BLOCK), 1)
    for h in range(nh):
        qt = qt_ref[h]
        gate = jnp.dot(km_scr[h], qt, precision=HIGHEST, preferred_element_type=F32)
        cnt = jnp.zeros(gate.shape, F32)
        for m in range(nblk):
            gm = gate[m:m + 1, :]
            beats = (gm > gate) | ((gm == gate) & (m < blk))
            cnt = cnt + jnp.where(beats & (m < i), 1.0, 0.0)
        sel_scr[h] = jnp.where((blk < i) & (cnt < float(MOBA_TOPK)), 1.0, 0.0)
        qb = qt.astype(BF16)
        qb_scr[h] = qb
        s = jnp.dot(k_ref[h, i].astype(BF16), qb, preferred_element_type=F32) * scale
        s = jnp.where(kpos <= qpos, s, NEG_INF)
        m0 = jnp.max(s, axis=0, keepdims=True)
        p = jnp.exp(s - m0)
        m_scr[h][...] = m0
        l_scr[h][...] = jnp.sum(p, axis=0, keepdims=True)
        acc_scr[h][...] = jnp.dot(vt_ref[h, i].astype(BF16), p.astype(BF16), preferred_element_type=F32)

    def body(n, carry):
        scores = [jnp.dot(k_ref[h, n].astype(BF16), qb_scr[h], preferred_element_type=F32) for h in range(nh)]
        probs, alphas = [], []
        for h in range(nh):
            m = m_scr[h][...]
            s = jnp.where(sel_scr[h, pl.ds(n, 1), :] > 0.5, scores[h] * scale, NEG_INF)
            m_new = jnp.maximum(m, jnp.max(s, axis=0, keepdims=True))
            p = jnp.exp(s - m_new)
            alpha = jnp.exp(m - m_new)
            m_scr[h][...] = m_new
            l_scr[h][...] = alpha * l_scr[h][...] + jnp.sum(p, axis=0, keepdims=True)
            probs.append(p.astype(BF16))
            alphas.append(alpha)
        for h in range(nh):
            acc_scr[h][...] = alphas[h] * acc_scr[h][...] + jnp.dot(vt_ref[h, n].astype(BF16), probs[h],
                                                                    preferred_element_type=F32)
        return carry

    lax.fori_loop(0, i, body, 0)
    for h in range(nh):
        o_ref[h] = acc_scr[h][...] / l_scr[h][...]


def _moba_branch_pallas(p_c, bt, seq):
    nh, dh, bs = MOBA_HEADS, MOBA_HEAD_DIM, MOBA_BLOCK
    assert seq % bs == 0
    nblk = seq // bs
    qkv = p_c.reshape(bt, seq, 3, nh, dh)
    qt = qkv[:, :, 0].transpose(0, 2, 3, 1)
    k5 = qkv[:, :, 1].reshape(bt, nblk, bs, nh, dh).transpose(0, 3, 1, 2, 4)
    vt5 = qkv[:, :, 2].reshape(bt, nblk, bs, nh, dh).transpose(0, 3, 1, 4, 2)
    ot = pl.pallas_call(
        _moba_kernel,
        grid=(bt, nblk),
        in_specs=[pl.BlockSpec((None, nh, dh, bs), lambda b, i: (b, 0, 0, i)),
                  pl.BlockSpec((None, nh, nblk, bs, dh), lambda b, i: (b, 0, 0, 0, 0)),
                  pl.BlockSpec((None, nh, nblk, dh, bs), lambda b, i: (b, 0, 0, 0, 0))],
        out_specs=pl.BlockSpec((None, nh, dh, bs), lambda b, i: (b, 0, 0, i)),
        out_shape=jax.ShapeDtypeStruct((bt, nh, dh, seq), F32),
        scratch_shapes=[pltpu.VMEM((nh, nblk, dh), F32), pltpu.VMEM((nh, nblk, bs), F32),
                        pltpu.VMEM((nh, dh, bs), BF16)]
        + [pltpu.VMEM((1, bs), F32)] * (2 * nh) + [pltpu.VMEM((dh, bs), F32)] * nh,
        compiler_params=_cparams("parallel", "arbitrary"),
        name="moba_attn",
    )(qt, k5, vt5)
    return ot.transpose(0, 3, 1, 2).reshape(bt * seq, nh * dh)


def kernel(x, c, ada_w, ada_b, norm1_w, w_in, gdn_conv_w, gdn_a_log, gdn_dt_bias, gdn_norm_w, s5_a_re, s5_a_im, s5_b_re, s5_b_im, s5_c_re, s5_c_im, s5_d, s5_log_dt, s5_glu_w, s5_glu_b, w_branch_a, w_branch_b, w_branch_c, w_out, norm2_w, peer_wq, peer_k1, peer_k2, peer_u, peer_v, final_norm_w):
    bt, seq, d = x.shape
    n = bt * seq
    depth = ada_w.shape[0]
    x2d = x.reshape(n, d)
    mod = _ada_call(c, ada_w, ada_b)
    gate_perm = np.asarray(_SLOT_AT, np.int32)
    for l in range(depth):
        sh1, sc1, g1, sh2, sc2, g2 = (mod[l, :, j * d:(j + 1) * d].reshape(bt, 1, d) for j in range(6))
        p_a, p_ub, p_c, p_gate, p_ba = _in_proj_call(
            x2d, norm1_w[l].reshape(1, d), sc1, sh1, _arrange_w_in(w_in[l]), seq)
        o_a = _gdn_branch_pallas(p_a, p_ba, gdn_conv_w[l], gdn_a_log[l], gdn_dt_bias[l], gdn_norm_w[l], bt, seq)
        o_b = _s5_branch_pallas(p_ub, bt, seq, s5_a_re[l], s5_a_im[l], s5_b_re[l], s5_b_im[l], s5_c_re[l],
                                s5_c_im[l], s5_d[l], s5_log_dt[l], s5_glu_w[l], s5_glu_b[l])
        o_c = _moba_branch_pallas(p_c, bt, seq)
        x2d = _merge_call(o_a, o_b, o_c, p_gate, x2d, g1, w_branch_a[l].astype(BF16), w_branch_b[l].astype(BF16),
                          w_branch_c[l].astype(BF16), w_out[l].astype(BF16), seq)
        h2, idx_t, gate_t = _peer_query_call(x2d, norm2_w[l].reshape(1, d), sc2, sh2,
                                             peer_wq[l].T.astype(BF16), peer_k1[l], peer_k2[l], seq)
        uv = jnp.concatenate([peer_u[l].reshape(-1, SUBLANES, LANES), peer_v[l].reshape(-1, SUBLANES, LANES)], axis=1)
        x3 = _peer_gather_call(idx_t.T, gate_t.T[:, gate_perm], h2.reshape(n, SUBLANES, LANES),
                               x2d.reshape(n, SUBLANES, LANES), g2.reshape(bt, SUBLANES, LANES), uv, seq)
        x2d = x3.reshape(n, d)
    return _final_norm_call(x2d, final_norm_w.reshape(1, d)).reshape(bt, seq, d)
```

```python
import functools
import math

import jax
import jax.numpy as jnp
import numpy as np
from jax import lax
from jax.experimental import pallas as pl
from jax.experimental.pallas import tpu as pltpu

F32 = jnp.float32
BF16 = jnp.bfloat16
HIGHEST = lax.Precision.HIGHEST

D_MODEL = 1024
GDN_HEADS = 4
GDN_HEAD_DIM = 128
GDN_WIDTH = GDN_HEADS * GDN_HEAD_DIM
GDN_CONV = 4
GDN_CHUNK = 64
S5_GROUP = 16
S5_GROUPS = 16
S5_WIDTH = S5_GROUPS * S5_GROUP
S5_STATE = 64
MOBA_HEADS = 4
MOBA_HEAD_DIM = 64
MOBA_WIDTH = MOBA_HEADS * MOBA_HEAD_DIM
MOBA_BLOCK = 256
MOBA_TOPK = 3
MOBA_Q_CHUNK = 64
N_BRANCH = 3
PEER_HEADS = 8
PEER_NKEYS = 128
PEER_QDIM = 256
PEER_TOPK = 16
PEER_SEL = PEER_HEADS * PEER_TOPK
RMS_EPS = 1e-6
NEG_INF = -1e30

SUBLANES = 8
LANES = 128
VMEM_LIMIT_BYTES = 56 * 1024 * 1024

_OFF_QKV_A = 0
_OFF_Z_A = 3 * GDN_WIDTH
_OFF_BETA = _OFF_Z_A + GDN_WIDTH
_OFF_ALPHA = _OFF_BETA + GDN_HEADS
_OFF_UB = _OFF_ALPHA + GDN_HEADS
_OFF_QKV_C = _OFF_UB + S5_WIDTH
_OFF_GATE = _OFF_QKV_C + 3 * MOBA_WIDTH
_IN_COLS = _OFF_GATE + N_BRANCH * D_MODEL


def _cparams(*sem):
    return pltpu.CompilerParams(dimension_semantics=sem, vmem_limit_bytes=VMEM_LIMIT_BYTES)


def _resident(shape):
    nd = len(shape)
    return pl.BlockSpec(shape, lambda *_: (0,) * nd)


def _ada_kernel(c_ref, w_ref, b_ref, o_ref):
    c = c_ref[...]
    sc = c * jax.nn.sigmoid(c)
    o_ref[...] = jnp.dot(sc, w_ref[...], precision=HIGHEST, preferred_element_type=F32) + b_ref[...]


def _ada_call(c, ada_w, ada_b):
    depth, d, d6 = ada_w.shape
    bt = c.shape[0]
    nj = d6 // d
    return pl.pallas_call(
        _ada_kernel,
        grid=(depth, nj),
        in_specs=[
            pl.BlockSpec((bt, d), lambda l, j: (0, 0)),
            pl.BlockSpec((None, d, d), lambda l, j: (l, 0, j)),
            pl.BlockSpec((None, 1, d), lambda l, j: (l, 0, j)),
        ],
        out_specs=pl.BlockSpec((None, bt, d), lambda l, j: (l, 0, j)),
        out_shape=jax.ShapeDtypeStruct((depth, bt, d6), F32),
        compiler_params=_cparams("parallel", "parallel"),
        name="ada_mod",
    )(c, ada_w, ada_b.reshape(depth, 1, d6))


def _norm_mod(x, nw, sc, sh):
    y = x * lax.rsqrt(jnp.mean(x * x, axis=-1, keepdims=True) + RMS_EPS)
    return (y * nw) * (1.0 + sc) + sh


_IN_SPLITS = (4 * GDN_WIDTH, S5_WIDTH, 3 * MOBA_WIDTH, N_BRANCH * D_MODEL, LANES)
_IN_DTYPES = (F32, F32, F32, BF16, F32)


def _in_proj_kernel(x_ref, nw_ref, sc_ref, sh_ref, w_ref, oa_ref, ob_ref, oc_ref, og_ref, oba_ref):
    h = _norm_mod(x_ref[...], nw_ref[...], sc_ref[...], sh_ref[...]).astype(BF16)
    off = 0
    for o_ref, width in zip((oa_ref, ob_ref, oc_ref, og_ref, oba_ref), _IN_SPLITS):
        o_ref[...] = jnp.dot(h, w_ref[:, off:off + width], preferred_element_type=F32).astype(o_ref.dtype)
        off += width


def _in_proj_call(x2d, nw, sc, sh, w_r, seq, tm=256):
    n, d = x2d.shape
    per_b = seq // tm
    wcols = w_r.shape[1]
    outs = [jax.ShapeDtypeStruct((n, wd), dt) for wd, dt in zip(_IN_SPLITS, _IN_DTYPES)]
    return pl.pallas_call(
        _in_proj_kernel,
        grid=(n // tm,),
        in_specs=[
            pl.BlockSpec((tm, d), lambda i: (i, 0)),
            _resident((1, d)),
            pl.BlockSpec((None, 1, d), lambda i: (i // per_b, 0, 0)),
            pl.BlockSpec((None, 1, d), lambda i: (i // per_b, 0, 0)),
            _resident((d, wcols)),
        ],
        out_specs=[pl.BlockSpec((tm, wd), lambda i: (i, 0)) for wd in _IN_SPLITS],
        out_shape=outs,
        compiler_params=_cparams("parallel"),
        name="in_proj",
    )(x2d, nw, sc, sh, w_r)


def _arrange_w_in(w_in_l):
    pad = jnp.zeros((w_in_l.shape[0], LANES - 2 * GDN_HEADS), w_in_l.dtype)
    return jnp.concatenate([
        w_in_l[:, _OFF_QKV_A:_OFF_BETA],
        w_in_l[:, _OFF_UB:_OFF_QKV_C],
        w_in_l[:, _OFF_QKV_C:_OFF_GATE],
        w_in_l[:, _OFF_GATE:_IN_COLS],
        w_in_l[:, _OFF_BETA:_OFF_UB], pad,
    ], axis=1).astype(BF16)


def _merge_kernel(oa_ref, ob_ref, oc_ref, gt_ref, x_ref, g1_ref, wa_ref, wb_ref, wc_ref, wo_ref, o_ref):
    d = D_MODEL
    ya = jnp.dot(oa_ref[...].astype(BF16), wa_ref[...], preferred_element_type=F32)
    yb = jnp.dot(ob_ref[...].astype(BF16), wb_ref[...], preferred_element_type=F32)
    yc = jnp.dot(oc_ref[...].astype(BF16), wc_ref[...], preferred_element_type=F32)
    gate = lambda j: jax.nn.sigmoid(gt_ref[:, j * d:(j + 1) * d].astype(F32))
    merged = gate(0) * ya + gate(1) * yb + gate(2) * yc
    y = jnp.dot(merged.astype(BF16), wo_ref[...], preferred_element_type=F32)
    o_ref[...] = x_ref[...] + g1_ref[...] * y


def _merge_call(o_a, o_b, o_c, p_gate, x2d, g1, wa, wb, wc, wo, seq, tm=512):
    n, d = x2d.shape
    per_b = seq // tm
    row = lambda wd: pl.BlockSpec((tm, wd), lambda i: (i, 0))
    return pl.pallas_call(
        _merge_kernel,
        grid=(n // tm,),
        in_specs=[row(GDN_WIDTH), row(S5_WIDTH), row(MOBA_WIDTH), row(N_BRANCH * d), row(d),
                  pl.BlockSpec((None, 1, d), lambda i: (i // per_b, 0, 0)),
                  _resident(wa.shape), _resident(wb.shape), _resident(wc.shape), _resident(wo.shape)],
        out_specs=row(d),
        out_shape=jax.ShapeDtypeStruct((n, d), F32),
        compiler_params=_cparams("parallel"),
        name="merge_out",
    )(o_a, o_b, o_c, p_gate, x2d, g1, wa, wb, wc, wo)


def _candidate_tables():
    k = PEER_TOPK
    pairs = [(a, b) for a in range(k) for b in range(k) if (a + 1) * (b + 1) <= k]
    rows = -(-len(pairs) // SUBLANES) * SUBLANES
    sel = np.zeros((rows, 2 * k), np.float32)
    sel_id = np.zeros((rows, 2 * k), np.float32)
    bias = np.zeros((rows, 1), np.float32)
    order = np.zeros((rows, 1), np.float32)
    for r, (a, b) in enumerate(pairs):
        sel[r, a] = sel[r, k + b] = 1.0
        sel_id[r, a] = float(PEER_NKEYS)
        sel_id[r, k + b] = 1.0
        order[r, 0] = a * k + b
    for r in range(len(pairs), rows):
        bias[r, 0] = -np.inf
        order[r, 0] = k * k + r
    return sel, sel_id, bias, order


_CAND_SEL, _CAND_SEL_ID, _CAND_BIAS, _CAND_ORDER = _candidate_tables()
_CAND_ROWS = _CAND_SEL.shape[0]


def _extract_max(s, order, big, payload=None):
    m = jnp.max(s, axis=0, keepdims=True)
    pos = jnp.min(jnp.where(s == m, order, big), axis=0, keepdims=True)
    hit = order == pos
    tag = pos if payload is None else jnp.sum(jnp.where(hit, payload, 0.0), axis=0, keepdims=True)
    return m, tag, jnp.where(hit, -jnp.inf, s)


def _peer_keyproj_kernel(k_ref, wq_ref, o_ref):
    o_ref[...] = lax.dot_general(k_ref[...], wq_ref[...], (((1,), (1,)), ((), ())), precision=HIGHEST,
                                 preferred_element_type=F32).astype(o_ref.dtype)


def _peer_keyproj_call(wq, k1, k2):
    d = wq.shape[0]
    half = PEER_QDIM // 2
    keys = jnp.stack([k1, k2], axis=1).reshape(2 * PEER_HEADS, PEER_NKEYS, half)
    return pl.pallas_call(
        _peer_keyproj_kernel,
        grid=(2 * PEER_HEADS,),
        in_specs=[pl.BlockSpec((None, PEER_NKEYS, half), lambda g: (g, 0, 0)),
                  pl.BlockSpec((d, half), lambda g: (0, g))],
        out_specs=pl.BlockSpec((PEER_NKEYS, d), lambda g: (g, 0)),
        out_shape=jax.ShapeDtypeStruct((2 * PEER_HEADS * PEER_NKEYS, d), BF16),
        compiler_params=_cparams("parallel"),
        name="peer_keyproj",
    )(keys, wq)


def _peer_query_kernel(x_ref, nw_ref, sc_ref, sh_ref, wk_ref, sel_ref, selid_ref, cb_ref, co_ref,
                       h_ref, idx_ref, gate_ref, s_scr, v_scr, i_scr, cs_scr, ci_scr, ts_scr, te_scr):
    h = _norm_mod(x_ref[...], nw_ref[...], sc_ref[...], sh_ref[...])
    h_ref[...] = h
    s_scr[...] = lax.dot_general(wk_ref[...], h.astype(BF16), (((1,), (1,)), ((), ())),
                                 preferred_element_type=F32)
    nk = PEER_NKEYS
    k = PEER_TOPK
    key_iota = lax.broadcasted_iota(jnp.int32, (nk, s_scr.shape[1]), 0).astype(F32)
    cand_order = jnp.broadcast_to(co_ref[...], cs_scr.shape)

    for hd in range(PEER_HEADS + 1):
        do_keys, do_cand = hd < PEER_HEADS, hd > 0
        r1 = slice(2 * hd * nk, (2 * hd + 1) * nk)
        r2 = slice((2 * hd + 1) * nk, (2 * hd + 2) * nk)

        def trip(j, c, do_keys=do_keys, do_cand=do_cand, r1=r1, r2=r2):
            if do_keys:
                m1, p1, s1 = _extract_max(s_scr[r1, :], key_iota, float(nk))
                m2, p2, s2 = _extract_max(s_scr[r2, :], key_iota, float(nk))
                s_scr[r1, :] = s1
                s_scr[r2, :] = s2
                v_scr[pl.ds(j, 1), :] = m1
                i_scr[pl.ds(j, 1), :] = p1
                v_scr[pl.ds(j + k, 1), :] = m2
                i_scr[pl.ds(j + k, 1), :] = p2
            if do_cand:
                m, e, s = _extract_max(cs_scr[...], cand_order, float(4 * k * k), payload=ci_scr[...])
                cs_scr[...] = s
                ts_scr[pl.ds(j, 1), :] = m
                te_scr[pl.ds(j, 1), :] = e
            return c

        lax.fori_loop(0, k, trip, 0)
        if do_cand:
            ts = ts_scr[...]
            e = jnp.exp(ts - jnp.max(ts, axis=0, keepdims=True))
            out = (hd - 1) * k
            gate_ref[out:out + k, :] = e / jnp.sum(e, axis=0, keepdims=True)
            idx_ref[out:out + k, :] = te_scr[...].astype(jnp.int32)
        if do_keys:
            cs_scr[...] = (jnp.dot(sel_ref[...], v_scr[...], precision=HIGHEST, preferred_element_type=F32)
                           + cb_ref[...])
            ci_scr[...] = jnp.dot(selid_ref[...], i_scr[...].astype(BF16), preferred_element_type=F32)


def _peer_query_call(x2d, nw, sc, sh, wk, seq, tm=512):
    n, d = x2d.shape
    per_b = seq // tm
    k = PEER_TOPK
    scratch = [pltpu.VMEM((wk.shape[0], tm), F32),
               pltpu.VMEM((2 * k, tm), F32), pltpu.VMEM((2 * k, tm), F32),
               pltpu.VMEM((_CAND_ROWS, tm), F32), pltpu.VMEM((_CAND_ROWS, tm), F32),
               pltpu.VMEM((k, tm), F32), pltpu.VMEM((k, tm), F32)]
    cand = [jnp.asarray(_CAND_SEL), jnp.asarray(_CAND_SEL_ID, dtype=BF16), jnp.asarray(_CAND_BIAS),
            jnp.asarray(_CAND_ORDER)]
    return pl.pallas_call(
        _peer_query_kernel,
        grid=(n // tm,),
        in_specs=[
            pl.BlockSpec((tm, d), lambda i: (i, 0)),
            _resident((1, d)),
            pl.BlockSpec((None, 1, d), lambda i: (i // per_b, 0, 0)),
            pl.BlockSpec((None, 1, d), lambda i: (i // per_b, 0, 0)),
            _resident(wk.shape),
        ] + [_resident(t.shape) for t in cand],
        out_specs=[pl.BlockSpec((tm, d), lambda i: (i, 0)),
                   pl.BlockSpec((PEER_SEL, tm), lambda i: (0, i)),
                   pl.BlockSpec((PEER_SEL, tm), lambda i: (0, i))],
        out_shape=[jax.ShapeDtypeStruct((n, d), F32),
                   jax.ShapeDtypeStruct((PEER_SEL, n), jnp.int32),
                   jax.ShapeDtypeStruct((PEER_SEL, n), F32)],
        scratch_shapes=scratch,
        compiler_params=_cparams("parallel"),
        name="peer_query",
    )(x2d, nw, sc, sh, wk, *cand)


PEER_TOKENS_PER_STEP = 8
PEER_DMA_THREADS = 2
_ROW_SUB = 2 * SUBLANES


def _fold_pair(a, b, shift, keep_a):
    fa = a + pltpu.roll(a, shift, axis=0)
    fb = b + pltpu.roll(b, SUBLANES - shift, axis=0)
    return jnp.where(keep_a, fa, fb)


def _slot_order():
    tiles = [[k] * SUBLANES for k in range(PEER_SEL)]
    for shift, keep in ((4, [j < 4 for j in range(8)]), (2, [(j % 4) >= 2 for j in range(8)]),
                        (1, [(j % 2) == 1 for j in range(8)])):
        tiles = [[tiles[2 * m][j] if keep[j] else tiles[2 * m + 1][j] for j in range(8)]
                 for m in range(len(tiles) // 2)]
    return [s for t in tiles for s in t]


_SLOT_AT = _slot_order()
_POS_OF = [0] * PEER_SEL
for _p, _s in enumerate(_SLOT_AT):
    _POS_OF[_s] = _p


def _peer_gather_kernel(idx0_ref, idxc_ref, idxn_ref, g_ref, h_ref, x_ref, g2_ref, uv_hbm, o_ref,
                        buf0, buf1, wb_scr, sems):
    tb = PEER_TOKENS_PER_STEP
    blk = tb * PEER_SEL
    i = pl.program_id(0)
    bufs = (buf0, buf1)

    def row_copy(e, half, row, t):
        return pltpu.make_async_copy(uv_hbm.at[e], bufs[half].at[row], sems.at[half * tb + t])

    def slot_copy(half, t):
        return pltpu.make_async_copy(uv_hbm.at[pl.ds(0, PEER_SEL)], bufs[half].at[pl.ds(t * PEER_SEL, PEER_SEL)],
                                     sems.at[half * tb + t])

    @pl.when(i == 0)
    def _():
        def issue0(t, c):
            for k in range(PEER_SEL):
                row_copy(idx0_ref[0, t * PEER_SEL + k], 0, t * PEER_SEL + k, t).start()
            return c
        lax.fori_loop(0, tb, issue0, 0)

    sub = lax.broadcasted_iota(jnp.int32, (SUBLANES, LANES), 0)
    keep4 = sub < 4
    keep2 = (sub % 4) >= 2
    keep1 = (sub % 2) == 1
    eye = (lax.broadcasted_iota(jnp.int32, (PEER_SEL, LANES), 0)
           == lax.broadcasted_iota(jnp.int32, (PEER_SEL, LANES), 1))

    for half in range(2):
        nidx_ref, noff = (idxc_ref, blk) if half == 0 else (idxn_ref, 0)
        buf = bufs[half]
        for t in range(tb):
            tt = half * tb + t
            slot_copy(half, t).wait()
            for k in range(PEER_SEL):
                row_copy(nidx_ref[0, noff + t * PEER_SEL + k], 1 - half, t * PEER_SEL + k, t).start(
                    priority=k % PEER_DMA_THREADS)
            base = t * PEER_SEL
            ht = h_ref[tt]
            tiles = [buf[base + k, 0:SUBLANES, :] * ht for k in range(PEER_SEL)]
            tiles = [_fold_pair(tiles[2 * m], tiles[2 * m + 1], 4, keep4) for m in range(PEER_SEL // 2)]
            tiles = [_fold_pair(tiles[2 * m], tiles[2 * m + 1], 2, keep2) for m in range(PEER_SEL // 4)]
            tiles = [_fold_pair(tiles[2 * m], tiles[2 * m + 1], 1, keep1) for m in range(PEER_SEL // 8)]
            part = jnp.concatenate(tiles, axis=0)
            s = jnp.sum(part, axis=-1, keepdims=True)
            act = 0.5 * s * (1.0 + lax.erf(s * (2.0 ** -0.5)))
            grow = jnp.broadcast_to(g_ref[tt:tt + 1, :], (PEER_SEL, LANES))
            gcol = jnp.sum(jnp.where(eye, grow, 0.0), axis=-1, keepdims=True)
            wbase = tt * PEER_SEL
            wb_scr[wbase:wbase + PEER_SEL, :] = jnp.broadcast_to(gcol * act, (PEER_SEL, LANES))
            accs = [jnp.zeros((SUBLANES, LANES), F32) for _ in range(4)]
            for k in range(PEER_SEL):
                p = wbase + _POS_OF[k]
                wk = jnp.broadcast_to(wb_scr[p:p + 1, :], (SUBLANES, LANES))
                accs[k % 4] = accs[k % 4] + wk * buf[base + k, SUBLANES:_ROW_SUB, :]
            y = (accs[0] + accs[1]) + (accs[2] + accs[3])
            o_ref[tt] = x_ref[tt] + g2_ref[...] * y

    @pl.when(i == pl.num_programs(0) - 1)
    def _():
        for t in range(tb):
            slot_copy(0, t).wait()


def _peer_gather_call(idx, gate_p, h3, x3, g2_3, uv, seq):
    n = idx.shape[0]
    tb = PEER_TOKENS_PER_STEP
    ns = n // (2 * tb)
    per_b = seq // (2 * tb)
    idx3 = idx.reshape(ns, 1, 2 * tb * PEER_SEL)
    smem_blk = lambda fn: pl.BlockSpec((None, 1, 2 * tb * PEER_SEL), fn, memory_space=pltpu.SMEM)
    tok3 = pl.BlockSpec((2 * tb, SUBLANES, LANES), lambda i: (i, 0, 0))
    return pl.pallas_call(
        _peer_gather_kernel,
        grid=(ns,),
        in_specs=[
            smem_blk(lambda i: (0, 0, 0)),
            smem_blk(lambda i: (i, 0, 0)),
            smem_blk(lambda i: (jnp.minimum(i + 1, ns - 1), 0, 0)),
            pl.BlockSpec((2 * tb, PEER_SEL), lambda i: (i, 0)),
            tok3, tok3,
            pl.BlockSpec((None, SUBLANES, LANES), lambda i: (i // per_b, 0, 0)),
            pl.BlockSpec(memory_space=pl.ANY),
        ],
        out_specs=tok3,
        out_shape=jax.ShapeDtypeStruct((n, SUBLANES, LANES), F32),
        scratch_shapes=[pltpu.VMEM((tb * PEER_SEL, _ROW_SUB, LANES), F32),
                        pltpu.VMEM((tb * PEER_SEL, _ROW_SUB, LANES), F32),
                        pltpu.VMEM((2 * tb * PEER_SEL, LANES), F32),
                        pltpu.SemaphoreType.DMA((2 * tb,))],
        compiler_params=_cparams("arbitrary"),
        name="peer_gather",
    )(idx3, idx3, idx3, gate_p, h3, x3, g2_3, uv)


def _final_norm_kernel(x_ref, w_ref, o_ref):
    x = x_ref[...]
    o_ref[...] = x * lax.rsqrt(jnp.mean(x * x, axis=-1, keepdims=True) + RMS_EPS) * w_ref[...]


def _final_norm_call(x2d, w, tm=1024):
    n, d = x2d.shape
    return pl.pallas_call(
        _final_norm_kernel,
        grid=(n // tm,),
        in_specs=[pl.BlockSpec((tm, d), lambda i: (i, 0)), _resident((1, d))],
        out_specs=pl.BlockSpec((tm, d), lambda i: (i, 0)),
        out_shape=jax.ShapeDtypeStruct((n, d), F32),
        compiler_params=_cparams("parallel"),
        name="final_norm",
    )(x2d, w)


S5_LANES = S5_GROUPS * S5_STATE


def _s5_disc_kernel(are_ref, aim_ref, ldt_ref, bre_ref, bim_ref, ar_ref, ai_ref, br_ref, bi_ref):
    a_re = are_ref[...]
    a_im = aim_ref[...]
    dt = jnp.exp(ldt_ref[...])
    mag = jnp.exp(a_re * dt)
    ar = mag * jnp.cos(a_im * dt)
    ai = mag * jnp.sin(a_im * dt)
    den = a_re * a_re + a_im * a_im
    cr = ((ar - 1.0) * a_re + ai * a_im) / den
    ci = (ai * a_re - (ar - 1.0) * a_im) / den
    ar_ref[...] = ar
    ai_ref[...] = ai
    br_ref[...] = cr * bre_ref[...] - ci * bim_ref[...]
    bi_ref[...] = cr * bim_ref[...] + ci * bre_ref[...]


def _s5_disc_call(a_re, a_im, log_dt, b_re, b_im):
    rows = S5_LANES
    col = lambda t: t.reshape(rows, 1)
    ldt = jnp.broadcast_to(log_dt[:, None], (S5_GROUPS, S5_STATE))
    out = [jax.ShapeDtypeStruct((rows, 1), F32)] * 2 + [jax.ShapeDtypeStruct((rows, S5_GROUP), F32)] * 2
    return pl.pallas_call(_s5_disc_kernel, out_shape=out, name="s5_disc")(
        col(a_re), col(a_im), col(ldt), b_re.reshape(rows, S5_GROUP), b_im.reshape(rows, S5_GROUP))


def _s5_kernel(u_ref, bcat_ref, ar_ref, ai_ref, ccat_ref, d_ref, gw_ref, gb_ref, o_ref, st_scr, bu_scr, *, steps, bt):
    @pl.when(pl.program_id(0) == 0)
    def _():
        st_scr[...] = jnp.zeros_like(st_scr)

    u = u_ref[...]
    bu_scr[...] = jnp.dot(u.astype(BF16), bcat_ref[...], preferred_element_type=F32)
    a_re = jnp.broadcast_to(ar_ref[...], (bt, S5_LANES))
    a_im = jnp.broadcast_to(ai_ref[...], (bt, S5_LANES))

    def step(t, carry):
        s_re, s_im = carry
        r = pl.multiple_of(t * bt, bt)
        n_re = a_re * s_re - a_im * s_im + bu_scr[pl.ds(r, bt), 0:S5_LANES]
        n_im = a_re * s_im + a_im * s_re + bu_scr[pl.ds(r, bt), S5_LANES:2 * S5_LANES]
        bu_scr[pl.ds(r, bt), 0:S5_LANES] = n_re
        bu_scr[pl.ds(r, bt), S5_LANES:2 * S5_LANES] = n_im
        return n_re, n_im

    s_re, s_im = lax.fori_loop(0, steps, step, (st_scr[:, 0:S5_LANES], st_scr[:, S5_LANES:2 * S5_LANES]))
    st_scr[:, 0:S5_LANES] = s_re
    st_scr[:, S5_LANES:2 * S5_LANES] = s_im
    y = jnp.dot(bu_scr[...].astype(BF16), ccat_ref[...], preferred_element_type=F32) + d_ref[...] * u
    zg = 0.5 * y * (1.0 + lax.erf(y * (2.0 ** -0.5)))
    zz = jnp.dot(zg.astype(BF16), gw_ref[...], preferred_element_type=F32) + gb_ref[...]
    o_ref[...] = zz[:, 0:S5_WIDTH] * jax.nn.sigmoid(zz[:, S5_WIDTH:2 * S5_WIDTH])


def _s5_call(u_tb, bcat, abar_re, abar_im, ccat, d, glu_w, glu_b, bt, steps=64):
    rows = u_tb.shape[0]
    blk = steps * bt
    return pl.pallas_call(
        functools.partial(_s5_kernel, steps=steps, bt=bt),
        grid=(rows // blk,),
        in_specs=[pl.BlockSpec((blk, S5_WIDTH), lambda i: (i, 0)),
                  _resident(bcat.shape), _resident(abar_re.shape), _resident(abar_im.shape), _resident(ccat.shape),
                  _resident(d.shape), _resident(glu_w.shape), _resident(glu_b.shape)],
        out_specs=pl.BlockSpec((blk, S5_WIDTH), lambda i: (i, 0)),
        out_shape=jax.ShapeDtypeStruct((rows, S5_WIDTH), F32),
        scratch_shapes=[pltpu.VMEM((bt, 2 * S5_LANES), F32), pltpu.VMEM((blk, 2 * S5_LANES), F32)],
        compiler_params=_cparams("arbitrary"),
        name="s5_scan",
    )(u_tb, bcat, abar_re, abar_im, ccat, d, glu_w, glu_b)


def _s5_branch_pallas(p_ub, bt, seq, a_re, a_im, b_re, b_im, c_re, c_im, d, log_dt, glu_w, glu_b):
    ar, ai, br, bi = _s5_disc_call(a_re, a_im, log_dt, b_re, b_im)
    g_ix = jnp.arange(S5_GROUPS)
    def blockdiag_in(b):
        b = b.reshape(S5_GROUPS, S5_STATE, S5_GROUP)
        full = jnp.zeros((S5_GROUPS, S5_GROUP, S5_GROUPS, S5_STATE), F32)
        return full.at[g_ix, :, g_ix, :].set(b.transpose(0, 2, 1)).reshape(S5_WIDTH, S5_LANES)

    def blockdiag_out(c):
        full = jnp.zeros((S5_GROUPS, S5_STATE, S5_GROUPS, S5_GROUP), F32)
        return full.at[g_ix, :, g_ix, :].set(c.transpose(0, 2, 1)).reshape(S5_LANES, S5_WIDTH)

    bcat = jnp.concatenate([blockdiag_in(br), blockdiag_in(bi)], axis=1).astype(BF16)
    ccat = jnp.concatenate([blockdiag_out(c_re), -blockdiag_out(c_im)], axis=0).astype(BF16)
    u_tb = p_ub.reshape(bt, seq, S5_WIDTH).transpose(1, 0, 2).reshape(seq * bt, S5_WIDTH)
    o_tb = _s5_call(u_tb, bcat, ar.reshape(1, S5_LANES), ai.reshape(1, S5_LANES), ccat, d.reshape(1, S5_WIDTH),
                    glu_w.astype(BF16), glu_b.reshape(1, 2 * S5_WIDTH), bt)
    return o_tb.reshape(seq, bt, S5_WIDTH).transpose(1, 0, 2).reshape(bt * seq, S5_WIDTH)


def _mm(a, b):
    return jnp.dot(a.astype(BF16), b.astype(BF16), preferred_element_type=F32)


def _mm_nt(a, b, precision=None):
    if precision is None:
        a, b = a.astype(BF16), b.astype(BF16)
    return lax.dot_general(a, b, (((1,), (1,)), ((), ())), precision=precision, preferred_element_type=F32)


def _mm_tn(a, b):
    return lax.dot_general(a.astype(BF16), b.astype(BF16), (((0,), (0,)), ((), ())), preferred_element_type=F32)


def _softplus(x):
    return jnp.maximum(x, 0.0) + jnp.log1p(jnp.exp(-jnp.abs(x)))


def _gdn_kernel(pa_ref, ba_ref, cw_ref, alog_ref, dtb_ref, nw_ref, o_ref,
                st_scr, carry_scr, q_scr, k_scr, v_scr, g_scr, b_scr, u_scr, w_scr, qk_scr, gi_scr):
    j = pl.program_id(1)
    t_blk = pa_ref.shape[0]
    c_len, wid, dh = GDN_CHUNK, GDN_WIDTH, GDN_HEAD_DIM

    @pl.when(j == 0)
    def _():
        st_scr[...] = jnp.zeros_like(st_scr)
        carry_scr[...] = jnp.zeros_like(carry_scr)

    x = pa_ref[:, 0:3 * wid]
    xx = jnp.concatenate([carry_scr[...], x], axis=0)
    conv = cw_ref[GDN_CONV - 1:GDN_CONV, :] * x
    for k in range(1, GDN_CONV):
        conv = conv + cw_ref[GDN_CONV - 1 - k:GDN_CONV - k, :] * pltpu.roll(xx, k, axis=0)[SUBLANES:, :]
    carry_scr[...] = x[t_blk - SUBLANES:t_blk, :]
    act = conv * jax.nn.sigmoid(conv)
    for h in range(GDN_HEADS):
        qh = act[:, h * dh:(h + 1) * dh]
        kh = act[:, wid + h * dh:wid + (h + 1) * dh]
        q_scr[:, h * dh:(h + 1) * dh] = qh * lax.rsqrt(jnp.sum(qh * qh, axis=-1, keepdims=True) + 1e-6) * (dh ** -0.5)
        k_scr[:, h * dh:(h + 1) * dh] = kh * lax.rsqrt(jnp.sum(kh * kh, axis=-1, keepdims=True) + 1e-6)
    v_scr[...] = act[:, 2 * wid:3 * wid]
    ba = ba_ref[...]
    g_scr[...] = -jnp.exp(alog_ref[...]) * _softplus(ba + dtb_ref[...])
    b_scr[...] = jax.nn.sigmoid(ba)

    row = lax.broadcasted_iota(jnp.int32, (t_blk, t_blk), 0)
    col = lax.broadcasted_iota(jnp.int32, (t_blk, t_blk), 1)
    same = (row // c_len) == (col // c_len)
    tril = same & (row >= col)
    strict = same & (row > col)
    eye = jnp.where(row == col, 1.0, 0.0)
    lane = lax.broadcasted_iota(jnp.int32, (t_blk, LANES), 1)
    nw = nw_ref[...]
    gcum = jnp.dot(jnp.where(tril, 1.0, 0.0), g_scr[...], precision=HIGHEST, preferred_element_type=F32)
    beta_all = b_scr[...]
    heads = range(GDN_HEADS)
    hsl = [slice(h * dh, (h + 1) * dh) for h in heads]
    gis = [gcum[:, GDN_HEADS + h:GDN_HEADS + h + 1] for h in heads]
    betas = [beta_all[:, h:h + 1] for h in heads]
    decays, ps, ts = [], [], []
    for h in heads:
        g1 = jnp.where(lane == 0, gis[h], jnp.where(lane == 1, 1.0, 0.0))
        g2 = jnp.where(lane == 0, 1.0, jnp.where(lane == 1, -gis[h], 0.0))
        diff = _mm_nt(g1, g2, precision=HIGHEST)
        decays.append(jnp.where(tril, jnp.exp(jnp.where(tril, diff, 0.0)), 0.0))
    for h in heads:
        k_h = k_scr[:, hsl[h]]
        a = jnp.where(strict, _mm_nt(k_h * betas[h], k_h) * decays[h], 0.0)
        ps.append(-a)
        ts.append(eye - a)
    for _ in range(int(math.log2(c_len)) - 1):
        ps = [_mm(p, p) for p in ps]
        ts = [t + _mm(t, p) for t, p in zip(ts, ps)]
    for h in heads:
        k_h, v_h, q_h = k_scr[:, hsl[h]], v_scr[:, hsl[h]], q_scr[:, hsl[h]]
        e_g = jnp.exp(gis[h])
        u_scr[:, hsl[h]] = _mm(ts[h], v_h * betas[h])
        w_scr[:, hsl[h]] = _mm(ts[h], k_h * betas[h] * e_g)
        qk_scr[h] = jnp.where(tril, _mm_nt(q_h, k_h) * decays[h], 0.0)
        q_scr[:, hsl[h]] = q_h * e_g
        gi_scr[:, h:h + 1] = gis[h]

    for c in range(t_blk // c_len):
        rows = slice(c * c_len, (c + 1) * c_len)
        states = [st_scr[h] for h in heads]
        v_news = [u_scr[rows, hsl[h]] - _mm(w_scr[rows, hsl[h]], states[h]) for h in heads]
        o_state = [_mm(q_scr[rows, hsl[h]], states[h]) for h in heads]
        outs = [o_state[h] + _mm(qk_scr[h, rows, rows], v_news[h]) for h in heads]
        for h in heads:
            gi = gi_scr[rows, h:h + 1]
            g_last = gi[c_len - 1:c_len, :]
            st_scr[h] = states[h] * jnp.exp(g_last) + _mm_tn(k_scr[rows, hsl[h]] * jnp.exp(g_last - gi), v_news[h])
        for h in heads:
            o = outs[h]
            z = pa_ref[rows, 3 * wid + h * dh:3 * wid + (h + 1) * dh]
            o_ref[rows, hsl[h]] = (o * lax.rsqrt(jnp.mean(o * o, axis=-1, keepdims=True) + RMS_EPS) * nw
                                   * (z * jax.nn.sigmoid(z)))


def _gdn_branch_pallas(p_a, p_ba, conv_w, a_log, dt_bias, norm_w, bt, seq, t_blk=256):
    n = p_a.shape[0]
    nj = seq // t_blk
    wid = GDN_WIDTH
    lane_pad = lambda v: jnp.zeros((1, LANES), F32).at[0, GDN_HEADS:2 * GDN_HEADS].set(v)
    rows = lambda wd: pl.BlockSpec((t_blk, wd), lambda b, j: (b * nj + j, 0))
    return pl.pallas_call(
        _gdn_kernel,
        grid=(bt, nj),
        in_specs=[rows(4 * wid), rows(LANES), _resident((GDN_CONV, 3 * wid)), _resident((1, LANES)),
                  _resident((1, LANES)), _resident((1, GDN_HEAD_DIM))],
        out_specs=rows(wid),
        out_shape=jax.ShapeDtypeStruct((n, wid), F32),
        scratch_shapes=[pltpu.VMEM((GDN_HEADS, GDN_HEAD_DIM, GDN_HEAD_DIM), F32), pltpu.VMEM((SUBLANES, 3 * wid), F32),
                        pltpu.VMEM((t_blk, wid), F32), pltpu.VMEM((t_blk, wid), F32), pltpu.VMEM((t_blk, wid), F32),
                        pltpu.VMEM((t_blk, LANES), F32), pltpu.VMEM((t_blk, LANES), F32),
                        pltpu.VMEM((t_blk, wid), F32), pltpu.VMEM((t_blk, wid), F32),
                        pltpu.VMEM((GDN_HEADS, t_blk, t_blk), F32), pltpu.VMEM((t_blk, LANES), F32)],
        compiler_params=_cparams("parallel", "arbitrary"),
        name="gdn_delta",
    )(p_a, p_ba, conv_w, lane_pad(a_log), lane_pad(dt_bias), norm_w.reshape(1, GDN_HEAD_DIM))


def _moba_kernel(qt_ref, k_ref, vt_ref, o_ref, km_scr, sel_scr, qb_scr, *head_scr):
    i = pl.program_id(1)
    nh, nblk = k_ref.shape[0], k_ref.shape[1]
    scale = MOBA_HEAD_DIM ** -0.5
    m_scr, l_scr, acc_scr = head_scr[0:nh], head_scr[nh:2 * nh], head_scr[2 * nh:3 * nh]

    @pl.when(i == 0)
    def _():
        for h in range(nh):
            for n in range(nblk):
                km_scr[h, n:n + 1, :] = jnp.mean(k_ref[h, n], axis=0, keepdims=True)

    blk = lax.broadcasted_iota(jnp.int32, (nblk, MOBA_BLOCK), 0)
    kpos = lax.broadcasted_iota(jnp.int32, (MOBA_BLOCK, MOBA_BLOCK), 0)
    qpos = lax.broadcasted_iota(jnp.int32, (MOBA_BLOCK, MOBA_BLOCK), 1)
    for h in range(nh):
        qt = qt_ref[h]
        gate = jnp.dot(km_scr[h], qt, precision=HIGHEST, preferred_element_type=F32)
        cnt = jnp.zeros(gate.shape, F32)
        for m in range(nblk):
            gm = gate[m:m + 1, :]
            beats = (gm > gate) | ((gm == gate) & (m < blk))
            cnt = cnt + jnp.where(beats & (m < i), 1.0, 0.0)
        sel_scr[h] = jnp.where((blk < i) & (cnt < float(MOBA_TOPK)), 1.0, 0.0)
        qb = qt.astype(BF16)
        qb_scr[h] = qb
        s = jnp.dot(k_ref[h, i].astype(BF16), qb, preferred_element_type=F32) * scale
        s = jnp.where(kpos <= qpos, s, NEG_INF)
        m0 = jnp.max(s, axis=0, keepdims=True)
        p = jnp.exp(s - m0)
        m_scr[h][...] = m0
        l_scr[h][...] = jnp.sum(p, axis=0, keepdims=True)
        acc_scr[h][...] = jnp.dot(vt_ref[h, i].astype(BF16), p.astype(BF16), preferred_element_type=F32)

    def body(n, carry):
        scores = [jnp.dot(k_ref[h, n].astype(BF16), qb_scr[h], preferred_element_type=F32) for h in range(nh)]
        probs, alphas = [], []
        for h in range(nh):
            m = m_scr[h][...]
            s = jnp.where(sel_scr[h, pl.ds(n, 1), :] > 0.5, scores[h] * scale, NEG_INF)
            m_new = jnp.maximum(m, jnp.max(s, axis=0, keepdims=True))
            p = jnp.exp(s - m_new)
            alpha = jnp.exp(m - m_new)
            m_scr[h][...] = m_new
            l_scr[h][...] = alpha * l_scr[h][...] + jnp.sum(p, axis=0, keepdims=True)
            probs.append(p.astype(BF16))
            alphas.append(alpha)
        for h in range(nh):
            acc_scr[h][...] = alphas[h] * acc_scr[h][...] + jnp.dot(vt_ref[h, n].astype(BF16), probs[h],
                                                                    preferred_element_type=F32)
        return carry

    lax.fori_loop(0, i, body, 0)
    for h in range(nh):
        o_ref[h] = acc_scr[h][...] / l_scr[h][...]


def _moba_branch_pallas(p_c, bt, seq):
    nh, dh, bs = MOBA_HEADS, MOBA_HEAD_DIM, MOBA_BLOCK
    assert seq % bs == 0
    nblk = seq // bs
    qkv = p_c.reshape(bt, seq, 3, nh, dh)
    qt = qkv[:, :, 0].transpose(0, 2, 3, 1)
    k5 = qkv[:, :, 1].reshape(bt, nblk, bs, nh, dh).transpose(0, 3, 1, 2, 4)
    vt5 = qkv[:, :, 2].reshape(bt, nblk, bs, nh, dh).transpose(0, 3, 1, 4, 2)
    ot = pl.pallas_call(
        _moba_kernel,
        grid=(bt, nblk),
        in_specs=[pl.BlockSpec((None, nh, dh, bs), lambda b, i: (b, 0, 0, i)),
                  pl.BlockSpec((None, nh, nblk, bs, dh), lambda b, i: (b, 0, 0, 0, 0)),
                  pl.BlockSpec((None, nh, nblk, dh, bs), lambda b, i: (b, 0, 0, 0, 0))],
        out_specs=pl.BlockSpec((None, nh, dh, bs), lambda b, i: (b, 0, 0, i)),
        out_shape=jax.ShapeDtypeStruct((bt, nh, dh, seq), F32),
        scratch_shapes=[pltpu.VMEM((nh, nblk, dh), F32), pltpu.VMEM((nh, nblk, bs), F32),
                        pltpu.VMEM((nh, dh, bs), BF16)]
        + [pltpu.VMEM((1, bs), F32)] * (2 * nh) + [pltpu.VMEM((dh, bs), F32)] * nh,
        compiler_params=_cparams("parallel", "arbitrary"),
        name="moba_attn",
    )(qt, k5, vt5)
    return ot.transpose(0, 3, 1, 2).reshape(bt * seq, nh * dh)


def kernel(x, c, ada_w, ada_b, norm1_w, w_in, gdn_conv_w, gdn_a_log, gdn_dt_bias, gdn_norm_w, s5_a_re, s5_a_im, s5_b_re, s5_b_im, s5_c_re, s5_c_im, s5_d, s5_log_dt, s5_glu_w, s5_glu_b, w_branch_a, w_branch_b, w_branch_c, w_out, norm2_w, peer_wq, peer_k1, peer_k2, peer_u, peer_v, final_norm_w):
    bt, seq, d = x.shape
    n = bt * seq
    depth = ada_w.shape[0]
    x2d = x.reshape(n, d)
    mod = _ada_call(c, ada_w, ada_b)
    gate_perm = np.asarray(_SLOT_AT, np.int32)
    for l in range(depth):
        sh1, sc1, g1, sh2, sc2, g2 = (mod[l, :, j * d:(j + 1) * d].reshape(bt, 1, d) for j in range(6))
        p_a, p_ub, p_c, p_gate, p_ba = _in_proj_call(
            x2d, norm1_w[l].reshape(1, d), sc1, sh1, _arrange_w_in(w_in[l]), seq)
        o_a = _gdn_branch_pallas(p_a, p_ba, gdn_conv_w[l], gdn_a_log[l], gdn_dt_bias[l], gdn_norm_w[l], bt, seq)
        o_b = _s5_branch_pallas(p_ub, bt, seq, s5_a_re[l], s5_a_im[l], s5_b_re[l], s5_b_im[l], s5_c_re[l],
                                s5_c_im[l], s5_d[l], s5_log_dt[l], s5_glu_w[l], s5_glu_b[l])
        o_c = _moba_branch_pallas(p_c, bt, seq)
        x2d = _merge_call(o_a, o_b, o_c, p_gate, x2d, g1, w_branch_a[l].astype(BF16), w_branch_b[l].astype(BF16),
                          w_branch_c[l].astype(BF16), w_out[l].astype(BF16), seq)
        h2, idx_t, gate_t = _peer_query_call(x2d, norm2_w[l].reshape(1, d), sc2, sh2,
                                             _peer_keyproj_call(peer_wq[l], peer_k1[l], peer_k2[l]), seq)
        uv = jnp.concatenate([peer_u[l].reshape(-1, SUBLANES, LANES), peer_v[l].reshape(-1, SUBLANES, LANES)], axis=1)
        x3 = _peer_gather_call(idx_t.T, gate_t.T[:, gate_perm], h2.reshape(n, SUBLANES, LANES),
                               x2d.reshape(n, SUBLANES, LANES), g2.reshape(bt, SUBLANES, LANES), uv, seq)
        x2d = x3.reshape(n, d)
    return _final_norm_call(x2d, final_norm_w.reshape(1, d)).reshape(bt, seq, d)
```

```python
import functools
import math

import jax
import jax.numpy as jnp
import numpy as np
from jax import lax
from jax.experimental import pallas as pl
from jax.experimental.pallas import tpu as pltpu

F32 = jnp.float32
BF16 = jnp.bfloat16
HIGHEST = lax.Precision.HIGHEST

D_MODEL = 1024
GDN_HEADS = 4
GDN_HEAD_DIM = 128
GDN_WIDTH = GDN_HEADS * GDN_HEAD_DIM
GDN_CONV = 4
GDN_CHUNK = 64
S5_GROUP = 16
S5_GROUPS = 16
S5_WIDTH = S5_GROUPS * S5_GROUP
S5_STATE = 64
MOBA_HEADS = 4
MOBA_HEAD_DIM = 64
MOBA_WIDTH = MOBA_HEADS * MOBA_HEAD_DIM
MOBA_BLOCK = 256
MOBA_TOPK = 3
MOBA_Q_CHUNK = 64
N_BRANCH = 3
PEER_HEADS = 8
PEER_NKEYS = 128
PEER_QDIM = 256
PEER_TOPK = 16
PEER_SEL = PEER_HEADS * PEER_TOPK
RMS_EPS = 1e-6
NEG_INF = -1e30

SUBLANES = 8
LANES = 128
VMEM_LIMIT_BYTES = 56 * 1024 * 1024

_OFF_QKV_A = 0
_OFF_Z_A = 3 * GDN_WIDTH
_OFF_BETA = _OFF_Z_A + GDN_WIDTH
_OFF_ALPHA = _OFF_BETA + GDN_HEADS
_OFF_UB = _OFF_ALPHA + GDN_HEADS
_OFF_QKV_C = _OFF_UB + S5_WIDTH
_OFF_GATE = _OFF_QKV_C + 3 * MOBA_WIDTH
_IN_COLS = _OFF_GATE + N_BRANCH * D_MODEL


def _cparams(*sem):
    return pltpu.CompilerParams(dimension_semantics=sem, vmem_limit_bytes=VMEM_LIMIT_BYTES)


def _resident(shape):
    nd = len(shape)
    return pl.BlockSpec(shape, lambda *_: (0,) * nd)


def _ada_kernel(c_ref, w_ref, b_ref, o_ref):
    c = c_ref[...]
    sc = c * jax.nn.sigmoid(c)
    o_ref[...] = jnp.dot(sc, w_ref[...], precision=HIGHEST, preferred_element_type=F32) + b_ref[...]


def _ada_call(c, ada_w, ada_b):
    depth, d, d6 = ada_w.shape
    bt = c.shape[0]
    nj = d6 // d
    return pl.pallas_call(
        _ada_kernel,
        grid=(depth, nj),
        in_specs=[
            pl.BlockSpec((bt, d), lambda l, j: (0, 0)),
            pl.BlockSpec((None, d, d), lambda l, j: (l, 0, j)),
            pl.BlockSpec((None, 1, d), lambda l, j: (l, 0, j)),
        ],
        out_specs=pl.BlockSpec((None, bt, d), lambda l, j: (l, 0, j)),
        out_shape=jax.ShapeDtypeStruct((depth, bt, d6), F32),
        compiler_params=_cparams("parallel", "parallel"),
        name="ada_mod",
    )(c, ada_w, ada_b.reshape(depth, 1, d6))


def _norm_mod(x, nw, sc, sh):
    y = x * lax.rsqrt(jnp.mean(x * x, axis=-1, keepdims=True) + RMS_EPS)
    return (y * nw) * (1.0 + sc) + sh


_IN_SPLITS = (4 * GDN_WIDTH, S5_WIDTH, 3 * MOBA_WIDTH, N_BRANCH * D_MODEL, LANES)
_IN_DTYPES = (F32, F32, F32, BF16, F32)


def _in_proj_kernel(x_ref, nw_ref, sc_ref, sh_ref, w_ref, oa_ref, ob_ref, oc_ref, og_ref, oba_ref):
    h = _norm_mod(x_ref[...], nw_ref[...], sc_ref[...], sh_ref[...]).astype(BF16)
    off = 0
    for o_ref, width in zip((oa_ref, ob_ref, oc_ref, og_ref, oba_ref), _IN_SPLITS):
        o_ref[...] = jnp.dot(h, w_ref[:, off:off + width], preferred_element_type=F32).astype(o_ref.dtype)
        off += width


def _in_proj_call(x2d, nw, sc, sh, w_r, seq, tm=512):
    n, d = x2d.shape
    per_b = seq // tm
    wcols = w_r.shape[1]
    outs = [jax.ShapeDtypeStruct((n, wd), dt) for wd, dt in zip(_IN_SPLITS, _IN_DTYPES)]
    return pl.pallas_call(
        _in_proj_kernel,
        grid=(n // tm,),
        in_specs=[
            pl.BlockSpec((tm, d), lambda i: (i, 0)),
            _resident((1, d)),
            pl.BlockSpec((None, 1, d), lambda i: (i // per_b, 0, 0)),
            pl.BlockSpec((None, 1, d), lambda i: (i // per_b, 0, 0)),
            _resident((d, wcols)),
        ],
        out_specs=[pl.BlockSpec((tm, wd), lambda i: (i, 0)) for wd in _IN_SPLITS],
        out_shape=outs,
        compiler_params=_cparams("parallel"),
        name="in_proj",
    )(x2d, nw, sc, sh, w_r)


def _arrange_w_in(w_in_l):
    pad = jnp.zeros((w_in_l.shape[0], LANES - 2 * GDN_HEADS), w_in_l.dtype)
    return jnp.concatenate([
        w_in_l[:, _OFF_QKV_A:_OFF_BETA],
        w_in_l[:, _OFF_UB:_OFF_QKV_C],
        w_in_l[:, _OFF_QKV_C:_OFF_GATE],
        w_in_l[:, _OFF_GATE:_IN_COLS],
        w_in_l[:, _OFF_BETA:_OFF_UB], pad,
    ], axis=1).astype(BF16)


def _merge_kernel(oa_ref, ob_ref, oc_ref, gt_ref, x_ref, g1_ref, wa_ref, wb_ref, wc_ref, wo_ref, o_ref):
    d = D_MODEL
    ya = jnp.dot(oa_ref[...].astype(BF16), wa_ref[...], preferred_element_type=F32)
    yb = jnp.dot(ob_ref[...].astype(BF16), wb_ref[...], preferred_element_type=F32)
    yc = jnp.dot(oc_ref[...].astype(BF16), wc_ref[...], preferred_element_type=F32)
    gate = lambda j: jax.nn.sigmoid(gt_ref[:, j * d:(j + 1) * d].astype(F32))
    merged = gate(0) * ya + gate(1) * yb + gate(2) * yc
    y = jnp.dot(merged.astype(BF16), wo_ref[...], preferred_element_type=F32)
    o_ref[...] = x_ref[...] + g1_ref[...] * y


def _merge_call(o_a, o_b, o_c, p_gate, x2d, g1, wa, wb, wc, wo, seq, tm=512):
    n, d = x2d.shape
    per_b = seq // tm
    row = lambda wd: pl.BlockSpec((tm, wd), lambda i: (i, 0))
    return pl.pallas_call(
        _merge_kernel,
        grid=(n // tm,),
        in_specs=[row(GDN_WIDTH), row(S5_WIDTH), row(MOBA_WIDTH), row(N_BRANCH * d), row(d),
                  pl.BlockSpec((None, 1, d), lambda i: (i // per_b, 0, 0)),
                  _resident(wa.shape), _resident(wb.shape), _resident(wc.shape), _resident(wo.shape)],
        out_specs=row(d),
        out_shape=jax.ShapeDtypeStruct((n, d), F32),
        compiler_params=_cparams("parallel"),
        name="merge_out",
    )(o_a, o_b, o_c, p_gate, x2d, g1, wa, wb, wc, wo)


def _candidate_tables():
    k = PEER_TOPK
    pairs = [(a, b) for a in range(k) for b in range(k) if (a + 1) * (b + 1) <= k]
    rows = -(-len(pairs) // SUBLANES) * SUBLANES
    sel = np.zeros((rows, 2 * k), np.float32)
    sel_id = np.zeros((rows, 2 * k), np.float32)
    bias = np.zeros((rows, 1), np.float32)
    order = np.zeros((rows, 1), np.float32)
    for r, (a, b) in enumerate(pairs):
        sel[r, a] = sel[r, k + b] = 1.0
        sel_id[r, a] = float(PEER_NKEYS)
        sel_id[r, k + b] = 1.0
        order[r, 0] = a * k + b
    for r in range(len(pairs), rows):
        bias[r, 0] = -np.inf
        order[r, 0] = k * k + r
    return sel, sel_id, bias, order


_CAND_SEL, _CAND_SEL_ID, _CAND_BIAS, _CAND_ORDER = _candidate_tables()
_CAND_ROWS = _CAND_SEL.shape[0]


def _extract_max(s, order, big, payload=None):
    m = jnp.max(s, axis=0, keepdims=True)
    pos = jnp.min(jnp.where(s == m, order, big), axis=0, keepdims=True)
    hit = order == pos
    tag = pos if payload is None else jnp.sum(jnp.where(hit, payload, 0.0), axis=0, keepdims=True)
    return m, tag, jnp.where(hit, -jnp.inf, s)


def _peer_keyproj_kernel(k_ref, wq_ref, o_ref):
    o_ref[...] = lax.dot_general(k_ref[...], wq_ref[...], (((1,), (1,)), ((), ())), precision=HIGHEST,
                                 preferred_element_type=F32).astype(o_ref.dtype)


def _peer_keyproj_call(wq, k1, k2):
    d = wq.shape[0]
    half = PEER_QDIM // 2
    keys = jnp.stack([k1, k2], axis=1).reshape(2 * PEER_HEADS, PEER_NKEYS, half)
    return pl.pallas_call(
        _peer_keyproj_kernel,
        grid=(2 * PEER_HEADS,),
        in_specs=[pl.BlockSpec((None, PEER_NKEYS, half), lambda g: (g, 0, 0)),
                  pl.BlockSpec((d, half), lambda g: (0, g))],
        out_specs=pl.BlockSpec((PEER_NKEYS, d), lambda g: (g, 0)),
        out_shape=jax.ShapeDtypeStruct((2 * PEER_HEADS * PEER_NKEYS, d), BF16),
        compiler_params=_cparams("parallel"),
        name="peer_keyproj",
    )(keys, wq)


def _peer_query_kernel(x_ref, nw_ref, sc_ref, sh_ref, wk_ref, sel_ref, selid_ref, cb_ref, co_ref,
                       h_ref, idx_ref, gate_ref, s_scr, v_scr, i_scr, cs_scr, ci_scr, ts_scr, te_scr):
    h = _norm_mod(x_ref[...], nw_ref[...], sc_ref[...], sh_ref[...])
    h_ref[...] = h
    s_scr[...] = lax.dot_general(wk_ref[...], h.astype(BF16), (((1,), (1,)), ((), ())),
                                 preferred_element_type=F32)
    nk = PEER_NKEYS
    k = PEER_TOPK
    key_iota = lax.broadcasted_iota(jnp.int32, (nk, s_scr.shape[1]), 0).astype(F32)
    cand_order = jnp.broadcast_to(co_ref[...], cs_scr.shape)

    for hd in range(PEER_HEADS + 1):
        do_keys, do_cand = hd < PEER_HEADS, hd > 0
        r1 = slice(2 * hd * nk, (2 * hd + 1) * nk)
        r2 = slice((2 * hd + 1) * nk, (2 * hd + 2) * nk)

        def trip(j, c, do_keys=do_keys, do_cand=do_cand, r1=r1, r2=r2):
            if do_keys:
                m1, p1, s1 = _extract_max(s_scr[r1, :], key_iota, float(nk))
                m2, p2, s2 = _extract_max(s_scr[r2, :], key_iota, float(nk))
                s_scr[r1, :] = s1
                s_scr[r2, :] = s2
                v_scr[pl.ds(j, 1), :] = m1
                i_scr[pl.ds(j, 1), :] = p1
                v_scr[pl.ds(j + k, 1), :] = m2
                i_scr[pl.ds(j + k, 1), :] = p2
            if do_cand:
                m, e, s = _extract_max(cs_scr[...], cand_order, float(4 * k * k), payload=ci_scr[...])
                cs_scr[...] = s
                ts_scr[pl.ds(j, 1), :] = m
                te_scr[pl.ds(j, 1), :] = e
            return c

        lax.fori_loop(0, k, trip, 0)
        if do_cand:
            ts = ts_scr[...]
            e = jnp.exp(ts - jnp.max(ts, axis=0, keepdims=True))
            out = (hd - 1) * k
            gate_ref[out:out + k, :] = e / jnp.sum(e, axis=0, keepdims=True)
            idx_ref[out:out + k, :] = te_scr[...].astype(jnp.int32)
        if do_keys:
            cs_scr[...] = (jnp.dot(sel_ref[...], v_scr[...], precision=HIGHEST, preferred_element_type=F32)
                           + cb_ref[...])
            ci_scr[...] = jnp.dot(selid_ref[...], i_scr[...].astype(BF16), preferred_element_type=F32)


def _peer_query_call(x2d, nw, sc, sh, wk, seq, tm=512):
    n, d = x2d.shape
    per_b = seq // tm
    k = PEER_TOPK
    scratch = [pltpu.VMEM((wk.shape[0], tm), F32),
               pltpu.VMEM((2 * k, tm), F32), pltpu.VMEM((2 * k, tm), F32),
               pltpu.VMEM((_CAND_ROWS, tm), F32), pltpu.VMEM((_CAND_ROWS, tm), F32),
               pltpu.VMEM((k, tm), F32), pltpu.VMEM((k, tm), F32)]
    cand = [jnp.asarray(_CAND_SEL), jnp.asarray(_CAND_SEL_ID, dtype=BF16), jnp.asarray(_CAND_BIAS),
            jnp.asarray(_CAND_ORDER)]
    return pl.pallas_call(
        _peer_query_kernel,
        grid=(n // tm,),
        in_specs=[
            pl.BlockSpec((tm, d), lambda i: (i, 0)),
            _resident((1, d)),
            pl.BlockSpec((None, 1, d), lambda i: (i // per_b, 0, 0)),
            pl.BlockSpec((None, 1, d), lambda i: (i // per_b, 0, 0)),
            _resident(wk.shape),
        ] + [_resident(t.shape) for t in cand],
        out_specs=[pl.BlockSpec((tm, d), lambda i: (i, 0)),
                   pl.BlockSpec((PEER_SEL, tm), lambda i: (0, i)),
                   pl.BlockSpec((PEER_SEL, tm), lambda i: (0, i))],
        out_shape=[jax.ShapeDtypeStruct((n, d), F32),
                   jax.ShapeDtypeStruct((PEER_SEL, n), jnp.int32),
                   jax.ShapeDtypeStruct((PEER_SEL, n), F32)],
        scratch_shapes=scratch,
        compiler_params=_cparams("parallel"),
        name="peer_query",
    )(x2d, nw, sc, sh, wk, *cand)


PEER_TOKENS_PER_STEP = 16
PEER_DMA_THREADS = 2
_ROW_SUB = 2 * SUBLANES


def _fold_pair(a, b, shift, keep_a):
    fa = a + pltpu.roll(a, shift, axis=0)
    fb = b + pltpu.roll(b, SUBLANES - shift, axis=0)
    return jnp.where(keep_a, fa, fb)


def _slot_order():
    tiles = [[k] * SUBLANES for k in range(PEER_SEL)]
    for shift, keep in ((4, [j < 4 for j in range(8)]), (2, [(j % 4) >= 2 for j in range(8)]),
                        (1, [(j % 2) == 1 for j in range(8)])):
        tiles = [[tiles[2 * m][j] if keep[j] else tiles[2 * m + 1][j] for j in range(8)]
                 for m in range(len(tiles) // 2)]
    return [s for t in tiles for s in t]


_SLOT_AT = _slot_order()
_POS_OF = [0] * PEER_SEL
for _p, _s in enumerate(_SLOT_AT):
    _POS_OF[_s] = _p


def _peer_gather_kernel(idx0_ref, idxc_ref, idxn_ref, g_ref, h_ref, x_ref, g2_ref, uv_hbm, o_ref,
                        buf0, buf1, wb_scr, sems):
    tb = PEER_TOKENS_PER_STEP
    blk = tb * PEER_SEL
    i = pl.program_id(0)
    bufs = (buf0, buf1)

    def row_copy(e, half, row, t):
        return pltpu.make_async_copy(uv_hbm.at[e], bufs[half].at[row], sems.at[half * tb + t])

    def slot_copy(half, t):
        return pltpu.make_async_copy(uv_hbm.at[pl.ds(0, PEER_SEL)], bufs[half].at[pl.ds(t * PEER_SEL, PEER_SEL)],
                                     sems.at[half * tb + t])

    @pl.when(i == 0)
    def _():
        def issue0(t, c):
            for k in range(PEER_SEL):
                row_copy(idx0_ref[0, t * PEER_SEL + k], 0, t * PEER_SEL + k, t).start()
            return c
        lax.fori_loop(0, tb, issue0, 0)

    sub = lax.broadcasted_iota(jnp.int32, (SUBLANES, LANES), 0)
    keep4 = sub < 4
    keep2 = (sub % 4) >= 2
    keep1 = (sub % 2) == 1
    eye = (lax.broadcasted_iota(jnp.int32, (PEER_SEL, LANES), 0)
           == lax.broadcasted_iota(jnp.int32, (PEER_SEL, LANES), 1))

    for half in range(2):
        nidx_ref, noff = (idxc_ref, blk) if half == 0 else (idxn_ref, 0)
        buf = bufs[half]
        for t in range(tb):
            tt = half * tb + t
            slot_copy(half, t).wait()
            for k in range(PEER_SEL):
                row_copy(nidx_ref[0, noff + t * PEER_SEL + k], 1 - half, t * PEER_SEL + k, t).start(
                    priority=k % PEER_DMA_THREADS)
            base = t * PEER_SEL
            ht = h_ref[tt]
            tiles = [buf[base + k, 0:SUBLANES, :] * ht for k in range(PEER_SEL)]
            tiles = [_fold_pair(tiles[2 * m], tiles[2 * m + 1], 4, keep4) for m in range(PEER_SEL // 2)]
            tiles = [_fold_pair(tiles[2 * m], tiles[2 * m + 1], 2, keep2) for m in range(PEER_SEL // 4)]
            tiles = [_fold_pair(tiles[2 * m], tiles[2 * m + 1], 1, keep1) for m in range(PEER_SEL // 8)]
            part = jnp.concatenate(tiles, axis=0)
            s = jnp.sum(part, axis=-1, keepdims=True)
            act = 0.5 * s * (1.0 + lax.erf(s * (2.0 ** -0.5)))
            grow = jnp.broadcast_to(g_ref[tt:tt + 1, :], (PEER_SEL, LANES))
            gcol = jnp.sum(jnp.where(eye, grow, 0.0), axis=-1, keepdims=True)
            wbase = tt * PEER_SEL
            wb_scr[wbase:wbase + PEER_SEL, :] = jnp.broadcast_to(gcol * act, (PEER_SEL, LANES))
            accs = [jnp.zeros((SUBLANES, LANES), F32) for _ in range(4)]
            for k in range(PEER_SEL):
                p = wbase + _POS_OF[k]
                wk = jnp.broadcast_to(wb_scr[p:p + 1, :], (SUBLANES, LANES))
                accs[k % 4] = accs[k % 4] + wk * buf[base + k, SUBLANES:_ROW_SUB, :]
            y = (accs[0] + accs[1]) + (accs[2] + accs[3])
            o_ref[tt] = x_ref[tt] + g2_ref[...] * y

    @pl.when(i == pl.num_programs(0) - 1)
    def _():
        for t in range(tb):
            slot_copy(0, t).wait()


def _peer_gather_call(idx, gate_p, h3, x3, g2_3, uv, seq):
    n = idx.shape[0]
    tb = PEER_TOKENS_PER_STEP
    ns = n // (2 * tb)
    per_b = seq // (2 * tb)
    idx3 = idx.reshape(ns, 1, 2 * tb * PEER_SEL)
    smem_blk = lambda fn: pl.BlockSpec((None, 1, 2 * tb * PEER_SEL), fn, memory_space=pltpu.SMEM)
    tok3 = pl.BlockSpec((2 * tb, SUBLANES, LANES), lambda i: (i, 0, 0))
    return pl.pallas_call(
        _peer_gather_kernel,
        grid=(ns,),
        in_specs=[
            smem_blk(lambda i: (0, 0, 0)),
            smem_blk(lambda i: (i, 0, 0)),
            smem_blk(lambda i: (jnp.minimum(i + 1, ns - 1), 0, 0)),
            pl.BlockSpec((2 * tb, PEER_SEL), lambda i: (i, 0)),
            tok3, tok3,
            pl.BlockSpec((None, SUBLANES, LANES), lambda i: (i // per_b, 0, 0)),
            pl.BlockSpec(memory_space=pl.ANY),
        ],
        out_specs=tok3,
        out_shape=jax.ShapeDtypeStruct((n, SUBLANES, LANES), F32),
        scratch_shapes=[pltpu.VMEM((tb * PEER_SEL, _ROW_SUB, LANES), F32),
                        pltpu.VMEM((tb * PEER_SEL, _ROW_SUB, LANES), F32),
                        pltpu.VMEM((2 * tb * PEER_SEL, LANES), F32),
                        pltpu.SemaphoreType.DMA((2 * tb,))],
        compiler_params=_cparams("arbitrary"),
        name="peer_gather",
    )(idx3, idx3, idx3, gate_p, h3, x3, g2_3, uv)


def _final_norm_kernel(x_ref, w_ref, o_ref):
    x = x_ref[...]
    o_ref[...] = x * lax.rsqrt(jnp.mean(x * x, axis=-1, keepdims=True) + RMS_EPS) * w_ref[...]


def _final_norm_call(x2d, w, tm=1024):
    n, d = x2d.shape
    return pl.pallas_call(
        _final_norm_kernel,
        grid=(n // tm,),
        in_specs=[pl.BlockSpec((tm, d), lambda i: (i, 0)), _resident((1, d))],
        out_specs=pl.BlockSpec((tm, d), lambda i: (i, 0)),
        out_shape=jax.ShapeDtypeStruct((n, d), F32),
        compiler_params=_cparams("parallel"),
        name="final_norm",
    )(x2d, w)


S5_LANES = S5_GROUPS * S5_STATE


def _s5_disc_kernel(are_ref, aim_ref, ldt_ref, bre_ref, bim_ref, ar_ref, ai_ref, br_ref, bi_ref):
    a_re = are_ref[...]
    a_im = aim_ref[...]
    dt = jnp.exp(ldt_ref[...])
    mag = jnp.exp(a_re * dt)
    ar = mag * jnp.cos(a_im * dt)
    ai = mag * jnp.sin(a_im * dt)
    den = a_re * a_re + a_im * a_im
    cr = ((ar - 1.0) * a_re + ai * a_im) / den
    ci = (ai * a_re - (ar - 1.0) * a_im) / den
    ar_ref[...] = ar
    ai_ref[...] = ai
    br_ref[...] = cr * bre_ref[...] - ci * bim_ref[...]
    bi_ref[...] = cr * bim_ref[...] + ci * bre_ref[...]


def _s5_disc_call(a_re, a_im, log_dt, b_re, b_im):
    rows = S5_LANES
    col = lambda t: t.reshape(rows, 1)
    ldt = jnp.broadcast_to(log_dt[:, None], (S5_GROUPS, S5_STATE))
    out = [jax.ShapeDtypeStruct((rows, 1), F32)] * 2 + [jax.ShapeDtypeStruct((rows, S5_GROUP), F32)] * 2
    return pl.pallas_call(_s5_disc_kernel, out_shape=out, name="s5_disc")(
        col(a_re), col(a_im), col(ldt), b_re.reshape(rows, S5_GROUP), b_im.reshape(rows, S5_GROUP))


def _s5_kernel(u_ref, bcat_ref, ar_ref, ai_ref, ccat_ref, d_ref, gw_ref, gb_ref, o_ref, st_scr, bu_scr, *, steps, bt):
    @pl.when(pl.program_id(0) == 0)
    def _():
        st_scr[...] = jnp.zeros_like(st_scr)

    u = u_ref[...]
    bu_scr[...] = jnp.dot(u.astype(BF16), bcat_ref[...], preferred_element_type=F32)
    a_re = jnp.broadcast_to(ar_ref[...], (bt, S5_LANES))
    a_im = jnp.broadcast_to(ai_ref[...], (bt, S5_LANES))

    def step(t, carry):
        s_re, s_im = carry
        r = pl.multiple_of(t * bt, bt)
        n_re = a_re * s_re - a_im * s_im + bu_scr[pl.ds(r, bt), 0:S5_LANES]
        n_im = a_re * s_im + a_im * s_re + bu_scr[pl.ds(r, bt), S5_LANES:2 * S5_LANES]
        bu_scr[pl.ds(r, bt), 0:S5_LANES] = n_re
        bu_scr[pl.ds(r, bt), S5_LANES:2 * S5_LANES] = n_im
        return n_re, n_im

    s_re, s_im = lax.fori_loop(0, steps, step, (st_scr[:, 0:S5_LANES], st_scr[:, S5_LANES:2 * S5_LANES]))
    st_scr[:, 0:S5_LANES] = s_re
    st_scr[:, S5_LANES:2 * S5_LANES] = s_im
    y = jnp.dot(bu_scr[...].astype(BF16), ccat_ref[...], preferred_element_type=F32) + d_ref[...] * u
    zg = 0.5 * y * (1.0 + lax.erf(y * (2.0 ** -0.5)))
    zz = jnp.dot(zg.astype(BF16), gw_ref[...], preferred_element_type=F32) + gb_ref[...]
    o_ref[...] = zz[:, 0:S5_WIDTH] * jax.nn.sigmoid(zz[:, S5_WIDTH:2 * S5_WIDTH])


def _s5_call(u_tb, bcat, abar_re, abar_im, ccat, d, glu_w, glu_b, bt, steps=128):
    rows = u_tb.shape[0]
    blk = steps * bt
    return pl.pallas_call(
        functools.partial(_s5_kernel, steps=steps, bt=bt),
        grid=(rows // blk,),
        in_specs=[pl.BlockSpec((blk, S5_WIDTH), lambda i: (i, 0)),
                  _resident(bcat.shape), _resident(abar_re.shape), _resident(abar_im.shape), _resident(ccat.shape),
                  _resident(d.shape), _resident(glu_w.shape), _resident(glu_b.shape)],
        out_specs=pl.BlockSpec((blk, S5_WIDTH), lambda i: (i, 0)),
        out_shape=jax.ShapeDtypeStruct((rows, S5_WIDTH), F32),
        scratch_shapes=[pltpu.VMEM((bt, 2 * S5_LANES), F32), pltpu.VMEM((blk, 2 * S5_LANES), F32)],
        compiler_params=_cparams("arbitrary"),
        name="s5_scan",
    )(u_tb, bcat, abar_re, abar_im, ccat, d, glu_w, glu_b)


def _s5_branch_pallas(p_ub, bt, seq, a_re, a_im, b_re, b_im, c_re, c_im, d, log_dt, glu_w, glu_b):
    ar, ai, br, bi = _s5_disc_call(a_re, a_im, log_dt, b_re, b_im)
    g_ix = jnp.arange(S5_GROUPS)
    def blockdiag_in(b):
        b = b.reshape(S5_GROUPS, S5_STATE, S5_GROUP)
        full = jnp.zeros((S5_GROUPS, S5_GROUP, S5_GROUPS, S5_STATE), F32)
        return full.at[g_ix, :, g_ix, :].set(b.transpose(0, 2, 1)).reshape(S5_WIDTH, S5_LANES)

    def blockdiag_out(c):
        full = jnp.zeros((S5_GROUPS, S5_STATE, S5_GROUPS, S5_GROUP), F32)
        return full.at[g_ix, :, g_ix, :].set(c.transpose(0, 2, 1)).reshape(S5_LANES, S5_WIDTH)

    bcat = jnp.concatenate([blockdiag_in(br), blockdiag_in(bi)], axis=1).astype(BF16)
    ccat = jnp.concatenate([blockdiag_out(c_re), -blockdiag_out(c_im)], axis=0).astype(BF16)
    u_tb = p_ub.reshape(bt, seq, S5_WIDTH).transpose(1, 0, 2).reshape(seq * bt, S5_WIDTH)
    o_tb = _s5_call(u_tb, bcat, ar.reshape(1, S5_LANES), ai.reshape(1, S5_LANES), ccat, d.reshape(1, S5_WIDTH),
                    glu_w.astype(BF16), glu_b.reshape(1, 2 * S5_WIDTH), bt)
    return o_tb.reshape(seq, bt, S5_WIDTH).transpose(1, 0, 2).reshape(bt * seq, S5_WIDTH)


def _mm(a, b):
    return jnp.dot(a.astype(BF16), b.astype(BF16), preferred_element_type=F32)


def _mm_nt(a, b, precision=None):
    if precision is None:
        a, b = a.astype(BF16), b.astype(BF16)
    return lax.dot_general(a, b, (((1,), (1,)), ((), ())), precision=precision, preferred_element_type=F32)


def _mm_tn(a, b):
    return lax.dot_general(a.astype(BF16), b.astype(BF16), (((0,), (0,)), ((), ())), preferred_element_type=F32)


def _softplus(x):
    return jnp.maximum(x, 0.0) + jnp.log1p(jnp.exp(-jnp.abs(x)))


def _gdn_kernel(pa_ref, ba_ref, cw_ref, alog_ref, dtb_ref, nw_ref, o_ref,
                st_scr, carry_scr, q_scr, k_scr, v_scr, g_scr, b_scr, u_scr, w_scr, qk_scr, gi_scr):
    j = pl.program_id(1)
    t_blk = pa_ref.shape[0]
    c_len, wid, dh = GDN_CHUNK, GDN_WIDTH, GDN_HEAD_DIM

    @pl.when(j == 0)
    def _():
        st_scr[...] = jnp.zeros_like(st_scr)
        carry_scr[...] = jnp.zeros_like(carry_scr)

    x = pa_ref[:, 0:3 * wid]
    xx = jnp.concatenate([carry_scr[...], x], axis=0)
    conv = cw_ref[GDN_CONV - 1:GDN_CONV, :] * x
    for k in range(1, GDN_CONV):
        conv = conv + cw_ref[GDN_CONV - 1 - k:GDN_CONV - k, :] * pltpu.roll(xx, k, axis=0)[SUBLANES:, :]
    carry_scr[...] = x[t_blk - SUBLANES:t_blk, :]
    act = conv * jax.nn.sigmoid(conv)
    for h in range(GDN_HEADS):
        qh = act[:, h * dh:(h + 1) * dh]
        kh = act[:, wid + h * dh:wid + (h + 1) * dh]
        q_scr[:, h * dh:(h + 1) * dh] = qh * lax.rsqrt(jnp.sum(qh * qh, axis=-1, keepdims=True) + 1e-6) * (dh ** -0.5)
        k_scr[:, h * dh:(h + 1) * dh] = kh * lax.rsqrt(jnp.sum(kh * kh, axis=-1, keepdims=True) + 1e-6)
    v_scr[...] = act[:, 2 * wid:3 * wid]
    ba = ba_ref[...]
    g_scr[...] = -jnp.exp(alog_ref[...]) * _softplus(ba + dtb_ref[...])
    b_scr[...] = jax.nn.sigmoid(ba)

    row = lax.broadcasted_iota(jnp.int32, (t_blk, t_blk), 0)
    col = lax.broadcasted_iota(jnp.int32, (t_blk, t_blk), 1)
    same = (row // c_len) == (col // c_len)
    tril = same & (row >= col)
    strict = same & (row > col)
    eye = jnp.where(row == col, 1.0, 0.0)
    lane = lax.broadcasted_iota(jnp.int32, (t_blk, LANES), 1)
    nw = nw_ref[...]
    gcum = jnp.dot(jnp.where(tril, 1.0, 0.0), g_scr[...], precision=HIGHEST, preferred_element_type=F32)
    beta_all = b_scr[...]
    heads = range(GDN_HEADS)
    hsl = [slice(h * dh, (h + 1) * dh) for h in heads]
    gis = [gcum[:, GDN_HEADS + h:GDN_HEADS + h + 1] for h in heads]
    betas = [beta_all[:, h:h + 1] for h in heads]
    decays, ps, ts = [], [], []
    for h in heads:
        g1 = jnp.where(lane == 0, gis[h], jnp.where(lane == 1, 1.0, 0.0))
        g2 = jnp.where(lane == 0, 1.0, jnp.where(lane == 1, -gis[h], 0.0))
        diff = _mm_nt(g1, g2, precision=HIGHEST)
        decays.append(jnp.where(tril, jnp.exp(jnp.where(tril, diff, 0.0)), 0.0))
    for h in heads:
        k_h = k_scr[:, hsl[h]]
        a = jnp.where(strict, _mm_nt(k_h * betas[h], k_h) * decays[h], 0.0)
        ps.append(-a)
        ts.append(eye - a)
    for _ in range(int(math.log2(c_len)) - 1):
        ps = [_mm(p, p) for p in ps]
        ts = [t + _mm(t, p) for t, p in zip(ts, ps)]
    for h in heads:
        k_h, v_h, q_h = k_scr[:, hsl[h]], v_scr[:, hsl[h]], q_scr[:, hsl[h]]
        e_g = jnp.exp(gis[h])
        u_scr[:, hsl[h]] = _mm(ts[h], v_h * betas[h])
        w_scr[:, hsl[h]] = _mm(ts[h], k_h * betas[h] * e_g)
        qk_scr[h] = jnp.where(tril, _mm_nt(q_h, k_h) * decays[h], 0.0)
        q_scr[:, hsl[h]] = q_h * e_g
        gi_scr[:, h:h + 1] = gis[h]

    for c in range(t_blk // c_len):
        rows = slice(c * c_len, (c + 1) * c_len)
        states = [st_scr[h] for h in heads]
        v_news = [u_scr[rows, hsl[h]] - _mm(w_scr[rows, hsl[h]], states[h]) for h in heads]
        o_state = [_mm(q_scr[rows, hsl[h]], states[h]) for h in heads]
        outs = [o_state[h] + _mm(qk_scr[h, rows, rows], v_news[h]) for h in heads]
        for h in heads:
            gi = gi_scr[rows, h:h + 1]
            g_last = gi[c_len - 1:c_len, :]
            st_scr[h] = states[h] * jnp.exp(g_last) + _mm_tn(k_scr[rows, hsl[h]] * jnp.exp(g_last - gi), v_news[h])
        for h in heads:
            o = outs[h]
            z = pa_ref[rows, 3 * wid + h * dh:3 * wid + (h + 1) * dh]
            o_ref[rows, hsl[h]] = (o * lax.rsqrt(jnp.mean(o * o, axis=-1, keepdims=True) + RMS_EPS) * nw
                                   * (z * jax.nn.sigmoid(z)))


def _gdn_branch_pallas(p_a, p_ba, conv_w, a_log, dt_bias, norm_w, bt, seq, t_blk=256):
    n = p_a.shape[0]
    nj = seq // t_blk
    wid = GDN_WIDTH
    lane_pad = lambda v: jnp.zeros((1, LANES), F32).at[0, GDN_HEADS:2 * GDN_HEADS].set(v)
    rows = lambda wd: pl.BlockSpec((t_blk, wd), lambda b, j: (b * nj + j, 0))
    return pl.pallas_call(
        _gdn_kernel,
        grid=(bt, nj),
        in_specs=[rows(4 * wid), rows(LANES), _resident((GDN_CONV, 3 * wid)), _resident((1, LANES)),
                  _resident((1, LANES)), _resident((1, GDN_HEAD_DIM))],
        out_specs=rows(wid),
        out_shape=jax.ShapeDtypeStruct((n, wid), F32),
        scratch_shapes=[pltpu.VMEM((GDN_HEADS, GDN_HEAD_DIM, GDN_HEAD_DIM), F32), pltpu.VMEM((SUBLANES, 3 * wid), F32),
                        pltpu.VMEM((t_blk, wid), F32), pltpu.VMEM((t_blk, wid), F32), pltpu.VMEM((t_blk, wid), F32),
                        pltpu.VMEM((t_blk, LANES), F32), pltpu.VMEM((t_blk, LANES), F32),
                        pltpu.VMEM((t_blk, wid), F32), pltpu.VMEM((t_blk, wid), F32),
                        pltpu.VMEM((GDN_HEADS, t_blk, t_blk), F32), pltpu.VMEM((t_blk, LANES), F32)],
        compiler_params=_cparams("parallel", "arbitrary"),
        name="gdn_delta",
    )(p_a, p_ba, conv_w, lane_pad(a_log), lane_pad(dt_bias), norm_w.reshape(1, GDN_HEAD_DIM))


def _moba_kernel(qt_ref, k_ref, vt_ref, o_ref, km_scr, sel_scr, qb_scr, *head_scr):
    i = pl.program_id(1)
    nh, nblk = k_ref.shape[0], k_ref.shape[1]
    scale = MOBA_HEAD_DIM ** -0.5
    m_scr, l_scr, acc_scr = head_scr[0:nh], head_scr[nh:2 * nh], head_scr[2 * nh:3 * nh]

    @pl.when(i == 0)
    def _():
        for h in range(nh):
            for n in range(nblk):
                km_scr[h, n:n + 1, :] = jnp.mean(k_ref[h, n], axis=0, keepdims=True)

    blk = lax.broadcasted_iota(jnp.int32, (nblk, MOBA_BLOCK), 0)
    kpos = lax.broadcasted_iota(jnp.int32, (MOBA_BLOCK, MOBA_BLOCK), 0)
    qpos = lax.broadcasted_iota(jnp.int32, (MOBA_BLOCK, MOBA_BLOCK), 1)
    for h in range(nh):
        qt = qt_ref[h]
        gate = jnp.dot(km_scr[h], qt, precision=HIGHEST, preferred_element_type=F32)
        cnt = jnp.zeros(gate.shape, F32)
        for m in range(nblk):
            gm = gate[m:m + 1, :]
            beats = (gm > gate) | ((gm == gate) & (m < blk))
            cnt = cnt + jnp.where(beats & (m < i), 1.0, 0.0)
        sel_scr[h] = jnp.where((blk < i) & (cnt < float(MOBA_TOPK)), 1.0, 0.0)
        qb = qt.astype(BF16)
        qb_scr[h] = qb
        s = jnp.dot(k_ref[h, i].astype(BF16), qb, preferred_element_type=F32) * scale
        s = jnp.where(kpos <= qpos, s, NEG_INF)
        m0 = jnp.max(s, axis=0, keepdims=True)
        p = jnp.exp(s - m0)
        m_scr[h][...] = m0
        l_scr[h][...] = jnp.sum(p, axis=0, keepdims=True)
        acc_scr[h][...] = jnp.dot(vt_ref[h, i].astype(BF16), p.astype(BF16), preferred_element_type=F32)

    def body(n, carry):
        scores = [jnp.dot(k_ref[h, n].astype(BF16), qb_scr[h], preferred_element_type=F32) for h in range(nh)]
        probs, alphas = [], []
        for h in range(nh):
            m = m_scr[h][...]
            s = jnp.where(sel_scr[h, pl.ds(n, 1), :] > 0.5, scores[h] * scale, NEG_INF)
            m_new = jnp.maximum(m, jnp.max(s, axis=0, keepdims=True))
            p = jnp.exp(s - m_new)
            alpha = jnp.exp(m - m_new)
            m_scr[h][...] = m_new
            l_scr[h][...] = alpha * l_scr[h][...] + jnp.sum(p, axis=0, keepdims=True)
            probs.append(p.astype(BF16))
            alphas.append(alpha)
        for h in range(nh):
            acc_scr[h][...] = alphas[h] * acc_scr[h][...] + jnp.dot(vt_ref[h, n].astype(BF16), probs[h],
                                                                    preferred_element_type=F32)
        return carry

    lax.fori_loop(0, i, body, 0)
    for h in range(nh):
        o_ref[h] = acc_scr[h][...] / l_scr[h][...]


def _moba_branch_pallas(p_c, bt, seq):
    nh, dh, bs = MOBA_HEADS, MOBA_HEAD_DIM, MOBA_BLOCK
    assert seq % bs == 0
    nblk = seq // bs
    qkv = p_c.reshape(bt, seq, 3, nh, dh)
    qt = qkv[:, :, 0].transpose(0, 2, 3, 1)
    k5 = qkv[:, :, 1].reshape(bt, nblk, bs, nh, dh).transpose(0, 3, 1, 2, 4)
    vt5 = qkv[:, :, 2].reshape(bt, nblk, bs, nh, dh).transpose(0, 3, 1, 4, 2)
    ot = pl.pallas_call(
        _moba_kernel,
        grid=(bt, nblk),
        in_specs=[pl.BlockSpec((None, nh, dh, bs), lambda b, i: (b, 0, 0, i)),
                  pl.BlockSpec((None, nh, nblk, bs, dh), lambda b, i: (b, 0, 0, 0, 0)),
                  pl.BlockSpec((None, nh, nblk, dh, bs), lambda b, i: (b, 0, 0, 0, 0))],
        out_specs=pl.BlockSpec((None, nh, dh, bs), lambda b, i: (b, 0, 0, i)),
        out_shape=jax.ShapeDtypeStruct((bt, nh, dh, seq), F32),
        scratch_shapes=[pltpu.VMEM((nh, nblk, dh), F32), pltpu.VMEM((nh, nblk, bs), F32),
                        pltpu.VMEM((nh, dh, bs), BF16)]
        + [pltpu.VMEM((1, bs), F32)] * (2 * nh) + [pltpu.VMEM((dh, bs), F32)] * nh,
        compiler_params=_cparams("parallel", "arbitrary"),
        name="moba_attn",
    )(qt, k5, vt5)
    return ot.transpose(0, 3, 1, 2).reshape(bt * seq, nh * dh)


def kernel(x, c, ada_w, ada_b, norm1_w, w_in, gdn_conv_w, gdn_a_log, gdn_dt_bias, gdn_norm_w, s5_a_re, s5_a_im, s5_b_re, s5_b_im, s5_c_re, s5_c_im, s5_d, s5_log_dt, s5_glu_w, s5_glu_b, w_branch_a, w_branch_b, w_branch_c, w_out, norm2_w, peer_wq, peer_k1, peer_k2, peer_u, peer_v, final_norm_w):
    bt, seq, d = x.shape
    n = bt * seq
    depth = ada_w.shape[0]
    x2d = x.reshape(n, d)
    mod = _ada_call(c, ada_w, ada_b)
    gate_perm = np.asarray(_SLOT_AT, np.int32)
    for l in range(depth):
        sh1, sc1, g1, sh2, sc2, g2 = (mod[l, :, j * d:(j + 1) * d].reshape(bt, 1, d) for j in range(6))
        p_a, p_ub, p_c, p_gate, p_ba = _in_proj_call(
            x2d, norm1_w[l].reshape(1, d), sc1, sh1, _arrange_w_in(w_in[l]), seq)
        o_a = _gdn_branch_pallas(p_a, p_ba, gdn_conv_w[l], gdn_a_log[l], gdn_dt_bias[l], gdn_norm_w[l], bt, seq)
        o_b = _s5_branch_pallas(p_ub, bt, seq, s5_a_re[l], s5_a_im[l], s5_b_re[l], s5_b_im[l], s5_c_re[l],
                                s5_c_im[l], s5_d[l], s5_log_dt[l], s5_glu_w[l], s5_glu_b[l])
        o_c = _moba_branch_pallas(p_c, bt, seq)
        x2d = _merge_call(o_a, o_b, o_c, p_gate, x2d, g1, w_branch_a[l].astype(BF16), w_branch_b[l].astype(BF16),
                          w_branch_c[l].astype(BF16), w_out[l].astype(BF16), seq)
        h2, idx_t, gate_t = _peer_query_call(x2d, norm2_w[l].reshape(1, d), sc2, sh2,
                                             _peer_keyproj_call(peer_wq[l], peer_k1[l], peer_k2[l]), seq)
        uv = jnp.concatenate([peer_u[l].reshape(-1, SUBLANES, LANES), peer_v[l].reshape(-1, SUBLANES, LANES)], axis=1)
        x3 = _peer_gather_call(idx_t.T, gate_t.T[:, gate_perm], h2.reshape(n, SUBLANES, LANES),
                               x2d.reshape(n, SUBLANES, LANES), g2.reshape(bt, SUBLANES, LANES), uv, seq)
        x2d = x3.reshape(n, d)
    return _final_norm_call(x2d, final_norm_w.reshape(1, d)).reshape(bt, seq, d)
```

```python
import functools
import math

import jax
import jax.numpy as jnp
import numpy as np
from jax import lax
from jax.experimental import pallas as pl
from jax.experimental.pallas import tpu as pltpu

F32 = jnp.float32
BF16 = jnp.bfloat16
HIGHEST = lax.Precision.HIGHEST

D_MODEL = 1024
GDN_HEADS = 4
GDN_HEAD_DIM = 128
GDN_WIDTH = GDN_HEADS * GDN_HEAD_DIM
GDN_CONV = 4
GDN_CHUNK = 64
S5_GROUP = 16
S5_GROUPS = 16
S5_WIDTH = S5_GROUPS * S5_GROUP
S5_STATE = 64
MOBA_HEADS = 4
MOBA_HEAD_DIM = 64
MOBA_WIDTH = MOBA_HEADS * MOBA_HEAD_DIM
MOBA_BLOCK = 256
MOBA_TOPK = 3
MOBA_Q_CHUNK = 64
N_BRANCH = 3
PEER_HEADS = 8
PEER_NKEYS = 128
PEER_QDIM = 256
PEER_TOPK = 16
PEER_SEL = PEER_HEADS * PEER_TOPK
RMS_EPS = 1e-6
NEG_INF = -1e30

SUBLANES = 8
LANES = 128
VMEM_LIMIT_BYTES = 56 * 1024 * 1024

_OFF_QKV_A = 0
_OFF_Z_A = 3 * GDN_WIDTH
_OFF_BETA = _OFF_Z_A + GDN_WIDTH
_OFF_ALPHA = _OFF_BETA + GDN_HEADS
_OFF_UB = _OFF_ALPHA + GDN_HEADS
_OFF_QKV_C = _OFF_UB + S5_WIDTH
_OFF_GATE = _OFF_QKV_C + 3 * MOBA_WIDTH
_IN_COLS = _OFF_GATE + N_BRANCH * D_MODEL


def _cparams(*sem):
    return pltpu.CompilerParams(dimension_semantics=sem, vmem_limit_bytes=VMEM_LIMIT_BYTES)


def _resident(shape):
    nd = len(shape)
    return pl.BlockSpec(shape, lambda *_: (0,) * nd)


def _ada_kernel(c_ref, w_ref, b_ref, o_ref):
    c = c_ref[...]
    sc = c * jax.nn.sigmoid(c)
    o_ref[...] = jnp.dot(sc, w_ref[...], precision=HIGHEST, preferred_element_type=F32) + b_ref[...]


def _ada_call(c, ada_w, ada_b):
    depth, d, d6 = ada_w.shape
    bt = c.shape[0]
    nj = d6 // d
    return pl.pallas_call(
        _ada_kernel,
        grid=(depth, nj),
        in_specs=[
            pl.BlockSpec((bt, d), lambda l, j: (0, 0)),
            pl.BlockSpec((None, d, d), lambda l, j: (l, 0, j)),
            pl.BlockSpec((None, 1, d), lambda l, j: (l, 0, j)),
        ],
        out_specs=pl.BlockSpec((None, bt, d), lambda l, j: (l, 0, j)),
        out_shape=jax.ShapeDtypeStruct((depth, bt, d6), F32),
        compiler_params=_cparams("parallel", "parallel"),
        name="ada_mod",
    )(c, ada_w, ada_b.reshape(depth, 1, d6))


def _norm_mod(x, nw, sc, sh):
    y = x * lax.rsqrt(jnp.mean(x * x, axis=-1, keepdims=True) + RMS_EPS)
    return (y * nw) * (1.0 + sc) + sh


_IN_SPLITS = (4 * GDN_WIDTH, S5_WIDTH, 3 * MOBA_WIDTH, N_BRANCH * D_MODEL, LANES)
_IN_DTYPES = (F32, F32, F32, BF16, F32)


def _in_proj_kernel(x_ref, nw_ref, sc_ref, sh_ref, w_ref, oa_ref, ob_ref, oc_ref, og_ref, oba_ref):
    h = _norm_mod(x_ref[...], nw_ref[...], sc_ref[...], sh_ref[...]).astype(BF16)
    off = 0
    for o_ref, width in zip((oa_ref, ob_ref, oc_ref, og_ref, oba_ref), _IN_SPLITS):
        o_ref[...] = jnp.dot(h, w_ref[:, off:off + width], preferred_element_type=F32).astype(o_ref.dtype)
        off += width


def _in_proj_call(x2d, nw, sc, sh, w_r, seq, tm=256):
    n, d = x2d.shape
    per_b = seq // tm
    wcols = w_r.shape[1]
    outs = [jax.ShapeDtypeStruct((n, wd), dt) for wd, dt in zip(_IN_SPLITS, _IN_DTYPES)]
    return pl.pallas_call(
        _in_proj_kernel,
        grid=(n // tm,),
        in_specs=[
            pl.BlockSpec((tm, d), lambda i: (i, 0)),
            _resident((1, d)),
            pl.BlockSpec((None, 1, d), lambda i: (i // per_b, 0, 0)),
            pl.BlockSpec((None, 1, d), lambda i: (i // per_b, 0, 0)),
            _resident((d, wcols)),
        ],
        out_specs=[pl.BlockSpec((tm, wd), lambda i: (i, 0)) for wd in _IN_SPLITS],
        out_shape=outs,
        compiler_params=_cparams("parallel"),
        name="in_proj",
    )(x2d, nw, sc, sh, w_r)


def _arrange_w_in(w_in_l):
    pad = jnp.zeros((w_in_l.shape[0], LANES - 2 * GDN_HEADS), w_in_l.dtype)
    return jnp.concatenate([
        w_in_l[:, _OFF_QKV_A:_OFF_BETA],
        w_in_l[:, _OFF_UB:_OFF_QKV_C],
        w_in_l[:, _OFF_QKV_C:_OFF_GATE],
        w_in_l[:, _OFF_GATE:_IN_COLS],
        w_in_l[:, _OFF_BETA:_OFF_UB], pad,
    ], axis=1).astype(BF16)


def _merge_kernel(oa_ref, ob_ref, oc_ref, gt_ref, x_ref, g1_ref, wa_ref, wb_ref, wc_ref, wo_ref, o_ref):
    d = D_MODEL
    ya = jnp.dot(oa_ref[...].astype(BF16), wa_ref[...], preferred_element_type=F32)
    yb = jnp.dot(ob_ref[...].astype(BF16), wb_ref[...], preferred_element_type=F32)
    yc = jnp.dot(oc_ref[...].astype(BF16), wc_ref[...], preferred_element_type=F32)
    gate = lambda j: jax.nn.sigmoid(gt_ref[:, j * d:(j + 1) * d].astype(F32))
    merged = gate(0) * ya + gate(1) * yb + gate(2) * yc
    y = jnp.dot(merged.astype(BF16), wo_ref[...], preferred_element_type=F32)
    o_ref[...] = x_ref[...] + g1_ref[...] * y


def _merge_call(o_a, o_b, o_c, p_gate, x2d, g1, wa, wb, wc, wo, seq, tm=512):
    n, d = x2d.shape
    per_b = seq // tm
    row = lambda wd: pl.BlockSpec((tm, wd), lambda i: (i, 0))
    return pl.pallas_call(
        _merge_kernel,
        grid=(n // tm,),
        in_specs=[row(GDN_WIDTH), row(S5_WIDTH), row(MOBA_WIDTH), row(N_BRANCH * d), row(d),
                  pl.BlockSpec((None, 1, d), lambda i: (i // per_b, 0, 0)),
                  _resident(wa.shape), _resident(wb.shape), _resident(wc.shape), _resident(wo.shape)],
        out_specs=row(d),
        out_shape=jax.ShapeDtypeStruct((n, d), F32),
        compiler_params=_cparams("parallel"),
        name="merge_out",
    )(o_a, o_b, o_c, p_gate, x2d, g1, wa, wb, wc, wo)


def _candidate_tables():
    k = PEER_TOPK
    pairs = [(a, b) for a in range(k) for b in range(k) if (a + 1) * (b + 1) <= k]
    rows = -(-len(pairs) // SUBLANES) * SUBLANES
    sel = np.zeros((rows, 2 * k), np.float32)
    sel_id = np.zeros((rows, 2 * k), np.float32)
    bias = np.zeros((rows, 1), np.float32)
    order = np.zeros((rows, 1), np.float32)
    for r, (a, b) in enumerate(pairs):
        sel[r, a] = sel[r, k + b] = 1.0
        sel_id[r, a] = float(PEER_NKEYS)
        sel_id[r, k + b] = 1.0
        order[r, 0] = a * k + b
    for r in range(len(pairs), rows):
        bias[r, 0] = -np.inf
        order[r, 0] = k * k + r
    return sel, sel_id, bias, order


_CAND_SEL, _CAND_SEL_ID, _CAND_BIAS, _CAND_ORDER = _candidate_tables()
_CAND_ROWS = _CAND_SEL.shape[0]


def _extract_max(s, order, big, payload=None):
    m = jnp.max(s, axis=0, keepdims=True)
    pos = jnp.min(jnp.where(s == m, order, big), axis=0, keepdims=True)
    hit = order == pos
    tag = pos if payload is None else jnp.sum(jnp.where(hit, payload, 0.0), axis=0, keepdims=True)
    return m, tag, jnp.where(hit, -jnp.inf, s)


def _peer_keyproj_kernel(k_ref, wq_ref, o_ref):
    o_ref[...] = lax.dot_general(k_ref[...], wq_ref[...], (((1,), (1,)), ((), ())), precision=HIGHEST,
                                 preferred_element_type=F32).astype(o_ref.dtype)


def _peer_keyproj_call(wq, k1, k2):
    d = wq.shape[0]
    half = PEER_QDIM // 2
    keys = jnp.stack([k1, k2], axis=1).reshape(2 * PEER_HEADS, PEER_NKEYS, half)
    return pl.pallas_call(
        _peer_keyproj_kernel,
        grid=(2 * PEER_HEADS,),
        in_specs=[pl.BlockSpec((None, PEER_NKEYS, half), lambda g: (g, 0, 0)),
                  pl.BlockSpec((d, half), lambda g: (0, g))],
        out_specs=pl.BlockSpec((PEER_NKEYS, d), lambda g: (g, 0)),
        out_shape=jax.ShapeDtypeStruct((2 * PEER_HEADS * PEER_NKEYS, d), BF16),
        compiler_params=_cparams("parallel"),
        name="peer_keyproj",
    )(keys, wq)


def _peer_query_kernel(x_ref, nw_ref, sc_ref, sh_ref, wk_ref, sel_ref, selid_ref, cb_ref, co_ref,
                       h_ref, idx_ref, gate_ref, s_scr, v_scr, i_scr, cs_scr, ci_scr, ts_scr, te_scr):
    h = _norm_mod(x_ref[...], nw_ref[...], sc_ref[...], sh_ref[...])
    h_ref[...] = h
    s_scr[...] = lax.dot_general(wk_ref[...], h.astype(BF16), (((1,), (1,)), ((), ())),
                                 preferred_element_type=F32)
    nk = PEER_NKEYS
    k = PEER_TOPK
    key_iota = lax.broadcasted_iota(jnp.int32, (nk, s_scr.shape[1]), 0).astype(F32)
    cand_order = jnp.broadcast_to(co_ref[...], cs_scr.shape)

    for hd in range(PEER_HEADS + 1):
        do_keys, do_cand = hd < PEER_HEADS, hd > 0
        r1 = slice(2 * hd * nk, (2 * hd + 1) * nk)
        r2 = slice((2 * hd + 1) * nk, (2 * hd + 2) * nk)

        def trip(j, c, do_keys=do_keys, do_cand=do_cand, r1=r1, r2=r2):
            if do_keys:
                m1, p1, s1 = _extract_max(s_scr[r1, :], key_iota, float(nk))
                m2, p2, s2 = _extract_max(s_scr[r2, :], key_iota, float(nk))
                s_scr[r1, :] = s1
                s_scr[r2, :] = s2
                v_scr[pl.ds(j, 1), :] = m1
                i_scr[pl.ds(j, 1), :] = p1
                v_scr[pl.ds(j + k, 1), :] = m2
                i_scr[pl.ds(j + k, 1), :] = p2
            if do_cand:
                m, e, s = _extract_max(cs_scr[...], cand_order, float(4 * k * k), payload=ci_scr[...])
                cs_scr[...] = s
                ts_scr[pl.ds(j, 1), :] = m
                te_scr[pl.ds(j, 1), :] = e
            return c

        lax.fori_loop(0, k, trip, 0)
        if do_cand:
            ts = ts_scr[...]
            e = jnp.exp(ts - jnp.max(ts, axis=0, keepdims=True))
            out = (hd - 1) * k
            gate_ref[out:out + k, :] = e / jnp.sum(e, axis=0, keepdims=True)
            idx_ref[out:out + k, :] = te_scr[...].astype(jnp.int32)
        if do_keys:
            cs_scr[...] = (jnp.dot(sel_ref[...], v_scr[...], precision=HIGHEST, preferred_element_type=F32)
                           + cb_ref[...])
            ci_scr[...] = jnp.dot(selid_ref[...], i_scr[...].astype(BF16), preferred_element_type=F32)


def _peer_query_call(x2d, nw, sc, sh, wk, seq, tm=512):
    n, d = x2d.shape
    per_b = seq // tm
    k = PEER_TOPK
    scratch = [pltpu.VMEM((wk.shape[0], tm), F32),
               pltpu.VMEM((2 * k, tm), F32), pltpu.VMEM((2 * k, tm), F32),
               pltpu.VMEM((_CAND_ROWS, tm), F32), pltpu.VMEM((_CAND_ROWS, tm), F32),
               pltpu.VMEM((k, tm), F32), pltpu.VMEM((k, tm), F32)]
    cand = [jnp.asarray(_CAND_SEL), jnp.asarray(_CAND_SEL_ID, dtype=BF16), jnp.asarray(_CAND_BIAS),
            jnp.asarray(_CAND_ORDER)]
    return pl.pallas_call(
        _peer_query_kernel,
        grid=(n // tm,),
        in_specs=[
            pl.BlockSpec((tm, d), lambda i: (i, 0)),
            _resident((1, d)),
            pl.BlockSpec((None, 1, d), lambda i: (i // per_b, 0, 0)),
            pl.BlockSpec((None, 1, d), lambda i: (i // per_b, 0, 0)),
            _resident(wk.shape),
        ] + [_resident(t.shape) for t in cand],
        out_specs=[pl.BlockSpec((tm, d), lambda i: (i, 0)),
                   pl.BlockSpec((PEER_SEL, tm), lambda i: (0, i)),
                   pl.BlockSpec((PEER_SEL, tm), lambda i: (0, i))],
        out_shape=[jax.ShapeDtypeStruct((n, d), F32),
                   jax.ShapeDtypeStruct((PEER_SEL, n), jnp.int32),
                   jax.ShapeDtypeStruct((PEER_SEL, n), F32)],
        scratch_shapes=scratch,
        compiler_params=_cparams("parallel"),
        name="peer_query",
    )(x2d, nw, sc, sh, wk, *cand)


PEER_TOKENS_PER_STEP = 8
PEER_DMA_THREADS = 2
_ROW_SUB = 2 * SUBLANES


def _fold_pair(a, b, shift, keep_a):
    fa = a + pltpu.roll(a, shift, axis=0)
    fb = b + pltpu.roll(b, SUBLANES - shift, axis=0)
    return jnp.where(keep_a, fa, fb)


def _slot_order():
    tiles = [[k] * SUBLANES for k in range(PEER_SEL)]
    for shift, keep in ((4, [j < 4 for j in range(8)]), (2, [(j % 4) >= 2 for j in range(8)]),
                        (1, [(j % 2) == 1 for j in range(8)])):
        tiles = [[tiles[2 * m][j] if keep[j] else tiles[2 * m + 1][j] for j in range(8)]
                 for m in range(len(tiles) // 2)]
    return [s for t in tiles for s in t]


_SLOT_AT = _slot_order()
_POS_OF = [0] * PEER_SEL
for _p, _s in enumerate(_SLOT_AT):
    _POS_OF[_s] = _p


def _peer_gather_kernel(idx0_ref, idxc_ref, idxn_ref, g_ref, h_ref, x_ref, g2_ref, uv_hbm, o_ref,
                        buf0, buf1, wb_scr, sems):
    tb = PEER_TOKENS_PER_STEP
    blk = tb * PEER_SEL
    i = pl.program_id(0)
    bufs = (buf0, buf1)

    def row_copy(e, half, row, t):
        return pltpu.make_async_copy(uv_hbm.at[e], bufs[half].at[row], sems.at[half * tb + t])

    def slot_copy(half, t):
        return pltpu.make_async_copy(uv_hbm.at[pl.ds(0, PEER_SEL)], bufs[half].at[pl.ds(t * PEER_SEL, PEER_SEL)],
                                     sems.at[half * tb + t])

    @pl.when(i == 0)
    def _():
        def issue0(t, c):
            for k in range(PEER_SEL):
                row_copy(idx0_ref[0, t * PEER_SEL + k], 0, t * PEER_SEL + k, t).start()
            return c
        lax.fori_loop(0, tb, issue0, 0)

    sub = lax.broadcasted_iota(jnp.int32, (SUBLANES, LANES), 0)
    keep4 = sub < 4
    keep2 = (sub % 4) >= 2
    keep1 = (sub % 2) == 1
    eye = (lax.broadcasted_iota(jnp.int32, (PEER_SEL, LANES), 0)
           == lax.broadcasted_iota(jnp.int32, (PEER_SEL, LANES), 1))

    for half in range(2):
        nidx_ref, noff = (idxc_ref, blk) if half == 0 else (idxn_ref, 0)
        buf = bufs[half]
        for t in range(tb):
            tt = half * tb + t
            slot_copy(half, t).wait()
            for k in range(PEER_SEL):
                row_copy(nidx_ref[0, noff + t * PEER_SEL + k], 1 - half, t * PEER_SEL + k, t).start(
                    priority=k % PEER_DMA_THREADS)
            base = t * PEER_SEL
            ht = h_ref[tt]
            tiles = [buf[base + k, 0:SUBLANES, :] * ht for k in range(PEER_SEL)]
            tiles = [_fold_pair(tiles[2 * m], tiles[2 * m + 1], 4, keep4) for m in range(PEER_SEL // 2)]
            tiles = [_fold_pair(tiles[2 * m], tiles[2 * m + 1], 2, keep2) for m in range(PEER_SEL // 4)]
            tiles = [_fold_pair(tiles[2 * m], tiles[2 * m + 1], 1, keep1) for m in range(PEER_SEL // 8)]
            part = jnp.concatenate(tiles, axis=0)
            s = jnp.sum(part, axis=-1, keepdims=True)
            act = 0.5 * s * (1.0 + lax.erf(s * (2.0 ** -0.5)))
            grow = jnp.broadcast_to(g_ref[tt:tt + 1, :], (PEER_SEL, LANES))
            gcol = jnp.sum(jnp.where(eye, grow, 0.0), axis=-1, keepdims=True)
            wbase = tt * PEER_SEL
            wb_scr[wbase:wbase + PEER_SEL, :] = jnp.broadcast_to(gcol * act, (PEER_SEL, LANES))
            accs = [jnp.zeros((SUBLANES, LANES), F32) for _ in range(4)]
            for k in range(PEER_SEL):
                p = wbase + _POS_OF[k]
                wk = jnp.broadcast_to(wb_scr[p:p + 1, :], (SUBLANES, LANES))
                accs[k % 4] = accs[k % 4] + wk * buf[base + k, SUBLANES:_ROW_SUB, :]
            y = (accs[0] + accs[1]) + (accs[2] + accs[3])
            o_ref[tt] = x_ref[tt] + g2_ref[...] * y

    @pl.when(i == pl.num_programs(0) - 1)
    def _():
        for t in range(tb):
            slot_copy(0, t).wait()


def _peer_gather_call(idx, gate_p, h3, x3, g2_3, uv, seq):
    n = idx.shape[0]
    tb = PEER_TOKENS_PER_STEP
    ns = n // (2 * tb)
    per_b = seq // (2 * tb)
    idx3 = idx.reshape(ns, 1, 2 * tb * PEER_SEL)
    smem_blk = lambda fn: pl.BlockSpec((None, 1, 2 * tb * PEER_SEL), fn, memory_space=pltpu.SMEM)
    tok3 = pl.BlockSpec((2 * tb, SUBLANES, LANES), lambda i: (i, 0, 0))
    return pl.pallas_call(
        _peer_gather_kernel,
        grid=(ns,),
        in_specs=[
            smem_blk(lambda i: (0, 0, 0)),
            smem_blk(lambda i: (i, 0, 0)),
            smem_blk(lambda i: (jnp.minimum(i + 1, ns - 1), 0, 0)),
            pl.BlockSpec((2 * tb, PEER_SEL), lambda i: (i, 0)),
            tok3, tok3,
            pl.BlockSpec((None, SUBLANES, LANES), lambda i: (i // per_b, 0, 0)),
            pl.BlockSpec(memory_space=pl.ANY),
        ],
        out_specs=tok3,
        out_shape=jax.ShapeDtypeStruct((n, SUBLANES, LANES), F32),
        scratch_shapes=[pltpu.VMEM((tb * PEER_SEL, _ROW_SUB, LANES), F32),
                        pltpu.VMEM((tb * PEER_SEL, _ROW_SUB, LANES), F32),
                        pltpu.VMEM((2 * tb * PEER_SEL, LANES), F32),
                        pltpu.SemaphoreType.DMA((2 * tb,))],
        compiler_params=_cparams("arbitrary"),
        name="peer_gather",
    )(idx3, idx3, idx3, gate_p, h3, x3, g2_3, uv)


def _uv_pack_kernel(u_ref, v_ref, o_ref):
    rows = u_ref.shape[0]
    o_ref[:, 0:SUBLANES, :] = u_ref[...].reshape(rows, SUBLANES, LANES)
    o_ref[:, SUBLANES:_ROW_SUB, :] = v_ref[...].reshape(rows, SUBLANES, LANES)


def _uv_pack_call(u_tab, v_tab, rows=512):
    n_exp, d = u_tab.shape
    assert d == SUBLANES * LANES
    return pl.pallas_call(
        _uv_pack_kernel,
        grid=(n_exp // rows,),
        in_specs=[pl.BlockSpec((rows, d), lambda i: (i, 0)), pl.BlockSpec((rows, d), lambda i: (i, 0))],
        out_specs=pl.BlockSpec((rows, _ROW_SUB, LANES), lambda i: (i, 0, 0)),
        out_shape=jax.ShapeDtypeStruct((n_exp, _ROW_SUB, LANES), F32),
        compiler_params=_cparams("parallel"),
        name="uv_pack",
    )(u_tab, v_tab)


def _final_norm_kernel(x_ref, w_ref, o_ref):
    x = x_ref[...]
    o_ref[...] = x * lax.rsqrt(jnp.mean(x * x, axis=-1, keepdims=True) + RMS_EPS) * w_ref[...]


def _final_norm_call(x2d, w, tm=1024):
    n, d = x2d.shape
    return pl.pallas_call(
        _final_norm_kernel,
        grid=(n // tm,),
        in_specs=[pl.BlockSpec((tm, d), lambda i: (i, 0)), _resident((1, d))],
        out_specs=pl.BlockSpec((tm, d), lambda i: (i, 0)),
        out_shape=jax.ShapeDtypeStruct((n, d), F32),
        compiler_params=_cparams("parallel"),
        name="final_norm",
    )(x2d, w)


S5_LANES = S5_GROUPS * S5_STATE


def _s5_disc_kernel(are_ref, aim_ref, ldt_ref, bre_ref, bim_ref, ar_ref, ai_ref, br_ref, bi_ref):
    a_re = are_ref[...]
    a_im = aim_ref[...]
    dt = jnp.exp(ldt_ref[...])
    mag = jnp.exp(a_re * dt)
    ar = mag * jnp.cos(a_im * dt)
    ai = mag * jnp.sin(a_im * dt)
    den = a_re * a_re + a_im * a_im
    cr = ((ar - 1.0) * a_re + ai * a_im) / den
    ci = (ai * a_re - (ar - 1.0) * a_im) / den
    ar_ref[...] = ar
    ai_ref[...] = ai
    br_ref[...] = cr * bre_ref[...] - ci * bim_ref[...]
    bi_ref[...] = cr * bim_ref[...] + ci * bre_ref[...]


def _s5_disc_call(a_re, a_im, log_dt, b_re, b_im):
    rows = S5_LANES
    col = lambda t: t.reshape(rows, 1)
    ldt = jnp.broadcast_to(log_dt[:, None], (S5_GROUPS, S5_STATE))
    out = [jax.ShapeDtypeStruct((rows, 1), F32)] * 2 + [jax.ShapeDtypeStruct((rows, S5_GROUP), F32)] * 2
    return pl.pallas_call(_s5_disc_kernel, out_shape=out, name="s5_disc")(
        col(a_re), col(a_im), col(ldt), b_re.reshape(rows, S5_GROUP), b_im.reshape(rows, S5_GROUP))


def _s5_kernel(u_ref, bcat_ref, ar_ref, ai_ref, ccat_ref, d_ref, gw_ref, gb_ref, o_ref, st_scr, bu_scr, *, steps, bt):
    @pl.when(pl.program_id(0) == 0)
    def _():
        st_scr[...] = jnp.zeros_like(st_scr)

    u = u_ref[...]
    bu_scr[...] = jnp.dot(u.astype(BF16), bcat_ref[...], preferred_element_type=F32)
    a_re = jnp.broadcast_to(ar_ref[...], (bt, S5_LANES))
    a_im = jnp.broadcast_to(ai_ref[...], (bt, S5_LANES))

    def step(t, carry):
        s_re, s_im = carry
        r = pl.multiple_of(t * bt, bt)
        n_re = a_re * s_re - a_im * s_im + bu_scr[pl.ds(r, bt), 0:S5_LANES]
        n_im = a_re * s_im + a_im * s_re + bu_scr[pl.ds(r, bt), S5_LANES:2 * S5_LANES]
        bu_scr[pl.ds(r, bt), 0:S5_LANES] = n_re
        bu_scr[pl.ds(r, bt), S5_LANES:2 * S5_LANES] = n_im
        return n_re, n_im

    s_re, s_im = lax.fori_loop(0, steps, step, (st_scr[:, 0:S5_LANES], st_scr[:, S5_LANES:2 * S5_LANES]))
    st_scr[:, 0:S5_LANES] = s_re
    st_scr[:, S5_LANES:2 * S5_LANES] = s_im
    y = jnp.dot(bu_scr[...].astype(BF16), ccat_ref[...], preferred_element_type=F32) + d_ref[...] * u
    zg = 0.5 * y * (1.0 + lax.erf(y * (2.0 ** -0.5)))
    zz = jnp.dot(zg.astype(BF16), gw_ref[...], preferred_element_type=F32) + gb_ref[...]
    o_ref[...] = zz[:, 0:S5_WIDTH] * jax.nn.sigmoid(zz[:, S5_WIDTH:2 * S5_WIDTH])


def _s5_call(u_tb, bcat, abar_re, abar_im, ccat, d, glu_w, glu_b, bt, steps=64):
    rows = u_tb.shape[0]
    blk = steps * bt
    return pl.pallas_call(
        functools.partial(_s5_kernel, steps=steps, bt=bt),
        grid=(rows // blk,),
        in_specs=[pl.BlockSpec((blk, S5_WIDTH), lambda i: (i, 0)),
                  _resident(bcat.shape), _resident(abar_re.shape), _resident(abar_im.shape), _resident(ccat.shape),
                  _resident(d.shape), _resident(glu_w.shape), _resident(glu_b.shape)],
        out_specs=pl.BlockSpec((blk, S5_WIDTH), lambda i: (i, 0)),
        out_shape=jax.ShapeDtypeStruct((rows, S5_WIDTH), F32),
        scratch_shapes=[pltpu.VMEM((bt, 2 * S5_LANES), F32), pltpu.VMEM((blk, 2 * S5_LANES), F32)],
        compiler_params=_cparams("arbitrary"),
        name="s5_scan",
    )(u_tb, bcat, abar_re, abar_im, ccat, d, glu_w, glu_b)


def _s5_branch_pallas(p_ub, bt, seq, a_re, a_im, b_re, b_im, c_re, c_im, d, log_dt, glu_w, glu_b):
    ar, ai, br, bi = _s5_disc_call(a_re, a_im, log_dt, b_re, b_im)
    g_ix = jnp.arange(S5_GROUPS)
    def blockdiag_in(b):
        b = b.reshape(S5_GROUPS, S5_STATE, S5_GROUP)
        full = jnp.zeros((S5_GROUPS, S5_GROUP, S5_GROUPS, S5_STATE), F32)
        return full.at[g_ix, :, g_ix, :].set(b.transpose(0, 2, 1)).reshape(S5_WIDTH, S5_LANES)

    def blockdiag_out(c):
        full = jnp.zeros((S5_GROUPS, S5_STATE, S5_GROUPS, S5_GROUP), F32)
        return full.at[g_ix, :, g_ix, :].set(c.transpose(0, 2, 1)).reshape(S5_LANES, S5_WIDTH)

    bcat = jnp.concatenate([blockdiag_in(br), blockdiag_in(bi)], axis=1).astype(BF16)
    ccat = jnp.concatenate([blockdiag_out(c_re), -blockdiag_out(c_im)], axis=0).astype(BF16)
    u_tb = p_ub.reshape(bt, seq, S5_WIDTH).transpose(1, 0, 2).reshape(seq * bt, S5_WIDTH)
    o_tb = _s5_call(u_tb, bcat, ar.reshape(1, S5_LANES), ai.reshape(1, S5_LANES), ccat, d.reshape(1, S5_WIDTH),
                    glu_w.astype(BF16), glu_b.reshape(1, 2 * S5_WIDTH), bt)
    return o_tb.reshape(seq, bt, S5_WIDTH).transpose(1, 0, 2).reshape(bt * seq, S5_WIDTH)


def _mm(a, b):
    return jnp.dot(a.astype(BF16), b.astype(BF16), preferred_element_type=F32)


def _mm_nt(a, b, precision=None):
    if precision is None:
        a, b = a.astype(BF16), b.astype(BF16)
    return lax.dot_general(a, b, (((1,), (1,)), ((), ())), precision=precision, preferred_element_type=F32)


def _mm_tn(a, b):
    return lax.dot_general(a.astype(BF16), b.astype(BF16), (((0,), (0,)), ((), ())), preferred_element_type=F32)


def _softplus(x):
    return jnp.maximum(x, 0.0) + jnp.log1p(jnp.exp(-jnp.abs(x)))


def _gdn_kernel(pa_ref, ba_ref, cw_ref, alog_ref, dtb_ref, nw_ref, o_ref,
                st_scr, carry_scr, q_scr, k_scr, v_scr, g_scr, b_scr, u_scr, w_scr, qk_scr, gi_scr):
    j = pl.program_id(1)
    t_blk = pa_ref.shape[0]
    c_len, wid, dh = GDN_CHUNK, GDN_WIDTH, GDN_HEAD_DIM

    @pl.when(j == 0)
    def _():
        st_scr[...] = jnp.zeros_like(st_scr)
        carry_scr[...] = jnp.zeros_like(carry_scr)

    x = pa_ref[:, 0:3 * wid]
    xx = jnp.concatenate([carry_scr[...], x], axis=0)
    conv = cw_ref[GDN_CONV - 1:GDN_CONV, :] * x
    for k in range(1, GDN_CONV):
        conv = conv + cw_ref[GDN_CONV - 1 - k:GDN_CONV - k, :] * pltpu.roll(xx, k, axis=0)[SUBLANES:, :]
    carry_scr[...] = x[t_blk - SUBLANES:t_blk, :]
    act = conv * jax.nn.sigmoid(conv)
    for h in range(GDN_HEADS):
        qh = act[:, h * dh:(h + 1) * dh]
        kh = act[:, wid + h * dh:wid + (h + 1) * dh]
        q_scr[:, h * dh:(h + 1) * dh] = qh * lax.rsqrt(jnp.sum(qh * qh, axis=-1, keepdims=True) + 1e-6) * (dh ** -0.5)
        k_scr[:, h * dh:(h + 1) * dh] = kh * lax.rsqrt(jnp.sum(kh * kh, axis=-1, keepdims=True) + 1e-6)
    v_scr[...] = act[:, 2 * wid:3 * wid]
    ba = ba_ref[...]
    g_scr[...] = -jnp.exp(alog_ref[...]) * _softplus(ba + dtb_ref[...])
    b_scr[...] = jax.nn.sigmoid(ba)

    row = lax.broadcasted_iota(jnp.int32, (t_blk, t_blk), 0)
    col = lax.broadcasted_iota(jnp.int32, (t_blk, t_blk), 1)
    same = (row // c_len) == (col // c_len)
    tril = same & (row >= col)
    strict = same & (row > col)
    eye = jnp.where(row == col, 1.0, 0.0)
    lane = lax.broadcasted_iota(jnp.int32, (t_blk, LANES), 1)
    nw = nw_ref[...]
    gcum = jnp.dot(jnp.where(tril, 1.0, 0.0), g_scr[...], precision=HIGHEST, preferred_element_type=F32)
    beta_all = b_scr[...]
    heads = range(GDN_HEADS)
    hsl = [slice(h * dh, (h + 1) * dh) for h in heads]
    gis = [gcum[:, GDN_HEADS + h:GDN_HEADS + h + 1] for h in heads]
    betas = [beta_all[:, h:h + 1] for h in heads]
    decays, ps, ts = [], [], []
    for h in heads:
        g1 = jnp.where(lane == 0, gis[h], jnp.where(lane == 1, 1.0, 0.0))
        g2 = jnp.where(lane == 0, 1.0, jnp.where(lane == 1, -gis[h], 0.0))
        diff = _mm_nt(g1, g2, precision=HIGHEST)
        decays.append(jnp.where(tril, jnp.exp(jnp.where(tril, diff, 0.0)), 0.0))
    for h in heads:
        k_h = k_scr[:, hsl[h]]
        a = jnp.where(strict, _mm_nt(k_h * betas[h], k_h) * decays[h], 0.0)
        ps.append(-a)
        ts.append(eye - a)
    for _ in range(int(math.log2(c_len)) - 1):
        ps = [_mm(p, p) for p in ps]
        ts = [t + _mm(t, p) for t, p in zip(ts, ps)]
    for h in heads:
        k_h, v_h, q_h = k_scr[:, hsl[h]], v_scr[:, hsl[h]], q_scr[:, hsl[h]]
        e_g = jnp.exp(gis[h])
        u_scr[:, hsl[h]] = _mm(ts[h], v_h * betas[h])
        w_scr[:, hsl[h]] = _mm(ts[h], k_h * betas[h] * e_g)
        qk_scr[h] = jnp.where(tril, _mm_nt(q_h, k_h) * decays[h], 0.0)
        q_scr[:, hsl[h]] = q_h * e_g
        gi_scr[:, h:h + 1] = gis[h]

    for c in range(t_blk // c_len):
        rows = slice(c * c_len, (c + 1) * c_len)
        states = [st_scr[h] for h in heads]
        v_news = [u_scr[rows, hsl[h]] - _mm(w_scr[rows, hsl[h]], states[h]) for h in heads]
        o_state = [_mm(q_scr[rows, hsl[h]], states[h]) for h in heads]
        outs = [o_state[h] + _mm(qk_scr[h, rows, rows], v_news[h]) for h in heads]
        for h in heads:
            gi = gi_scr[rows, h:h + 1]
            g_last = gi[c_len - 1:c_len, :]
            st_scr[h] = states[h] * jnp.exp(g_last) + _mm_tn(k_scr[rows, hsl[h]] * jnp.exp(g_last - gi), v_news[h])
        for h in heads:
            o = outs[h]
            z = pa_ref[rows, 3 * wid + h * dh:3 * wid + (h + 1) * dh]
            o_ref[rows, hsl[h]] = (o * lax.rsqrt(jnp.mean(o * o, axis=-1, keepdims=True) + RMS_EPS) * nw
                                   * (z * jax.nn.sigmoid(z)))


def _gdn_branch_pallas(p_a, p_ba, conv_w, a_log, dt_bias, norm_w, bt, seq, t_blk=256):
    n = p_a.shape[0]
    nj = seq // t_blk
    wid = GDN_WIDTH
    lane_pad = lambda v: jnp.zeros((1, LANES), F32).at[0, GDN_HEADS:2 * GDN_HEADS].set(v)
    rows = lambda wd: pl.BlockSpec((t_blk, wd), lambda b, j: (b * nj + j, 0))
    return pl.pallas_call(
        _gdn_kernel,
        grid=(bt, nj),
        in_specs=[rows(4 * wid), rows(LANES), _resident((GDN_CONV, 3 * wid)), _resident((1, LANES)),
                  _resident((1, LANES)), _resident((1, GDN_HEAD_DIM))],
        out_specs=rows(wid),
        out_shape=jax.ShapeDtypeStruct((n, wid), F32),
        scratch_shapes=[pltpu.VMEM((GDN_HEADS, GDN_HEAD_DIM, GDN_HEAD_DIM), F32), pltpu.VMEM((SUBLANES, 3 * wid), F32),
                        pltpu.VMEM((t_blk, wid), F32), pltpu.VMEM((t_blk, wid), F32), pltpu.VMEM((t_blk, wid), F32),
                        pltpu.VMEM((t_blk, LANES), F32), pltpu.VMEM((t_blk, LANES), F32),
                        pltpu.VMEM((t_blk, wid), F32), pltpu.VMEM((t_blk, wid), F32),
                        pltpu.VMEM((GDN_HEADS, t_blk, t_blk), F32), pltpu.VMEM((t_blk, LANES), F32)],
        compiler_params=_cparams("parallel", "arbitrary"),
        name="gdn_delta",
    )(p_a, p_ba, conv_w, lane_pad(a_log), lane_pad(dt_bias), norm_w.reshape(1, GDN_HEAD_DIM))


def _moba_kernel(qt_ref, k_ref, vt_ref, o_ref, km_scr, sel_scr, qb_scr, *head_scr):
    i = pl.program_id(1)
    nh, nblk = k_ref.shape[0], k_ref.shape[1]
    scale = MOBA_HEAD_DIM ** -0.5
    m_scr, l_scr, acc_scr = head_scr[0:nh], head_scr[nh:2 * nh], head_scr[2 * nh:3 * nh]

    @pl.when(i == 0)
    def _():
        for h in range(nh):
            for n in range(nblk):
                km_scr[h, n:n + 1, :] = jnp.mean(k_ref[h, n], axis=0, keepdims=True)

    blk = lax.broadcasted_iota(jnp.int32, (nblk, MOBA_BLOCK), 0)
    kpos = lax.broadcasted_iota(jnp.int32, (MOBA_BLOCK, MOBA_BLOCK), 0)
    qpos = lax.broadcasted_iota(jnp.int32, (MOBA_BLOCK, MOBA_BLOCK), 1)
    for h in range(nh):
        qt = qt_ref[h]
        gate = jnp.dot(km_scr[h], qt, precision=HIGHEST, preferred_element_type=F32)
        cnt = jnp.zeros(gate.shape, F32)
        for m in range(nblk):
            gm = gate[m:m + 1, :]
            beats = (gm > gate) | ((gm == gate) & (m < blk))
            cnt = cnt + jnp.where(beats & (m < i), 1.0, 0.0)
        sel_scr[h] = jnp.where((blk < i) & (cnt < float(MOBA_TOPK)), 1.0, 0.0)
        qb = qt.astype(BF16)
        qb_scr[h] = qb
        s = jnp.dot(k_ref[h, i].astype(BF16), qb, preferred_element_type=F32) * scale
        s = jnp.where(kpos <= qpos, s, NEG_INF)
        m0 = jnp.max(s, axis=0, keepdims=True)
        p = jnp.exp(s - m0)
        m_scr[h][...] = m0
        l_scr[h][...] = jnp.sum(p, axis=0, keepdims=True)
        acc_scr[h][...] = jnp.dot(vt_ref[h, i].astype(BF16), p.astype(BF16), preferred_element_type=F32)

    def body(n, carry):
        scores = [jnp.dot(k_ref[h, n].astype(BF16), qb_scr[h], preferred_element_type=F32) for h in range(nh)]
        probs, alphas = [], []
        for h in range(nh):
            m = m_scr[h][...]
            s = jnp.where(sel_scr[h, pl.ds(n, 1), :] > 0.5, scores[h] * scale, NEG_INF)
            m_new = jnp.maximum(m, jnp.max(s, axis=0, keepdims=True))
            p = jnp.exp(s - m_new)
            alpha = jnp.exp(m - m_new)
            m_scr[h][...] = m_new
            l_scr[h][...] = alpha * l_scr[h][...] + jnp.sum(p, axis=0, keepdims=True)
            probs.append(p.astype(BF16))
            alphas.append(alpha)
        for h in range(nh):
            acc_scr[h][...] = alphas[h] * acc_scr[h][...] + jnp.dot(vt_ref[h, n].astype(BF16), probs[h],
                                                                    preferred_element_type=F32)
        return carry

    lax.fori_loop(0, i, body, 0)
    for h in range(nh):
        o_ref[h] = acc_scr[h][...] / l_scr[h][...]


def _moba_branch_pallas(p_c, bt, seq):
    nh, dh, bs = MOBA_HEADS, MOBA_HEAD_DIM, MOBA_BLOCK
    assert seq % bs == 0
    nblk = seq // bs
    qkv = p_c.reshape(bt, seq, 3, nh, dh)
    qt = qkv[:, :, 0].transpose(0, 2, 3, 1)
    k5 = qkv[:, :, 1].reshape(bt, nblk, bs, nh, dh).transpose(0, 3, 1, 2, 4)
    vt5 = qkv[:, :, 2].reshape(bt, nblk, bs, nh, dh).transpose(0, 3, 1, 4, 2)
    ot = pl.pallas_call(
        _moba_kernel,
        grid=(bt, nblk),
        in_specs=[pl.BlockSpec((None, nh, dh, bs), lambda b, i: (b, 0, 0, i)),
                  pl.BlockSpec((None, nh, nblk, bs, dh), lambda b, i: (b, 0, 0, 0, 0)),
                  pl.BlockSpec((None, nh, nblk, dh, bs), lambda b, i: (b, 0, 0, 0, 0))],
        out_specs=pl.BlockSpec((None, nh, dh, bs), lambda b, i: (b, 0, 0, i)),
        out_shape=jax.ShapeDtypeStruct((bt, nh, dh, seq), F32),
        scratch_shapes=[pltpu.VMEM((nh, nblk, dh), F32), pltpu.VMEM((nh, nblk, bs), F32),
                        pltpu.VMEM((nh, dh, bs), BF16)]
        + [pltpu.VMEM((1, bs), F32)] * (2 * nh) + [pltpu.VMEM((dh, bs), F32)] * nh,
        compiler_params=_cparams("parallel", "arbitrary"),
        name="moba_attn",
    )(qt, k5, vt5)
    return ot.transpose(0, 3, 1, 2).reshape(bt * seq, nh * dh)


def kernel(x, c, ada_w, ada_b, norm1_w, w_in, gdn_conv_w, gdn_a_log, gdn_dt_bias, gdn_norm_w, s5_a_re, s5_a_im, s5_b_re, s5_b_im, s5_c_re, s5_c_im, s5_d, s5_log_dt, s5_glu_w, s5_glu_b, w_branch_a, w_branch_b, w_branch_c, w_out, norm2_w, peer_wq, peer_k1, peer_k2, peer_u, peer_v, final_norm_w):
    bt, seq, d = x.shape
    n = bt * seq
    depth = ada_w.shape[0]
    x2d = x.reshape(n, d)
    mod = _ada_call(c, ada_w, ada_b)
    gate_perm = np.asarray(_SLOT_AT, np.int32)
    for l in range(depth):
        sh1, sc1, g1, sh2, sc2, g2 = (mod[l, :, j * d:(j + 1) * d].reshape(bt, 1, d) for j in range(6))
        p_a, p_ub, p_c, p_gate, p_ba = _in_proj_call(
            x2d, norm1_w[l].reshape(1, d), sc1, sh1, _arrange_w_in(w_in[l]), seq)
        o_a = _gdn_branch_pallas(p_a, p_ba, gdn_conv_w[l], gdn_a_log[l], gdn_dt_bias[l], gdn_norm_w[l], bt, seq)
        o_b = _s5_branch_pallas(p_ub, bt, seq, s5_a_re[l], s5_a_im[l], s5_b_re[l], s5_b_im[l], s5_c_re[l],
                                s5_c_im[l], s5_d[l], s5_log_dt[l], s5_glu_w[l], s5_glu_b[l])
        o_c = _moba_branch_pallas(p_c, bt, seq)
        x2d = _merge_call(o_a, o_b, o_c, p_gate, x2d, g1, w_branch_a[l].astype(BF16), w_branch_b[l].astype(BF16),
                          w_branch_c[l].astype(BF16), w_out[l].astype(BF16), seq)
        h2, idx_t, gate_t = _peer_query_call(x2d, norm2_w[l].reshape(1, d), sc2, sh2,
                                             _peer_keyproj_call(peer_wq[l], peer_k1[l], peer_k2[l]), seq)
        uv = _uv_pack_call(peer_u[l], peer_v[l])
        x3 = _peer_gather_call(idx_t.T, gate_t.T[:, gate_perm], h2.reshape(n, SUBLANES, LANES),
                               x2d.reshape(n, SUBLANES, LANES), g2.reshape(bt, SUBLANES, LANES), uv, seq)
        x2d = x3.reshape(n, d)
    return _final_norm_call(x2d, final_norm_w.reshape(1, d)).reshape(bt, seq, d)
```

```python
import functools
import math

import jax
import jax.numpy as jnp
import numpy as np
from jax import lax
from jax.experimental import pallas as pl
from jax.experimental.pallas import tpu as pltpu

F32 = jnp.float32
BF16 = jnp.bfloat16
HIGHEST = lax.Precision.HIGHEST

D_MODEL = 1024
GDN_HEADS = 4
GDN_HEAD_DIM = 128
GDN_WIDTH = GDN_HEADS * GDN_HEAD_DIM
GDN_CONV = 4
GDN_CHUNK = 64
S5_GROUP = 16
S5_GROUPS = 16
S5_WIDTH = S5_GROUPS * S5_GROUP
S5_STATE = 64
MOBA_HEADS = 4
MOBA_HEAD_DIM = 64
MOBA_WIDTH = MOBA_HEADS * MOBA_HEAD_DIM
MOBA_BLOCK = 256
MOBA_TOPK = 3
MOBA_Q_CHUNK = 64
N_BRANCH = 3
PEER_HEADS = 8
PEER_NKEYS = 128
PEER_QDIM = 256
PEER_TOPK = 16
PEER_SEL = PEER_HEADS * PEER_TOPK
RMS_EPS = 1e-6
NEG_INF = -1e30

SUBLANES = 8
LANES = 128
VMEM_LIMIT_BYTES = 56 * 1024 * 1024

_OFF_QKV_A = 0
_OFF_Z_A = 3 * GDN_WIDTH
_OFF_BETA = _OFF_Z_A + GDN_WIDTH
_OFF_ALPHA = _OFF_BETA + GDN_HEADS
_OFF_UB = _OFF_ALPHA + GDN_HEADS
_OFF_QKV_C = _OFF_UB + S5_WIDTH
_OFF_GATE = _OFF_QKV_C + 3 * MOBA_WIDTH
_IN_COLS = _OFF_GATE + N_BRANCH * D_MODEL


def _cparams(*sem):
    return pltpu.CompilerParams(dimension_semantics=sem, vmem_limit_bytes=VMEM_LIMIT_BYTES)


def _resident(shape):
    nd = len(shape)
    return pl.BlockSpec(shape, lambda *_: (0,) * nd)


def _ada_kernel(c_ref, w_ref, b_ref, o_ref):
    c = c_ref[...]
    sc = c * jax.nn.sigmoid(c)
    o_ref[...] = jnp.dot(sc, w_ref[...], precision=HIGHEST, preferred_element_type=F32) + b_ref[...]


def _ada_call(c, ada_w, ada_b):
    depth, d, d6 = ada_w.shape
    bt = c.shape[0]
    nj = d6 // d
    return pl.pallas_call(
        _ada_kernel,
        grid=(depth, nj),
        in_specs=[
            pl.BlockSpec((bt, d), lambda l, j: (0, 0)),
            pl.BlockSpec((None, d, d), lambda l, j: (l, 0, j)),
            pl.BlockSpec((None, 1, d), lambda l, j: (l, 0, j)),
        ],
        out_specs=pl.BlockSpec((None, bt, d), lambda l, j: (l, 0, j)),
        out_shape=jax.ShapeDtypeStruct((depth, bt, d6), F32),
        compiler_params=_cparams("parallel", "parallel"),
        name="ada_mod",
    )(c, ada_w, ada_b.reshape(depth, 1, d6))


def _norm_mod(x, nw, sc, sh):
    y = x * lax.rsqrt(jnp.mean(x * x, axis=-1, keepdims=True) + RMS_EPS)
    return (y * nw) * (1.0 + sc) + sh


_IN_SPLITS = (4 * GDN_WIDTH, S5_WIDTH, 3 * MOBA_WIDTH, N_BRANCH * D_MODEL, LANES)
_IN_DTYPES = (F32, F32, F32, BF16, F32)


def _in_proj_kernel(x_ref, nw_ref, sc_ref, sh_ref, w_ref, oa_ref, ob_ref, oc_ref, og_ref, oba_ref):
    h = _norm_mod(x_ref[...], nw_ref[...], sc_ref[...], sh_ref[...]).astype(BF16)
    off = 0
    for o_ref, width in zip((oa_ref, ob_ref, oc_ref, og_ref, oba_ref), _IN_SPLITS):
        o_ref[...] = jnp.dot(h, w_ref[:, off:off + width], preferred_element_type=F32).astype(o_ref.dtype)
        off += width


def _in_proj_call(x2d, nw, sc, sh, w_r, seq, tm=256):
    n, d = x2d.shape
    per_b = seq // tm
    wcols = w_r.shape[1]
    outs = [jax.ShapeDtypeStruct((n, wd), dt) for wd, dt in zip(_IN_SPLITS, _IN_DTYPES)]
    return pl.pallas_call(
        _in_proj_kernel,
        grid=(n // tm,),
        in_specs=[
            pl.BlockSpec((tm, d), lambda i: (i, 0)),
            _resident((1, d)),
            pl.BlockSpec((None, 1, d), lambda i: (i // per_b, 0, 0)),
            pl.BlockSpec((None, 1, d), lambda i: (i // per_b, 0, 0)),
            _resident((d, wcols)),
        ],
        out_specs=[pl.BlockSpec((tm, wd), lambda i: (i, 0)) for wd in _IN_SPLITS],
        out_shape=outs,
        compiler_params=_cparams("parallel"),
        name="in_proj",
    )(x2d, nw, sc, sh, w_r)


def _arrange_w_in(w_in_l):
    pad = jnp.zeros((w_in_l.shape[0], LANES - 2 * GDN_HEADS), w_in_l.dtype)
    return jnp.concatenate([
        w_in_l[:, _OFF_QKV_A:_OFF_BETA],
        w_in_l[:, _OFF_UB:_OFF_QKV_C],
        w_in_l[:, _OFF_QKV_C:_OFF_GATE],
        w_in_l[:, _OFF_GATE:_IN_COLS],
        w_in_l[:, _OFF_BETA:_OFF_UB], pad,
    ], axis=1).astype(BF16)


def _merge_kernel(oa_ref, ob_ref, oc_ref, gt_ref, x_ref, g1_ref, wa_ref, wb_ref, wc_ref, wo_ref, o_ref):
    d = D_MODEL
    ya = jnp.dot(oa_ref[...].astype(BF16), wa_ref[...], preferred_element_type=F32)
    yb = jnp.dot(ob_ref[...].astype(BF16), wb_ref[...], preferred_element_type=F32)
    yc = jnp.dot(oc_ref[...].astype(BF16), wc_ref[...], preferred_element_type=F32)
    gate = lambda j: jax.nn.sigmoid(gt_ref[:, j * d:(j + 1) * d].astype(F32))
    merged = gate(0) * ya + gate(1) * yb + gate(2) * yc
    y = jnp.dot(merged.astype(BF16), wo_ref[...], preferred_element_type=F32)
    o_ref[...] = x_ref[...] + g1_ref[...] * y


def _merge_call(o_a, o_b, o_c, p_gate, x2d, g1, wa, wb, wc, wo, seq, tm=512):
    n, d = x2d.shape
    per_b = seq // tm
    row = lambda wd: pl.BlockSpec((tm, wd), lambda i: (i, 0))
    return pl.pallas_call(
        _merge_kernel,
        grid=(n // tm,),
        in_specs=[row(GDN_WIDTH), row(S5_WIDTH), row(MOBA_WIDTH), row(N_BRANCH * d), row(d),
                  pl.BlockSpec((None, 1, d), lambda i: (i // per_b, 0, 0)),
                  _resident(wa.shape), _resident(wb.shape), _resident(wc.shape), _resident(wo.shape)],
        out_specs=row(d),
        out_shape=jax.ShapeDtypeStruct((n, d), F32),
        compiler_params=_cparams("parallel"),
        name="merge_out",
    )(o_a, o_b, o_c, p_gate, x2d, g1, wa, wb, wc, wo)


def _candidate_tables():
    k = PEER_TOPK
    pairs = [(a, b) for a in range(k) for b in range(k) if (a + 1) * (b + 1) <= k]
    rows = -(-len(pairs) // SUBLANES) * SUBLANES
    sel = np.zeros((rows, 2 * k), np.float32)
    sel_id = np.zeros((rows, 2 * k), np.float32)
    bias = np.zeros((rows, 1), np.float32)
    order = np.zeros((rows, 1), np.float32)
    for r, (a, b) in enumerate(pairs):
        sel[r, a] = sel[r, k + b] = 1.0
        sel_id[r, a] = float(PEER_NKEYS)
        sel_id[r, k + b] = 1.0
        order[r, 0] = a * k + b
    for r in range(len(pairs), rows):
        bias[r, 0] = -np.inf
        order[r, 0] = k * k + r
    return sel, sel_id, bias, order


_CAND_SEL, _CAND_SEL_ID, _CAND_BIAS, _CAND_ORDER = _candidate_tables()
_CAND_ROWS = _CAND_SEL.shape[0]


def _extract_max(s, order, big, payload=None):
    m = jnp.max(s, axis=0, keepdims=True)
    pos = jnp.min(jnp.where(s == m, order, big), axis=0, keepdims=True)
    hit = order == pos
    tag = pos if payload is None else jnp.sum(jnp.where(hit, payload, 0.0), axis=0, keepdims=True)
    return m, tag, jnp.where(hit, -jnp.inf, s)


def _peer_keyproj_kernel(k_ref, wq_ref, o_ref):
    o_ref[...] = lax.dot_general(k_ref[...], wq_ref[...], (((1,), (1,)), ((), ())), precision=HIGHEST,
                                 preferred_element_type=F32).astype(o_ref.dtype)


def _peer_keyproj_call(wq, k1, k2):
    d = wq.shape[0]
    half = PEER_QDIM // 2
    keys = jnp.stack([k1, k2], axis=1).reshape(2 * PEER_HEADS, PEER_NKEYS, half)
    return pl.pallas_call(
        _peer_keyproj_kernel,
        grid=(2 * PEER_HEADS,),
        in_specs=[pl.BlockSpec((None, PEER_NKEYS, half), lambda g: (g, 0, 0)),
                  pl.BlockSpec((d, half), lambda g: (0, g))],
        out_specs=pl.BlockSpec((PEER_NKEYS, d), lambda g: (g, 0)),
        out_shape=jax.ShapeDtypeStruct((2 * PEER_HEADS * PEER_NKEYS, d), BF16),
        compiler_params=_cparams("parallel"),
        name="peer_keyproj",
    )(keys, wq)


def _peer_query_kernel(x_ref, nw_ref, sc_ref, sh_ref, wk_ref, sel_ref, selid_ref, cb_ref, co_ref,
                       h_ref, idx_ref, gate_ref, s_scr, v_scr, i_scr, cs_scr, ci_scr, ts_scr, te_scr):
    h = _norm_mod(x_ref[...], nw_ref[...], sc_ref[...], sh_ref[...])
    h_ref[...] = h
    s_scr[...] = lax.dot_general(wk_ref[...], h.astype(BF16), (((1,), (1,)), ((), ())),
                                 preferred_element_type=F32)
    nk = PEER_NKEYS
    k = PEER_TOPK
    key_iota = lax.broadcasted_iota(jnp.int32, (nk, s_scr.shape[1]), 0).astype(F32)
    cand_order = jnp.broadcast_to(co_ref[...], cs_scr.shape)

    for hd in range(PEER_HEADS + 1):
        do_keys, do_cand = hd < PEER_HEADS, hd > 0
        r1 = slice(2 * hd * nk, (2 * hd + 1) * nk)
        r2 = slice((2 * hd + 1) * nk, (2 * hd + 2) * nk)

        def trip(j, c, do_keys=do_keys, do_cand=do_cand, r1=r1, r2=r2):
            if do_keys:
                m1, p1, s1 = _extract_max(s_scr[r1, :], key_iota, float(nk))
                m2, p2, s2 = _extract_max(s_scr[r2, :], key_iota, float(nk))
                s_scr[r1, :] = s1
                s_scr[r2, :] = s2
                v_scr[pl.ds(j, 1), :] = m1
                i_scr[pl.ds(j, 1), :] = p1
                v_scr[pl.ds(j + k, 1), :] = m2
                i_scr[pl.ds(j + k, 1), :] = p2
            if do_cand:
                m, e, s = _extract_max(cs_scr[...], cand_order, float(4 * k * k), payload=ci_scr[...])
                cs_scr[...] = s
                ts_scr[pl.ds(j, 1), :] = m
                te_scr[pl.ds(j, 1), :] = e
            return c

        lax.fori_loop(0, k, trip, 0)
        if do_cand:
            ts = ts_scr[...]
            e = jnp.exp(ts - jnp.max(ts, axis=0, keepdims=True))
            out = (hd - 1) * k
            gate_ref[out:out + k, :] = e / jnp.sum(e, axis=0, keepdims=True)
            idx_ref[out:out + k, :] = te_scr[...].astype(jnp.int32)
        if do_keys:
            cs_scr[...] = (jnp.dot(sel_ref[...], v_scr[...], precision=HIGHEST, preferred_element_type=F32)
                           + cb_ref[...])
            ci_scr[...] = jnp.dot(selid_ref[...], i_scr[...].astype(BF16), preferred_element_type=F32)


def _peer_query_call(x2d, nw, sc, sh, wk, seq, tm=512):
    n, d = x2d.shape
    per_b = seq // tm
    k = PEER_TOPK
    scratch = [pltpu.VMEM((wk.shape[0], tm), F32),
               pltpu.VMEM((2 * k, tm), F32), pltpu.VMEM((2 * k, tm), F32),
               pltpu.VMEM((_CAND_ROWS, tm), F32), pltpu.VMEM((_CAND_ROWS, tm), F32),
               pltpu.VMEM((k, tm), F32), pltpu.VMEM((k, tm), F32)]
    cand = [jnp.asarray(_CAND_SEL), jnp.asarray(_CAND_SEL_ID, dtype=BF16), jnp.asarray(_CAND_BIAS),
            jnp.asarray(_CAND_ORDER)]
    return pl.pallas_call(
        _peer_query_kernel,
        grid=(n // tm,),
        in_specs=[
            pl.BlockSpec((tm, d), lambda i: (i, 0)),
            _resident((1, d)),
            pl.BlockSpec((None, 1, d), lambda i: (i // per_b, 0, 0)),
            pl.BlockSpec((None, 1, d), lambda i: (i // per_b, 0, 0)),
            _resident(wk.shape),
        ] + [_resident(t.shape) for t in cand],
        out_specs=[pl.BlockSpec((tm, d), lambda i: (i, 0)),
                   pl.BlockSpec((PEER_SEL, tm), lambda i: (0, i)),
                   pl.BlockSpec((PEER_SEL, tm), lambda i: (0, i))],
        out_shape=[jax.ShapeDtypeStruct((n, d), F32),
                   jax.ShapeDtypeStruct((PEER_SEL, n), jnp.int32),
                   jax.ShapeDtypeStruct((PEER_SEL, n), F32)],
        scratch_shapes=scratch,
        compiler_params=_cparams("parallel"),
        name="peer_query",
    )(x2d, nw, sc, sh, wk, *cand)


PEER_TOKENS_PER_STEP = 8
PEER_DMA_THREADS = 2
_ROW_SUB = 2 * SUBLANES


def _fold_pair(a, b, shift, keep_a):
    fa = a + pltpu.roll(a, shift, axis=0)
    fb = b + pltpu.roll(b, SUBLANES - shift, axis=0)
    return jnp.where(keep_a, fa, fb)


def _slot_order():
    tiles = [[k] * SUBLANES for k in range(PEER_SEL)]
    for shift, keep in ((4, [j < 4 for j in range(8)]), (2, [(j % 4) >= 2 for j in range(8)]),
                        (1, [(j % 2) == 1 for j in range(8)])):
        tiles = [[tiles[2 * m][j] if keep[j] else tiles[2 * m + 1][j] for j in range(8)]
                 for m in range(len(tiles) // 2)]
    return [s for t in tiles for s in t]


_SLOT_AT = _slot_order()
_POS_OF = [0] * PEER_SEL
for _p, _s in enumerate(_SLOT_AT):
    _POS_OF[_s] = _p


def _peer_gather_kernel(idx0_ref, idxc_ref, idxn_ref, g_ref, h_ref, x_ref, g2_ref, uv_hbm, o_ref,
                        buf0, buf1, wb_scr, sems):
    tb = PEER_TOKENS_PER_STEP
    blk = tb * PEER_SEL
    i = pl.program_id(0)
    bufs = (buf0, buf1)

    def row_copy(e, half, row, t):
        return pltpu.make_async_copy(uv_hbm.at[e], bufs[half].at[row], sems.at[half * tb + t])

    def slot_copy(half, t):
        return pltpu.make_async_copy(uv_hbm.at[pl.ds(0, PEER_SEL)], bufs[half].at[pl.ds(t * PEER_SEL, PEER_SEL)],
                                     sems.at[half * tb + t])

    @pl.when(i == 0)
    def _():
        def issue0(t, c):
            for k in range(PEER_SEL):
                row_copy(idx0_ref[0, t * PEER_SEL + k], 0, t * PEER_SEL + k, t).start()
            return c
        lax.fori_loop(0, tb, issue0, 0)

    sub = lax.broadcasted_iota(jnp.int32, (SUBLANES, LANES), 0)
    keep4 = sub < 4
    keep2 = (sub % 4) >= 2
    keep1 = (sub % 2) == 1
    eye = (lax.broadcasted_iota(jnp.int32, (PEER_SEL, LANES), 0)
           == lax.broadcasted_iota(jnp.int32, (PEER_SEL, LANES), 1))

    for half in range(2):
        nidx_ref, noff = (idxc_ref, blk) if half == 0 else (idxn_ref, 0)
        buf = bufs[half]
        for t in range(tb):
            tt = half * tb + t
            slot_copy(half, t).wait()
            for k in range(PEER_SEL):
                row_copy(nidx_ref[0, noff + t * PEER_SEL + k], 1 - half, t * PEER_SEL + k, t).start(
                    priority=k % PEER_DMA_THREADS)
            base = t * PEER_SEL
            ht = h_ref[tt]
            tiles = [buf[base + k, 0:SUBLANES, :] * ht for k in range(PEER_SEL)]
            tiles = [_fold_pair(tiles[2 * m], tiles[2 * m + 1], 4, keep4) for m in range(PEER_SEL // 2)]
            tiles = [_fold_pair(tiles[2 * m], tiles[2 * m + 1], 2, keep2) for m in range(PEER_SEL // 4)]
            tiles = [_fold_pair(tiles[2 * m], tiles[2 * m + 1], 1, keep1) for m in range(PEER_SEL // 8)]
            part = jnp.concatenate(tiles, axis=0)
            s = jnp.sum(part, axis=-1, keepdims=True)
            act = 0.5 * s * (1.0 + lax.erf(s * (2.0 ** -0.5)))
            grow = jnp.broadcast_to(g_ref[tt:tt + 1, :], (PEER_SEL, LANES))
            gcol = jnp.sum(jnp.where(eye, grow, 0.0), axis=-1, keepdims=True)
            wbase = tt * PEER_SEL
            wb_scr[wbase:wbase + PEER_SEL, :] = jnp.broadcast_to(gcol * act, (PEER_SEL, LANES))
            accs = [jnp.zeros((SUBLANES, LANES), F32) for _ in range(4)]
            for k in range(PEER_SEL):
                p = wbase + _POS_OF[k]
                wk = jnp.broadcast_to(wb_scr[p:p + 1, :], (SUBLANES, LANES))
                accs[k % 4] = accs[k % 4] + wk * buf[base + k, SUBLANES:_ROW_SUB, :]
            y = (accs[0] + accs[1]) + (accs[2] + accs[3])
            o_ref[tt] = x_ref[tt] + g2_ref[...] * y

    @pl.when(i == pl.num_programs(0) - 1)
    def _():
        for t in range(tb):
            slot_copy(0, t).wait()


def _peer_gather_call(idx, gate_p, h3, x3, g2_3, uv, seq):
    n = idx.shape[0]
    tb = PEER_TOKENS_PER_STEP
    ns = n // (2 * tb)
    per_b = seq // (2 * tb)
    idx3 = idx.reshape(ns, 1, 2 * tb * PEER_SEL)
    smem_blk = lambda fn: pl.BlockSpec((None, 1, 2 * tb * PEER_SEL), fn, memory_space=pltpu.SMEM)
    tok3 = pl.BlockSpec((2 * tb, SUBLANES, LANES), lambda i: (i, 0, 0))
    return pl.pallas_call(
        _peer_gather_kernel,
        grid=(ns,),
        in_specs=[
            smem_blk(lambda i: (0, 0, 0)),
            smem_blk(lambda i: (i, 0, 0)),
            smem_blk(lambda i: (jnp.minimum(i + 1, ns - 1), 0, 0)),
            pl.BlockSpec((2 * tb, PEER_SEL), lambda i: (i, 0)),
            tok3, tok3,
            pl.BlockSpec((None, SUBLANES, LANES), lambda i: (i // per_b, 0, 0)),
            pl.BlockSpec(memory_space=pl.ANY),
        ],
        out_specs=tok3,
        out_shape=jax.ShapeDtypeStruct((n, SUBLANES, LANES), F32),
        scratch_shapes=[pltpu.VMEM((tb * PEER_SEL, _ROW_SUB, LANES), F32),
                        pltpu.VMEM((tb * PEER_SEL, _ROW_SUB, LANES), F32),
                        pltpu.VMEM((2 * tb * PEER_SEL, LANES), F32),
                        pltpu.SemaphoreType.DMA((2 * tb,))],
        compiler_params=_cparams("arbitrary"),
        name="peer_gather",
    )(idx3, idx3, idx3, gate_p, h3, x3, g2_3, uv)


def _uv_pack_kernel(u_ref, v_ref, o_ref):
    rows = u_ref.shape[0]
    o_ref[:, 0:SUBLANES, :] = u_ref[...].reshape(rows, SUBLANES, LANES)
    o_ref[:, SUBLANES:_ROW_SUB, :] = v_ref[...].reshape(rows, SUBLANES, LANES)


def _uv_pack_call(u_tabs, v_tabs, layer, rows=512):
    _, n_exp, d = u_tabs.shape
    assert d == SUBLANES * LANES
    tab = pl.BlockSpec((None, rows, d), lambda i: (layer, i, 0))
    return pl.pallas_call(
        _uv_pack_kernel,
        grid=(n_exp // rows,),
        in_specs=[tab, tab],
        out_specs=pl.BlockSpec((rows, _ROW_SUB, LANES), lambda i: (i, 0, 0)),
        out_shape=jax.ShapeDtypeStruct((n_exp, _ROW_SUB, LANES), F32),
        compiler_params=_cparams("parallel"),
        name="uv_pack",
    )(u_tabs, v_tabs)


def _final_norm_kernel(x_ref, w_ref, o_ref):
    x = x_ref[...]
    o_ref[...] = x * lax.rsqrt(jnp.mean(x * x, axis=-1, keepdims=True) + RMS_EPS) * w_ref[...]


def _final_norm_call(x2d, w, tm=1024):
    n, d = x2d.shape
    return pl.pallas_call(
        _final_norm_kernel,
        grid=(n // tm,),
        in_specs=[pl.BlockSpec((tm, d), lambda i: (i, 0)), _resident((1, d))],
        out_specs=pl.BlockSpec((tm, d), lambda i: (i, 0)),
        out_shape=jax.ShapeDtypeStruct((n, d), F32),
        compiler_params=_cparams("parallel"),
        name="final_norm",
    )(x2d, w)


S5_LANES = S5_GROUPS * S5_STATE


def _s5_disc_kernel(are_ref, aim_ref, ldt_ref, bre_ref, bim_ref, ar_ref, ai_ref, br_ref, bi_ref):
    a_re = are_ref[...]
    a_im = aim_ref[...]
    dt = jnp.exp(ldt_ref[...])
    mag = jnp.exp(a_re * dt)
    ar = mag * jnp.cos(a_im * dt)
    ai = mag * jnp.sin(a_im * dt)
    den = a_re * a_re + a_im * a_im
    cr = ((ar - 1.0) * a_re + ai * a_im) / den
    ci = (ai * a_re - (ar - 1.0) * a_im) / den
    ar_ref[...] = ar
    ai_ref[...] = ai
    br_ref[...] = cr * bre_ref[...] - ci * bim_ref[...]
    bi_ref[...] = cr * bim_ref[...] + ci * bre_ref[...]


def _s5_disc_call(a_re, a_im, log_dt, b_re, b_im):
    rows = S5_LANES
    col = lambda t: t.reshape(rows, 1)
    ldt = jnp.broadcast_to(log_dt[:, None], (S5_GROUPS, S5_STATE))
    out = [jax.ShapeDtypeStruct((rows, 1), F32)] * 2 + [jax.ShapeDtypeStruct((rows, S5_GROUP), F32)] * 2
    return pl.pallas_call(_s5_disc_kernel, out_shape=out, name="s5_disc")(
        col(a_re), col(a_im), col(ldt), b_re.reshape(rows, S5_GROUP), b_im.reshape(rows, S5_GROUP))


def _s5_kernel(u_ref, bcat_ref, ar_ref, ai_ref, ccat_ref, d_ref, gw_ref, gb_ref, o_ref, st_scr, bu_scr, *, steps, bt):
    @pl.when(pl.program_id(0) == 0)
    def _():
        st_scr[...] = jnp.zeros_like(st_scr)

    u = u_ref[...]
    bu_scr[...] = jnp.dot(u.astype(BF16), bcat_ref[...], preferred_element_type=F32)
    a_re = jnp.broadcast_to(ar_ref[...], (bt, S5_LANES))
    a_im = jnp.broadcast_to(ai_ref[...], (bt, S5_LANES))

    def step(t, carry):
        s_re, s_im = carry
        r = pl.multiple_of(t * bt, bt)
        n_re = a_re * s_re - a_im * s_im + bu_scr[pl.ds(r, bt), 0:S5_LANES]
        n_im = a_re * s_im + a_im * s_re + bu_scr[pl.ds(r, bt), S5_LANES:2 * S5_LANES]
        bu_scr[pl.ds(r, bt), 0:S5_LANES] = n_re
        bu_scr[pl.ds(r, bt), S5_LANES:2 * S5_LANES] = n_im
        return n_re, n_im

    s_re, s_im = lax.fori_loop(0, steps, step, (st_scr[:, 0:S5_LANES], st_scr[:, S5_LANES:2 * S5_LANES]))
    st_scr[:, 0:S5_LANES] = s_re
    st_scr[:, S5_LANES:2 * S5_LANES] = s_im
    y = jnp.dot(bu_scr[...].astype(BF16), ccat_ref[...], preferred_element_type=F32) + d_ref[...] * u
    zg = 0.5 * y * (1.0 + lax.erf(y * (2.0 ** -0.5)))
    zz = jnp.dot(zg.astype(BF16), gw_ref[...], preferred_element_type=F32) + gb_ref[...]
    o_ref[...] = zz[:, 0:S5_WIDTH] * jax.nn.sigmoid(zz[:, S5_WIDTH:2 * S5_WIDTH])


def _s5_call(u_tb, bcat, abar_re, abar_im, ccat, d, glu_w, glu_b, bt, steps=64):
    rows = u_tb.shape[0]
    blk = steps * bt
    return pl.pallas_call(
        functools.partial(_s5_kernel, steps=steps, bt=bt),
        grid=(rows // blk,),
        in_specs=[pl.BlockSpec((blk, S5_WIDTH), lambda i: (i, 0)),
                  _resident(bcat.shape), _resident(abar_re.shape), _resident(abar_im.shape), _resident(ccat.shape),
                  _resident(d.shape), _resident(glu_w.shape), _resident(glu_b.shape)],
        out_specs=pl.BlockSpec((blk, S5_WIDTH), lambda i: (i, 0)),
        out_shape=jax.ShapeDtypeStruct((rows, S5_WIDTH), F32),
        scratch_shapes=[pltpu.VMEM((bt, 2 * S5_LANES), F32), pltpu.VMEM((blk, 2 * S5_LANES), F32)],
        compiler_params=_cparams("arbitrary"),
        name="s5_scan",
    )(u_tb, bcat, abar_re, abar_im, ccat, d, glu_w, glu_b)


def _s5_branch_pallas(p_ub, bt, seq, a_re, a_im, b_re, b_im, c_re, c_im, d, log_dt, glu_w, glu_b):
    ar, ai, br, bi = _s5_disc_call(a_re, a_im, log_dt, b_re, b_im)
    g_ix = jnp.arange(S5_GROUPS)
    def blockdiag_in(b):
        b = b.reshape(S5_GROUPS, S5_STATE, S5_GROUP)
        full = jnp.zeros((S5_GROUPS, S5_GROUP, S5_GROUPS, S5_STATE), F32)
        return full.at[g_ix, :, g_ix, :].set(b.transpose(0, 2, 1)).reshape(S5_WIDTH, S5_LANES)

    def blockdiag_out(c):
        full = jnp.zeros((S5_GROUPS, S5_STATE, S5_GROUPS, S5_GROUP), F32)
        return full.at[g_ix, :, g_ix, :].set(c.transpose(0, 2, 1)).reshape(S5_LANES, S5_WIDTH)

    bcat = jnp.concatenate([blockdiag_in(br), blockdiag_in(bi)], axis=1).astype(BF16)
    ccat = jnp.concatenate([blockdiag_out(c_re), -blockdiag_out(c_im)], axis=0).astype(BF16)
    u_tb = p_ub.reshape(bt, seq, S5_WIDTH).transpose(1, 0, 2).reshape(seq * bt, S5_WIDTH)
    o_tb = _s5_call(u_tb, bcat, ar.reshape(1, S5_LANES), ai.reshape(1, S5_LANES), ccat, d.reshape(1, S5_WIDTH),
                    glu_w.astype(BF16), glu_b.reshape(1, 2 * S5_WIDTH), bt)
    return o_tb.reshape(seq, bt, S5_WIDTH).transpose(1, 0, 2).reshape(bt * seq, S5_WIDTH)


def _mm(a, b):
    return jnp.dot(a.astype(BF16), b.astype(BF16), preferred_element_type=F32)


def _mm_nt(a, b, precision=None):
    if precision is None:
        a, b = a.astype(BF16), b.astype(BF16)
    return lax.dot_general(a, b, (((1,), (1,)), ((), ())), precision=precision, preferred_element_type=F32)


def _mm_tn(a, b):
    return lax.dot_general(a.astype(BF16), b.astype(BF16), (((0,), (0,)), ((), ())), preferred_element_type=F32)


def _softplus(x):
    return jnp.maximum(x, 0.0) + jnp.log1p(jnp.exp(-jnp.abs(x)))


def _gdn_kernel(pa_ref, ba_ref, cw_ref, alog_ref, dtb_ref, nw_ref, o_ref,
                st_scr, carry_scr, q_scr, k_scr, v_scr, g_scr, b_scr, u_scr, w_scr, qk_scr, gi_scr):
    j = pl.program_id(1)
    t_blk = pa_ref.shape[0]
    c_len, wid, dh = GDN_CHUNK, GDN_WIDTH, GDN_HEAD_DIM

    @pl.when(j == 0)
    def _():
        st_scr[...] = jnp.zeros_like(st_scr)
        carry_scr[...] = jnp.zeros_like(carry_scr)

    x = pa_ref[:, 0:3 * wid]
    xx = jnp.concatenate([carry_scr[...], x], axis=0)
    conv = cw_ref[GDN_CONV - 1:GDN_CONV, :] * x
    for k in range(1, GDN_CONV):
        conv = conv + cw_ref[GDN_CONV - 1 - k:GDN_CONV - k, :] * pltpu.roll(xx, k, axis=0)[SUBLANES:, :]
    carry_scr[...] = x[t_blk - SUBLANES:t_blk, :]
    act = conv * jax.nn.sigmoid(conv)
    for h in range(GDN_HEADS):
        qh = act[:, h * dh:(h + 1) * dh]
        kh = act[:, wid + h * dh:wid + (h + 1) * dh]
        q_scr[:, h * dh:(h + 1) * dh] = qh * lax.rsqrt(jnp.sum(qh * qh, axis=-1, keepdims=True) + 1e-6) * (dh ** -0.5)
        k_scr[:, h * dh:(h + 1) * dh] = kh * lax.rsqrt(jnp.sum(kh * kh, axis=-1, keepdims=True) + 1e-6)
    v_scr[...] = act[:, 2 * wid:3 * wid]
    ba = ba_ref[...]
    g_scr[...] = -jnp.exp(alog_ref[...]) * _softplus(ba + dtb_ref[...])
    b_scr[...] = jax.nn.sigmoid(ba)

    row = lax.broadcasted_iota(jnp.int32, (t_blk, t_blk), 0)
    col = lax.broadcasted_iota(jnp.int32, (t_blk, t_blk), 1)
    same = (row // c_len) == (col // c_len)
    tril = same & (row >= col)
    strict = same & (row > col)
    eye = jnp.where(row == col, 1.0, 0.0)
    lane = lax.broadcasted_iota(jnp.int32, (t_blk, LANES), 1)
    nw = nw_ref[...]
    gcum = jnp.dot(jnp.where(tril, 1.0, 0.0), g_scr[...], precision=HIGHEST, preferred_element_type=F32)
    beta_all = b_scr[...]
    heads = range(GDN_HEADS)
    hsl = [slice(h * dh, (h + 1) * dh) for h in heads]
    gis = [gcum[:, GDN_HEADS + h:GDN_HEADS + h + 1] for h in heads]
    betas = [beta_all[:, h:h + 1] for h in heads]
    decays, ps, ts = [], [], []
    for h in heads:
        g1 = jnp.where(lane == 0, gis[h], jnp.where(lane == 1, 1.0, 0.0))
        g2 = jnp.where(lane == 0, 1.0, jnp.where(lane == 1, -gis[h], 0.0))
        diff = _mm_nt(g1, g2, precision=HIGHEST)
        decays.append(jnp.where(tril, jnp.exp(jnp.where(tril, diff, 0.0)), 0.0))
    for h in heads:
        k_h = k_scr[:, hsl[h]]
        a = jnp.where(strict, _mm_nt(k_h * betas[h], k_h) * decays[h], 0.0)
        ps.append(-a)
        ts.append(eye - a)
    for _ in range(int(math.log2(c_len)) - 1):
        ps = [_mm(p, p) for p in ps]
        ts = [t + _mm(t, p) for t, p in zip(ts, ps)]
    for h in heads:
        k_h, v_h, q_h = k_scr[:, hsl[h]], v_scr[:, hsl[h]], q_scr[:, hsl[h]]
        e_g = jnp.exp(gis[h])
        u_scr[:, hsl[h]] = _mm(ts[h], v_h * betas[h])
        w_scr[:, hsl[h]] = _mm(ts[h], k_h * betas[h] * e_g)
        qk_scr[h] = jnp.where(tril, _mm_nt(q_h, k_h) * decays[h], 0.0)
        q_scr[:, hsl[h]] = q_h * e_g
        gi_scr[:, h:h + 1] = gis[h]

    for c in range(t_blk // c_len):
        rows = slice(c * c_len, (c + 1) * c_len)
        states = [st_scr[h] for h in heads]
        v_news = [u_scr[rows, hsl[h]] - _mm(w_scr[rows, hsl[h]], states[h]) for h in heads]
        o_state = [_mm(q_scr[rows, hsl[h]], states[h]) for h in heads]
        outs = [o_state[h] + _mm(qk_scr[h, rows, rows], v_news[h]) for h in heads]
        for h in heads:
            gi = gi_scr[rows, h:h + 1]
            g_last = gi[c_len - 1:c_len, :]
            st_scr[h] = states[h] * jnp.exp(g_last) + _mm_tn(k_scr[rows, hsl[h]] * jnp.exp(g_last - gi), v_news[h])
        for h in heads:
            o = outs[h]
            z = pa_ref[rows, 3 * wid + h * dh:3 * wid + (h + 1) * dh]
            o_ref[rows, hsl[h]] = (o * lax.rsqrt(jnp.mean(o * o, axis=-1, keepdims=True) + RMS_EPS) * nw
                                   * (z * jax.nn.sigmoid(z)))


def _gdn_branch_pallas(p_a, p_ba, conv_w, a_log, dt_bias, norm_w, bt, seq, t_blk=256):
    n = p_a.shape[0]
    nj = seq // t_blk
    wid = GDN_WIDTH
    lane_pad = lambda v: jnp.zeros((1, LANES), F32).at[0, GDN_HEADS:2 * GDN_HEADS].set(v)
    rows = lambda wd: pl.BlockSpec((t_blk, wd), lambda b, j: (b * nj + j, 0))
    return pl.pallas_call(
        _gdn_kernel,
        grid=(bt, nj),
        in_specs=[rows(4 * wid), rows(LANES), _resident((GDN_CONV, 3 * wid)), _resident((1, LANES)),
                  _resident((1, LANES)), _resident((1, GDN_HEAD_DIM))],
        out_specs=rows(wid),
        out_shape=jax.ShapeDtypeStruct((n, wid), F32),
        scratch_shapes=[pltpu.VMEM((GDN_HEADS, GDN_HEAD_DIM, GDN_HEAD_DIM), F32), pltpu.VMEM((SUBLANES, 3 * wid), F32),
                        pltpu.VMEM((t_blk, wid), F32), pltpu.VMEM((t_blk, wid), F32), pltpu.VMEM((t_blk, wid), F32),
                        pltpu.VMEM((t_blk, LANES), F32), pltpu.VMEM((t_blk, LANES), F32),
                        pltpu.VMEM((t_blk, wid), F32), pltpu.VMEM((t_blk, wid), F32),
                        pltpu.VMEM((GDN_HEADS, t_blk, t_blk), F32), pltpu.VMEM((t_blk, LANES), F32)],
        compiler_params=_cparams("parallel", "arbitrary"),
        name="gdn_delta",
    )(p_a, p_ba, conv_w, lane_pad(a_log), lane_pad(dt_bias), norm_w.reshape(1, GDN_HEAD_DIM))


def _moba_kernel(qt_ref, k_ref, vt_ref, o_ref, km_scr, sel_scr, qb_scr, *head_scr):
    i = pl.program_id(1)
    nh, nblk = k_ref.shape[0], k_ref.shape[1]
    scale = MOBA_HEAD_DIM ** -0.5
    m_scr, l_scr, acc_scr = head_scr[0:nh], head_scr[nh:2 * nh], head_scr[2 * nh:3 * nh]

    @pl.when(i == 0)
    def _():
        for h in range(nh):
            for n in range(nblk):
                km_scr[h, n:n + 1, :] = jnp.mean(k_ref[h, n], axis=0, keepdims=True)

    blk = lax.broadcasted_iota(jnp.int32, (nblk, MOBA_BLOCK), 0)
    kpos = lax.broadcasted_iota(jnp.int32, (MOBA_BLOCK, MOBA_BLOCK), 0)
    qpos = lax.broadcasted_iota(jnp.int32, (MOBA_BLOCK, MOBA_BLOCK), 1)
    for h in range(nh):
        qt = qt_ref[h]
        gate = jnp.dot(km_scr[h], qt, precision=HIGHEST, preferred_element_type=F32)
        cnt = jnp.zeros(gate.shape, F32)
        for m in range(nblk):
            gm = gate[m:m + 1, :]
            beats = (gm > gate) | ((gm == gate) & (m < blk))
            cnt = cnt + jnp.where(beats & (m < i), 1.0, 0.0)
        sel_scr[h] = jnp.where((blk < i) & (cnt < float(MOBA_TOPK)), 1.0, 0.0)
        qb = qt.astype(BF16)
        qb_scr[h] = qb
        s = jnp.dot(k_ref[h, i].astype(BF16), qb, preferred_element_type=F32) * scale
        s = jnp.where(kpos <= qpos, s, NEG_INF)
        m0 = jnp.max(s, axis=0, keepdims=True)
        p = jnp.exp(s - m0)
        m_scr[h][...] = m0
        l_scr[h][...] = jnp.sum(p, axis=0, keepdims=True)
        acc_scr[h][...] = jnp.dot(vt_ref[h, i].astype(BF16), p.astype(BF16), preferred_element_type=F32)

    def body(n, carry):
        scores = [jnp.dot(k_ref[h, n].astype(BF16), qb_scr[h], preferred_element_type=F32) for h in range(nh)]
        probs, alphas = [], []
        for h in range(nh):
            m = m_scr[h][...]
            s = jnp.where(sel_scr[h, pl.ds(n, 1), :] > 0.5, scores[h] * scale, NEG_INF)
            m_new = jnp.maximum(m, jnp.max(s, axis=0, keepdims=True))
            p = jnp.exp(s - m_new)
            alpha = jnp.exp(m - m_new)
            m_scr[h][...] = m_new
            l_scr[h][...] = alpha * l_scr[h][...] + jnp.sum(p, axis=0, keepdims=True)
            probs.append(p.astype(BF16))
            alphas.append(alpha)
        for h in range(nh):
            acc_scr[h][...] = alphas[h] * acc_scr[h][...] + jnp.dot(vt_ref[h, n].astype(BF16), probs[h],
                                                                    preferred_element_type=F32)
        return carry

    lax.fori_loop(0, i, body, 0)
    for h in range(nh):
        o_ref[h] = acc_scr[h][...] / l_scr[h][...]


def _moba_branch_pallas(p_c, bt, seq):
    nh, dh, bs = MOBA_HEADS, MOBA_HEAD_DIM, MOBA_BLOCK
    assert seq % bs == 0
    nblk = seq // bs
    qkv = p_c.reshape(bt, seq, 3, nh, dh)
    qt = qkv[:, :, 0].transpose(0, 2, 3, 1)
    k5 = qkv[:, :, 1].reshape(bt, nblk, bs, nh, dh).transpose(0, 3, 1, 2, 4)
    vt5 = qkv[:, :, 2].reshape(bt, nblk, bs, nh, dh).transpose(0, 3, 1, 4, 2)
    ot = pl.pallas_call(
        _moba_kernel,
        grid=(bt, nblk),
        in_specs=[pl.BlockSpec((None, nh, dh, bs), lambda b, i: (b, 0, 0, i)),
                  pl.BlockSpec((None, nh, nblk, bs, dh), lambda b, i: (b, 0, 0, 0, 0)),
                  pl.BlockSpec((None, nh, nblk, dh, bs), lambda b, i: (b, 0, 0, 0, 0))],
        out_specs=pl.BlockSpec((None, nh, dh, bs), lambda b, i: (b, 0, 0, i)),
        out_shape=jax.ShapeDtypeStruct((bt, nh, dh, seq), F32),
        scratch_shapes=[pltpu.VMEM((nh, nblk, dh), F32), pltpu.VMEM((nh, nblk, bs), F32),
                        pltpu.VMEM((nh, dh, bs), BF16)]
        + [pltpu.VMEM((1, bs), F32)] * (2 * nh) + [pltpu.VMEM((dh, bs), F32)] * nh,
        compiler_params=_cparams("parallel", "arbitrary"),
        name="moba_attn",
    )(qt, k5, vt5)
    return ot.transpose(0, 3, 1, 2).reshape(bt * seq, nh * dh)


def kernel(x, c, ada_w, ada_b, norm1_w, w_in, gdn_conv_w, gdn_a_log, gdn_dt_bias, gdn_norm_w, s5_a_re, s5_a_im, s5_b_re, s5_b_im, s5_c_re, s5_c_im, s5_d, s5_log_dt, s5_glu_w, s5_glu_b, w_branch_a, w_branch_b, w_branch_c, w_out, norm2_w, peer_wq, peer_k1, peer_k2, peer_u, peer_v, final_norm_w):
    bt, seq, d = x.shape
    n = bt * seq
    depth = ada_w.shape[0]
    x2d = x.reshape(n, d)
    mod = _ada_call(c, ada_w, ada_b)
    gate_perm = np.asarray(_SLOT_AT, np.int32)
    for l in range(depth):
        sh1, sc1, g1, sh2, sc2, g2 = (mod[l, :, j * d:(j + 1) * d].reshape(bt, 1, d) for j in range(6))
        p_a, p_ub, p_c, p_gate, p_ba = _in_proj_call(
            x2d, norm1_w[l].reshape(1, d), sc1, sh1, _arrange_w_in(w_in[l]), seq)
        o_a = _gdn_branch_pallas(p_a, p_ba, gdn_conv_w[l], gdn_a_log[l], gdn_dt_bias[l], gdn_norm_w[l], bt, seq)
        o_b = _s5_branch_pallas(p_ub, bt, seq, s5_a_re[l], s5_a_im[l], s5_b_re[l], s5_b_im[l], s5_c_re[l],
                                s5_c_im[l], s5_d[l], s5_log_dt[l], s5_glu_w[l], s5_glu_b[l])
        o_c = _moba_branch_pallas(p_c, bt, seq)
        x2d = _merge_call(o_a, o_b, o_c, p_gate, x2d, g1, w_branch_a[l].astype(BF16), w_branch_b[l].astype(BF16),
                          w_branch_c[l].astype(BF16), w_out[l].astype(BF16), seq)
        h2, idx_t, gate_t = _peer_query_call(x2d, norm2_w[l].reshape(1, d), sc2, sh2,
                                             _peer_keyproj_call(peer_wq[l], peer_k1[l], peer_k2[l]), seq)
        uv = _uv_pack_call(peer_u, peer_v, l)
        x3 = _peer_gather_call(idx_t.T, gate_t.T[:, gate_perm], h2.reshape(n, SUBLANES, LANES),
                               x2d.reshape(n, SUBLANES, LANES), g2.reshape(bt, SUBLANES, LANES), uv, seq)
        x2d = x3.reshape(n, d)
    return _final_norm_call(x2d, final_norm_w.reshape(1, d)).reshape(bt, seq, d)
```

```python
import functools
import math

import jax
import jax.numpy as jnp
import numpy as np
from jax import lax
from jax.experimental import pallas as pl
from jax.experimental.pallas import tpu as pltpu

F32 = jnp.float32
BF16 = jnp.bfloat16
HIGHEST = lax.Precision.HIGHEST

D_MODEL = 1024
GDN_HEADS = 4
GDN_HEAD_DIM = 128
GDN_WIDTH = GDN_HEADS * GDN_HEAD_DIM
GDN_CONV = 4
GDN_CHUNK = 64
S5_GROUP = 16
S5_GROUPS = 16
S5_WIDTH = S5_GROUPS * S5_GROUP
S5_STATE = 64
MOBA_HEADS = 4
MOBA_HEAD_DIM = 64
MOBA_WIDTH = MOBA_HEADS * MOBA_HEAD_DIM
MOBA_BLOCK = 256
MOBA_TOPK = 3
MOBA_Q_CHUNK = 64
N_BRANCH = 3
PEER_HEADS = 8
PEER_NKEYS = 128
PEER_QDIM = 256
PEER_TOPK = 16
PEER_SEL = PEER_HEADS * PEER_TOPK
RMS_EPS = 1e-6
NEG_INF = -1e30

SUBLANES = 8
LANES = 128
VMEM_LIMIT_BYTES = 56 * 1024 * 1024

_OFF_QKV_A = 0
_OFF_Z_A = 3 * GDN_WIDTH
_OFF_BETA = _OFF_Z_A + GDN_WIDTH
_OFF_ALPHA = _OFF_BETA + GDN_HEADS
_OFF_UB = _OFF_ALPHA + GDN_HEADS
_OFF_QKV_C = _OFF_UB + S5_WIDTH
_OFF_GATE = _OFF_QKV_C + 3 * MOBA_WIDTH
_IN_COLS = _OFF_GATE + N_BRANCH * D_MODEL


def _cparams(*sem):
    return pltpu.CompilerParams(dimension_semantics=sem, vmem_limit_bytes=VMEM_LIMIT_BYTES)


def _resident(shape):
    nd = len(shape)
    return pl.BlockSpec(shape, lambda *_: (0,) * nd)


def _ada_kernel(c_ref, w_ref, b_ref, o_ref):
    c = c_ref[...]
    sc = c * jax.nn.sigmoid(c)
    o_ref[...] = jnp.dot(sc, w_ref[...], precision=HIGHEST, preferred_element_type=F32) + b_ref[...]


def _ada_call(c, ada_w, ada_b):
    depth, d, d6 = ada_w.shape
    bt = c.shape[0]
    nj = d6 // d
    return pl.pallas_call(
        _ada_kernel,
        grid=(depth, nj),
        in_specs=[
            pl.BlockSpec((bt, d), lambda l, j: (0, 0)),
            pl.BlockSpec((None, d, d), lambda l, j: (l, 0, j)),
            pl.BlockSpec((None, 1, d), lambda l, j: (l, 0, j)),
        ],
        out_specs=pl.BlockSpec((None, bt, d), lambda l, j: (l, 0, j)),
        out_shape=jax.ShapeDtypeStruct((depth, bt, d6), F32),
        compiler_params=_cparams("parallel", "parallel"),
        name="ada_mod",
    )(c, ada_w, ada_b.reshape(depth, 1, d6))


def _norm_mod(x, nw, sc, sh):
    y = x * lax.rsqrt(jnp.mean(x * x, axis=-1, keepdims=True) + RMS_EPS)
    return (y * nw) * (1.0 + sc) + sh


_IN_SPLITS = (4 * GDN_WIDTH, S5_WIDTH, 3 * MOBA_WIDTH, N_BRANCH * D_MODEL, LANES)
_IN_DTYPES = (F32, F32, F32, BF16, F32)


def _in_proj_kernel(x_ref, nw_ref, sc_ref, sh_ref, w_ref, oa_ref, ob_ref, oc_ref, og_ref, oba_ref):
    h = _norm_mod(x_ref[...], nw_ref[...], sc_ref[...], sh_ref[...]).astype(BF16)
    off = 0
    for o_ref, width in zip((oa_ref, ob_ref, oc_ref, og_ref, oba_ref), _IN_SPLITS):
        o_ref[...] = jnp.dot(h, w_ref[:, off:off + width], preferred_element_type=F32).astype(o_ref.dtype)
        off += width


def _in_proj_call(x2d, nw, sc, sh, w_r, seq, tm=256):
    n, d = x2d.shape
    per_b = seq // tm
    wcols = w_r.shape[1]
    outs = [jax.ShapeDtypeStruct((n, wd), dt) for wd, dt in zip(_IN_SPLITS, _IN_DTYPES)]
    return pl.pallas_call(
        _in_proj_kernel,
        grid=(n // tm,),
        in_specs=[
            pl.BlockSpec((tm, d), lambda i: (i, 0)),
            _resident((1, d)),
            pl.BlockSpec((None, 1, d), lambda i: (i // per_b, 0, 0)),
            pl.BlockSpec((None, 1, d), lambda i: (i // per_b, 0, 0)),
            _resident((d, wcols)),
        ],
        out_specs=[pl.BlockSpec((tm, wd), lambda i: (i, 0)) for wd in _IN_SPLITS],
        out_shape=outs,
        compiler_params=_cparams("parallel"),
        name="in_proj",
    )(x2d, nw, sc, sh, w_r)


def _arrange_w_in(w_in_l):
    pad = jnp.zeros((w_in_l.shape[0], LANES - 2 * GDN_HEADS), w_in_l.dtype)
    return jnp.concatenate([
        w_in_l[:, _OFF_QKV_A:_OFF_BETA],
        w_in_l[:, _OFF_UB:_OFF_QKV_C],
        w_in_l[:, _OFF_QKV_C:_OFF_GATE],
        w_in_l[:, _OFF_GATE:_IN_COLS],
        w_in_l[:, _OFF_BETA:_OFF_UB], pad,
    ], axis=1).astype(BF16)


def _merge_kernel(oa_ref, ob_ref, oc_ref, gt_ref, x_ref, g1_ref, wa_ref, wb_ref, wc_ref, wo_ref, o_ref):
    d = D_MODEL
    ya = jnp.dot(oa_ref[...].astype(BF16), wa_ref[...], preferred_element_type=F32)
    yb = jnp.dot(ob_ref[...].astype(BF16), wb_ref[...], preferred_element_type=F32)
    yc = jnp.dot(oc_ref[...].astype(BF16), wc_ref[...], preferred_element_type=F32)
    gate = lambda j: jax.nn.sigmoid(gt_ref[:, j * d:(j + 1) * d].astype(F32))
    merged = gate(0) * ya + gate(1) * yb + gate(2) * yc
    y = jnp.dot(merged.astype(BF16), wo_ref[...], preferred_element_type=F32)
    o_ref[...] = x_ref[...] + g1_ref[...] * y


def _merge_call(o_a, o_b, o_c, p_gate, x2d, g1, wa, wb, wc, wo, seq, tm=512):
    n, d = x2d.shape
    per_b = seq // tm
    row = lambda wd: pl.BlockSpec((tm, wd), lambda i: (i, 0))
    return pl.pallas_call(
        _merge_kernel,
        grid=(n // tm,),
        in_specs=[row(GDN_WIDTH), row(S5_WIDTH), row(MOBA_WIDTH), row(N_BRANCH * d), row(d),
                  pl.BlockSpec((None, 1, d), lambda i: (i // per_b, 0, 0)),
                  _resident(wa.shape), _resident(wb.shape), _resident(wc.shape), _resident(wo.shape)],
        out_specs=row(d),
        out_shape=jax.ShapeDtypeStruct((n, d), F32),
        compiler_params=_cparams("parallel"),
        name="merge_out",
    )(o_a, o_b, o_c, p_gate, x2d, g1, wa, wb, wc, wo)


def _candidate_tables():
    k = PEER_TOPK
    pairs = [(a, b) for a in range(k) for b in range(k) if (a + 1) * (b + 1) <= k]
    rows = -(-len(pairs) // SUBLANES) * SUBLANES
    sel = np.zeros((rows, 2 * k), np.float32)
    sel_id = np.zeros((rows, 2 * k), np.float32)
    bias = np.zeros((rows, 1), np.float32)
    order = np.zeros((rows, 1), np.float32)
    for r, (a, b) in enumerate(pairs):
        sel[r, a] = sel[r, k + b] = 1.0
        sel_id[r, a] = float(PEER_NKEYS)
        sel_id[r, k + b] = 1.0
        order[r, 0] = a * k + b
    for r in range(len(pairs), rows):
        bias[r, 0] = -np.inf
        order[r, 0] = k * k + r
    return sel, sel_id, bias, order


_CAND_SEL, _CAND_SEL_ID, _CAND_BIAS, _CAND_ORDER = _candidate_tables()
_CAND_ROWS = _CAND_SEL.shape[0]


def _extract_max(s, order, big, payload=None):
    m = jnp.max(s, axis=0, keepdims=True)
    pos = jnp.min(jnp.where(s == m, order, big), axis=0, keepdims=True)
    hit = order == pos
    tag = pos if payload is None else jnp.sum(jnp.where(hit, payload, 0.0), axis=0, keepdims=True)
    return m, tag, jnp.where(hit, -jnp.inf, s)


def _peer_keyproj_kernel(k_ref, wq_ref, o_ref):
    o_ref[...] = lax.dot_general(k_ref[...], wq_ref[...], (((1,), (1,)), ((), ())), precision=HIGHEST,
                                 preferred_element_type=F32).astype(o_ref.dtype)


def _peer_keyproj_call(wq, k1, k2):
    d = wq.shape[0]
    half = PEER_QDIM // 2
    keys = jnp.stack([k1, k2], axis=1).reshape(2 * PEER_HEADS, PEER_NKEYS, half)
    return pl.pallas_call(
        _peer_keyproj_kernel,
        grid=(2 * PEER_HEADS,),
        in_specs=[pl.BlockSpec((None, PEER_NKEYS, half), lambda g: (g, 0, 0)),
                  pl.BlockSpec((d, half), lambda g: (0, g))],
        out_specs=pl.BlockSpec((PEER_NKEYS, d), lambda g: (g, 0)),
        out_shape=jax.ShapeDtypeStruct((2 * PEER_HEADS * PEER_NKEYS, d), BF16),
        compiler_params=_cparams("parallel"),
        name="peer_keyproj",
    )(keys, wq)


def _peer_query_kernel(x_ref, nw_ref, sc_ref, sh_ref, wk_ref, sel_ref, selid_ref, cb_ref, co_ref,
                       h_ref, idx_ref, gate_ref, s_scr, v_scr, i_scr, cs_scr, ci_scr, ts_scr, te_scr):
    h = _norm_mod(x_ref[...], nw_ref[...], sc_ref[...], sh_ref[...])
    h_ref[...] = h
    s_scr[...] = lax.dot_general(wk_ref[...], h.astype(BF16), (((1,), (1,)), ((), ())),
                                 preferred_element_type=F32)
    nk = PEER_NKEYS
    k = PEER_TOPK
    key_iota = lax.broadcasted_iota(jnp.int32, (nk, s_scr.shape[1]), 0).astype(F32)
    cand_order = jnp.broadcast_to(co_ref[...], cs_scr.shape)

    for hd in range(PEER_HEADS + 1):
        do_keys, do_cand = hd < PEER_HEADS, hd > 0
        r1 = slice(2 * hd * nk, (2 * hd + 1) * nk)
        r2 = slice((2 * hd + 1) * nk, (2 * hd + 2) * nk)

        def trip(j, c, do_keys=do_keys, do_cand=do_cand, r1=r1, r2=r2):
            if do_keys:
                m1, p1, s1 = _extract_max(s_scr[r1, :], key_iota, float(nk))
                m2, p2, s2 = _extract_max(s_scr[r2, :], key_iota, float(nk))
                s_scr[r1, :] = s1
                s_scr[r2, :] = s2
                v_scr[pl.ds(j, 1), :] = m1
                i_scr[pl.ds(j, 1), :] = p1
                v_scr[pl.ds(j + k, 1), :] = m2
                i_scr[pl.ds(j + k, 1), :] = p2
            if do_cand:
                m, e, s = _extract_max(cs_scr[...], cand_order, float(4 * k * k), payload=ci_scr[...])
                cs_scr[...] = s
                ts_scr[pl.ds(j, 1), :] = m
                te_scr[pl.ds(j, 1), :] = e
            return c

        lax.fori_loop(0, k, trip, 0)
        if do_cand:
            ts = ts_scr[...]
            e = jnp.exp(ts - jnp.max(ts, axis=0, keepdims=True))
            out = (hd - 1) * k
            gate_ref[out:out + k, :] = e / jnp.sum(e, axis=0, keepdims=True)
            idx_ref[out:out + k, :] = te_scr[...].astype(jnp.int32)
        if do_keys:
            cs_scr[...] = (jnp.dot(sel_ref[...], v_scr[...], precision=HIGHEST, preferred_element_type=F32)
                           + cb_ref[...])
            ci_scr[...] = jnp.dot(selid_ref[...], i_scr[...].astype(BF16), preferred_element_type=F32)


def _peer_query_call(x2d, nw, sc, sh, wk, seq, tm=512):
    n, d = x2d.shape
    per_b = seq // tm
    k = PEER_TOPK
    scratch = [pltpu.VMEM((wk.shape[0], tm), F32),
               pltpu.VMEM((2 * k, tm), F32), pltpu.VMEM((2 * k, tm), F32),
               pltpu.VMEM((_CAND_ROWS, tm), F32), pltpu.VMEM((_CAND_ROWS, tm), F32),
               pltpu.VMEM((k, tm), F32), pltpu.VMEM((k, tm), F32)]
    cand = [jnp.asarray(_CAND_SEL), jnp.asarray(_CAND_SEL_ID, dtype=BF16), jnp.asarray(_CAND_BIAS),
            jnp.asarray(_CAND_ORDER)]
    return pl.pallas_call(
        _peer_query_kernel,
        grid=(n // tm,),
        in_specs=[
            pl.BlockSpec((tm, d), lambda i: (i, 0)),
            _resident((1, d)),
            pl.BlockSpec((None, 1, d), lambda i: (i // per_b, 0, 0)),
            pl.BlockSpec((None, 1, d), lambda i: (i // per_b, 0, 0)),
            _resident(wk.shape),
        ] + [_resident(t.shape) for t in cand],
        out_specs=[pl.BlockSpec((tm, d), lambda i: (i, 0)),
                   pl.BlockSpec((PEER_SEL, tm), lambda i: (0, i)),
                   pl.BlockSpec((PEER_SEL, tm), lambda i: (0, i))],
        out_shape=[jax.ShapeDtypeStruct((n, d), F32),
                   jax.ShapeDtypeStruct((PEER_SEL, n), jnp.int32),
                   jax.ShapeDtypeStruct((PEER_SEL, n), F32)],
        scratch_shapes=scratch,
        compiler_params=_cparams("parallel"),
        name="peer_query",
    )(x2d, nw, sc, sh, wk, *cand)


PEER_TOKENS_PER_STEP = 8
PEER_DMA_THREADS = 2
_ROW_SUB = 2 * SUBLANES


def _fold_pair(a, b, shift, keep_a):
    fa = a + pltpu.roll(a, shift, axis=0)
    fb = b + pltpu.roll(b, SUBLANES - shift, axis=0)
    return jnp.where(keep_a, fa, fb)


def _slot_order():
    tiles = [[k] * SUBLANES for k in range(PEER_SEL)]
    for shift, keep in ((4, [j < 4 for j in range(8)]), (2, [(j % 4) >= 2 for j in range(8)]),
                        (1, [(j % 2) == 1 for j in range(8)])):
        tiles = [[tiles[2 * m][j] if keep[j] else tiles[2 * m + 1][j] for j in range(8)]
                 for m in range(len(tiles) // 2)]
    return [s for t in tiles for s in t]


_SLOT_AT = _slot_order()
_POS_OF = [0] * PEER_SEL
for _p, _s in enumerate(_SLOT_AT):
    _POS_OF[_s] = _p


def _peer_gather_kernel(idx0_ref, idxc_ref, idxn_ref, g_ref, h_ref, x_ref, g2_ref, uv_hbm, o_ref,
                        buf0, buf1, wb_scr, sems):
    tb = PEER_TOKENS_PER_STEP
    blk = tb * PEER_SEL
    i = pl.program_id(0)
    bufs = (buf0, buf1)

    def row_copy(e, half, row, t):
        return pltpu.make_async_copy(uv_hbm.at[e], bufs[half].at[row], sems.at[half * tb + t])

    def slot_copy(half, t):
        return pltpu.make_async_copy(uv_hbm.at[pl.ds(0, PEER_SEL)], bufs[half].at[pl.ds(t * PEER_SEL, PEER_SEL)],
                                     sems.at[half * tb + t])

    @pl.when(i == 0)
    def _():
        def issue0(t, c):
            for k in range(PEER_SEL):
                row_copy(idx0_ref[0, t * PEER_SEL + k], 0, t * PEER_SEL + k, t).start()
            return c
        lax.fori_loop(0, tb, issue0, 0)

    sub = lax.broadcasted_iota(jnp.int32, (SUBLANES, LANES), 0)
    keep4 = sub < 4
    keep2 = (sub % 4) >= 2
    keep1 = (sub % 2) == 1
    eye = (lax.broadcasted_iota(jnp.int32, (PEER_SEL, LANES), 0)
           == lax.broadcasted_iota(jnp.int32, (PEER_SEL, LANES), 1))

    for half in range(2):
        nidx_ref, noff = (idxc_ref, blk) if half == 0 else (idxn_ref, 0)
        buf = bufs[half]
        for t in range(tb):
            tt = half * tb + t
            slot_copy(half, t).wait()
            for k in range(PEER_SEL):
                row_copy(nidx_ref[0, noff + t * PEER_SEL + k], 1 - half, t * PEER_SEL + k, t).start(
                    priority=k % PEER_DMA_THREADS)
            base = t * PEER_SEL
            ht = h_ref[tt]
            tiles = [buf[base + k, 0:SUBLANES, :] * ht for k in range(PEER_SEL)]
            tiles = [_fold_pair(tiles[2 * m], tiles[2 * m + 1], 4, keep4) for m in range(PEER_SEL // 2)]
            tiles = [_fold_pair(tiles[2 * m], tiles[2 * m + 1], 2, keep2) for m in range(PEER_SEL // 4)]
            tiles = [_fold_pair(tiles[2 * m], tiles[2 * m + 1], 1, keep1) for m in range(PEER_SEL // 8)]
            part = jnp.concatenate(tiles, axis=0)
            s = jnp.sum(part, axis=-1, keepdims=True)
            act = 0.5 * s * (1.0 + lax.erf(s * (2.0 ** -0.5)))
            grow = jnp.broadcast_to(g_ref[tt:tt + 1, :], (PEER_SEL, LANES))
            gcol = jnp.sum(jnp.where(eye, grow, 0.0), axis=-1, keepdims=True)
            wbase = tt * PEER_SEL
            wb_scr[wbase:wbase + PEER_SEL, :] = jnp.broadcast_to(gcol * act, (PEER_SEL, LANES))
            accs = [jnp.zeros((SUBLANES, LANES), F32) for _ in range(4)]
            for k in range(PEER_SEL):
                p = wbase + _POS_OF[k]
                wk = jnp.broadcast_to(wb_scr[p:p + 1, :], (SUBLANES, LANES))
                accs[k % 4] = accs[k % 4] + wk * buf[base + k, SUBLANES:_ROW_SUB, :]
            y = (accs[0] + accs[1]) + (accs[2] + accs[3])
            o_ref[tt] = x_ref[tt] + g2_ref[...] * y

    @pl.when(i == pl.num_programs(0) - 1)
    def _():
        for t in range(tb):
            slot_copy(0, t).wait()


def _peer_gather_call(idx, gate_p, h3, x3, g2_3, uv, seq):
    n = idx.shape[0]
    tb = PEER_TOKENS_PER_STEP
    ns = n // (2 * tb)
    per_b = seq // (2 * tb)
    idx3 = idx.reshape(ns, 1, 2 * tb * PEER_SEL)
    smem_blk = lambda fn: pl.BlockSpec((None, 1, 2 * tb * PEER_SEL), fn, memory_space=pltpu.SMEM)
    tok3 = pl.BlockSpec((2 * tb, SUBLANES, LANES), lambda i: (i, 0, 0))
    return pl.pallas_call(
        _peer_gather_kernel,
        grid=(ns,),
        in_specs=[
            smem_blk(lambda i: (0, 0, 0)),
            smem_blk(lambda i: (i, 0, 0)),
            smem_blk(lambda i: (jnp.minimum(i + 1, ns - 1), 0, 0)),
            pl.BlockSpec((2 * tb, PEER_SEL), lambda i: (i, 0)),
            tok3, tok3,
            pl.BlockSpec((None, SUBLANES, LANES), lambda i: (i // per_b, 0, 0)),
            pl.BlockSpec(memory_space=pl.ANY),
        ],
        out_specs=tok3,
        out_shape=jax.ShapeDtypeStruct((n, SUBLANES, LANES), F32),
        scratch_shapes=[pltpu.VMEM((tb * PEER_SEL, _ROW_SUB, LANES), F32),
                        pltpu.VMEM((tb * PEER_SEL, _ROW_SUB, LANES), F32),
                        pltpu.VMEM((2 * tb * PEER_SEL, LANES), F32),
                        pltpu.SemaphoreType.DMA((2 * tb,))],
        compiler_params=_cparams("arbitrary"),
        name="peer_gather",
    )(idx3, idx3, idx3, gate_p, h3, x3, g2_3, uv)


def _uv_pack_kernel(u_ref, v_ref, o_ref):
    rows = u_ref.shape[0]
    o_ref[:, 0:SUBLANES, :] = u_ref[...].reshape(rows, SUBLANES, LANES)
    o_ref[:, SUBLANES:_ROW_SUB, :] = v_ref[...].reshape(rows, SUBLANES, LANES)


def _uv_pack_call(u_tabs, v_tabs, layer, rows=1024):
    _, n_exp, d = u_tabs.shape
    assert d == SUBLANES * LANES
    tab = pl.BlockSpec((None, rows, d), lambda i: (layer, i, 0))
    return pl.pallas_call(
        _uv_pack_kernel,
        grid=(n_exp // rows,),
        in_specs=[tab, tab],
        out_specs=pl.BlockSpec((rows, _ROW_SUB, LANES), lambda i: (i, 0, 0)),
        out_shape=jax.ShapeDtypeStruct((n_exp, _ROW_SUB, LANES), F32),
        compiler_params=_cparams("parallel"),
        name="uv_pack",
    )(u_tabs, v_tabs)


def _final_norm_kernel(x_ref, w_ref, o_ref):
    x = x_ref[...]
    o_ref[...] = x * lax.rsqrt(jnp.mean(x * x, axis=-1, keepdims=True) + RMS_EPS) * w_ref[...]


def _final_norm_call(x2d, w, tm=1024):
    n, d = x2d.shape
    return pl.pallas_call(
        _final_norm_kernel,
        grid=(n // tm,),
        in_specs=[pl.BlockSpec((tm, d), lambda i: (i, 0)), _resident((1, d))],
        out_specs=pl.BlockSpec((tm, d), lambda i: (i, 0)),
        out_shape=jax.ShapeDtypeStruct((n, d), F32),
        compiler_params=_cparams("parallel"),
        name="final_norm",
    )(x2d, w)


S5_LANES = S5_GROUPS * S5_STATE


def _s5_disc_kernel(are_ref, aim_ref, ldt_ref, bre_ref, bim_ref, ar_ref, ai_ref, br_ref, bi_ref):
    a_re = are_ref[...]
    a_im = aim_ref[...]
    dt = jnp.exp(ldt_ref[...])
    mag = jnp.exp(a_re * dt)
    ar = mag * jnp.cos(a_im * dt)
    ai = mag * jnp.sin(a_im * dt)
    den = a_re * a_re + a_im * a_im
    cr = ((ar - 1.0) * a_re + ai * a_im) / den
    ci = (ai * a_re - (ar - 1.0) * a_im) / den
    ar_ref[...] = ar
    ai_ref[...] = ai
    br_ref[...] = cr * bre_ref[...] - ci * bim_ref[...]
    bi_ref[...] = cr * bim_ref[...] + ci * bre_ref[...]


def _s5_disc_call(a_re, a_im, log_dt, b_re, b_im):
    rows = S5_LANES
    col = lambda t: t.reshape(rows, 1)
    ldt = jnp.broadcast_to(log_dt[:, None], (S5_GROUPS, S5_STATE))
    out = [jax.ShapeDtypeStruct((rows, 1), F32)] * 2 + [jax.ShapeDtypeStruct((rows, S5_GROUP), F32)] * 2
    return pl.pallas_call(_s5_disc_kernel, out_shape=out, name="s5_disc")(
        col(a_re), col(a_im), col(ldt), b_re.reshape(rows, S5_GROUP), b_im.reshape(rows, S5_GROUP))


def _s5_kernel(u_ref, bcat_ref, ar_ref, ai_ref, ccat_ref, d_ref, gw_ref, gb_ref, o_ref, st_scr, bu_scr, *, steps, bt):
    @pl.when(pl.program_id(0) == 0)
    def _():
        st_scr[...] = jnp.zeros_like(st_scr)

    u = u_ref[...]
    bu_scr[...] = jnp.dot(u.astype(BF16), bcat_ref[...], preferred_element_type=F32)
    a_re = jnp.broadcast_to(ar_ref[...], (bt, S5_LANES))
    a_im = jnp.broadcast_to(ai_ref[...], (bt, S5_LANES))

    def step(t, carry):
        s_re, s_im = carry
        r = pl.multiple_of(t * bt, bt)
        n_re = a_re * s_re - a_im * s_im + bu_scr[pl.ds(r, bt), 0:S5_LANES]
        n_im = a_re * s_im + a_im * s_re + bu_scr[pl.ds(r, bt), S5_LANES:2 * S5_LANES]
        bu_scr[pl.ds(r, bt), 0:S5_LANES] = n_re
        bu_scr[pl.ds(r, bt), S5_LANES:2 * S5_LANES] = n_im
        return n_re, n_im

    s_re, s_im = lax.fori_loop(0, steps, step, (st_scr[:, 0:S5_LANES], st_scr[:, S5_LANES:2 * S5_LANES]))
    st_scr[:, 0:S5_LANES] = s_re
    st_scr[:, S5_LANES:2 * S5_LANES] = s_im
    y = jnp.dot(bu_scr[...].astype(BF16), ccat_ref[...], preferred_element_type=F32) + d_ref[...] * u
    zg = 0.5 * y * (1.0 + lax.erf(y * (2.0 ** -0.5)))
    zz = jnp.dot(zg.astype(BF16), gw_ref[...], preferred_element_type=F32) + gb_ref[...]
    o_ref[...] = zz[:, 0:S5_WIDTH] * jax.nn.sigmoid(zz[:, S5_WIDTH:2 * S5_WIDTH])


def _s5_call(u_tb, bcat, abar_re, abar_im, ccat, d, glu_w, glu_b, bt, steps=64):
    rows = u_tb.shape[0]
    blk = steps * bt
    return pl.pallas_call(
        functools.partial(_s5_kernel, steps=steps, bt=bt),
        grid=(rows // blk,),
        in_specs=[pl.BlockSpec((blk, S5_WIDTH), lambda i: (i, 0)),
                  _resident(bcat.shape), _resident(abar_re.shape), _resident(abar_im.shape), _resident(ccat.shape),
                  _resident(d.shape), _resident(glu_w.shape), _resident(glu_b.shape)],
        out_specs=pl.BlockSpec((blk, S5_WIDTH), lambda i: (i, 0)),
        out_shape=jax.ShapeDtypeStruct((rows, S5_WIDTH), F32),
        scratch_shapes=[pltpu.VMEM((bt, 2 * S5_LANES), F32), pltpu.VMEM((blk, 2 * S5_LANES), F32)],
        compiler_params=_cparams("arbitrary"),
        name="s5_scan",
    )(u_tb, bcat, abar_re, abar_im, ccat, d, glu_w, glu_b)


def _s5_branch_pallas(p_ub, bt, seq, a_re, a_im, b_re, b_im, c_re, c_im, d, log_dt, glu_w, glu_b):
    ar, ai, br, bi = _s5_disc_call(a_re, a_im, log_dt, b_re, b_im)
    g_ix = jnp.arange(S5_GROUPS)
    def blockdiag_in(b):
        b = b.reshape(S5_GROUPS, S5_STATE, S5_GROUP)
        full = jnp.zeros((S5_GROUPS, S5_GROUP, S5_GROUPS, S5_STATE), F32)
        return full.at[g_ix, :, g_ix, :].set(b.transpose(0, 2, 1)).reshape(S5_WIDTH, S5_LANES)

    def blockdiag_out(c):
        full = jnp.zeros((S5_GROUPS, S5_STATE, S5_GROUPS, S5_GROUP), F32)
        return full.at[g_ix, :, g_ix, :].set(c.transpose(0, 2, 1)).reshape(S5_LANES, S5_WIDTH)

    bcat = jnp.concatenate([blockdiag_in(br), blockdiag_in(bi)], axis=1).astype(BF16)
    ccat = jnp.concatenate([blockdiag_out(c_re), -blockdiag_out(c_im)], axis=0).astype(BF16)
    u_tb = p_ub.reshape(bt, seq, S5_WIDTH).transpose(1, 0, 2).reshape(seq * bt, S5_WIDTH)
    o_tb = _s5_call(u_tb, bcat, ar.reshape(1, S5_LANES), ai.reshape(1, S5_LANES), ccat, d.reshape(1, S5_WIDTH),
                    glu_w.astype(BF16), glu_b.reshape(1, 2 * S5_WIDTH), bt)
    return o_tb.reshape(seq, bt, S5_WIDTH).transpose(1, 0, 2).reshape(bt * seq, S5_WIDTH)


def _mm(a, b):
    return jnp.dot(a.astype(BF16), b.astype(BF16), preferred_element_type=F32)


def _mm_nt(a, b, precision=None):
    if precision is None:
        a, b = a.astype(BF16), b.astype(BF16)
    return lax.dot_general(a, b, (((1,), (1,)), ((), ())), precision=precision, preferred_element_type=F32)


def _mm_tn(a, b):
    return lax.dot_general(a.astype(BF16), b.astype(BF16), (((0,), (0,)), ((), ())), preferred_element_type=F32)


def _softplus(x):
    return jnp.maximum(x, 0.0) + jnp.log1p(jnp.exp(-jnp.abs(x)))


def _gdn_kernel(pa_ref, ba_ref, cw_ref, alog_ref, dtb_ref, nw_ref, o_ref,
                st_scr, carry_scr, q_scr, k_scr, v_scr, g_scr, b_scr, u_scr, w_scr, qk_scr, gi_scr):
    j = pl.program_id(1)
    t_blk = pa_ref.shape[0]
    c_len, wid, dh = GDN_CHUNK, GDN_WIDTH, GDN_HEAD_DIM

    @pl.when(j == 0)
    def _():
        st_scr[...] = jnp.zeros_like(st_scr)
        carry_scr[...] = jnp.zeros_like(carry_scr)

    x = pa_ref[:, 0:3 * wid]
    xx = jnp.concatenate([carry_scr[...], x], axis=0)
    conv = cw_ref[GDN_CONV - 1:GDN_CONV, :] * x
    for k in range(1, GDN_CONV):
        conv = conv + cw_ref[GDN_CONV - 1 - k:GDN_CONV - k, :] * pltpu.roll(xx, k, axis=0)[SUBLANES:, :]
    carry_scr[...] = x[t_blk - SUBLANES:t_blk, :]
    act = conv * jax.nn.sigmoid(conv)
    for h in range(GDN_HEADS):
        qh = act[:, h * dh:(h + 1) * dh]
        kh = act[:, wid + h * dh:wid + (h + 1) * dh]
        q_scr[:, h * dh:(h + 1) * dh] = qh * lax.rsqrt(jnp.sum(qh * qh, axis=-1, keepdims=True) + 1e-6) * (dh ** -0.5)
        k_scr[:, h * dh:(h + 1) * dh] = kh * lax.rsqrt(jnp.sum(kh * kh, axis=-1, keepdims=True) + 1e-6)
    v_scr[...] = act[:, 2 * wid:3 * wid]
    ba = ba_ref[...]
    g_scr[...] = -jnp.exp(alog_ref[...]) * _softplus(ba + dtb_ref[...])
    b_scr[...] = jax.nn.sigmoid(ba)

    row = lax.broadcasted_iota(jnp.int32, (t_blk, t_blk), 0)
    col = lax.broadcasted_iota(jnp.int32, (t_blk, t_blk), 1)
    same = (row // c_len) == (col // c_len)
    tril = same & (row >= col)
    strict = same & (row > col)
    eye = jnp.where(row == col, 1.0, 0.0)
    lane = lax.broadcasted_iota(jnp.int32, (t_blk, LANES), 1)
    nw = nw_ref[...]
    gcum = jnp.dot(jnp.where(tril, 1.0, 0.0), g_scr[...], precision=HIGHEST, preferred_element_type=F32)
    beta_all = b_scr[...]
    heads = range(GDN_HEADS)
    hsl = [slice(h * dh, (h + 1) * dh) for h in heads]
    gis = [gcum[:, GDN_HEADS + h:GDN_HEADS + h + 1] for h in heads]
    betas = [beta_all[:, h:h + 1] for h in heads]
    decays, ps, ts = [], [], []
    for h in heads:
        g1 = jnp.where(lane == 0, gis[h], jnp.where(lane == 1, 1.0, 0.0))
        g2 = jnp.where(lane == 0, 1.0, jnp.where(lane == 1, -gis[h], 0.0))
        diff = _mm_nt(g1, g2, precision=HIGHEST)
        decays.append(jnp.where(tril, jnp.exp(jnp.where(tril, diff, 0.0)), 0.0))
    for h in heads:
        k_h = k_scr[:, hsl[h]]
        a = jnp.where(strict, _mm_nt(k_h * betas[h], k_h) * decays[h], 0.0)
        ps.append(-a)
        ts.append(eye - a)
    for _ in range(int(math.log2(c_len)) - 1):
        ps = [_mm(p, p) for p in ps]
        ts = [t + _mm(t, p) for t, p in zip(ts, ps)]
    for h in heads:
        k_h, v_h, q_h = k_scr[:, hsl[h]], v_scr[:, hsl[h]], q_scr[:, hsl[h]]
        e_g = jnp.exp(gis[h])
        u_scr[:, hsl[h]] = _mm(ts[h], v_h * betas[h])
        w_scr[:, hsl[h]] = _mm(ts[h], k_h * betas[h] * e_g)
        qk_scr[h] = jnp.where(tril, _mm_nt(q_h, k_h) * decays[h], 0.0)
        q_scr[:, hsl[h]] = q_h * e_g
        gi_scr[:, h:h + 1] = gis[h]

    for c in range(t_blk // c_len):
        rows = slice(c * c_len, (c + 1) * c_len)
        states = [st_scr[h] for h in heads]
        v_news = [u_scr[rows, hsl[h]] - _mm(w_scr[rows, hsl[h]], states[h]) for h in heads]
        o_state = [_mm(q_scr[rows, hsl[h]], states[h]) for h in heads]
        outs = [o_state[h] + _mm(qk_scr[h, rows, rows], v_news[h]) for h in heads]
        for h in heads:
            gi = gi_scr[rows, h:h + 1]
            g_last = gi[c_len - 1:c_len, :]
            st_scr[h] = states[h] * jnp.exp(g_last) + _mm_tn(k_scr[rows, hsl[h]] * jnp.exp(g_last - gi), v_news[h])
        for h in heads:
            o = outs[h]
            z = pa_ref[rows, 3 * wid + h * dh:3 * wid + (h + 1) * dh]
            o_ref[rows, hsl[h]] = (o * lax.rsqrt(jnp.mean(o * o, axis=-1, keepdims=True) + RMS_EPS) * nw
                                   * (z * jax.nn.sigmoid(z)))


def _gdn_branch_pallas(p_a, p_ba, conv_w, a_log, dt_bias, norm_w, bt, seq, t_blk=256):
    n = p_a.shape[0]
    nj = seq // t_blk
    wid = GDN_WIDTH
    lane_pad = lambda v: jnp.zeros((1, LANES), F32).at[0, GDN_HEADS:2 * GDN_HEADS].set(v)
    rows = lambda wd: pl.BlockSpec((t_blk, wd), lambda b, j: (b * nj + j, 0))
    return pl.pallas_call(
        _gdn_kernel,
        grid=(bt, nj),
        in_specs=[rows(4 * wid), rows(LANES), _resident((GDN_CONV, 3 * wid)), _resident((1, LANES)),
                  _resident((1, LANES)), _resident((1, GDN_HEAD_DIM))],
        out_specs=rows(wid),
        out_shape=jax.ShapeDtypeStruct((n, wid), F32),
        scratch_shapes=[pltpu.VMEM((GDN_HEADS, GDN_HEAD_DIM, GDN_HEAD_DIM), F32), pltpu.VMEM((SUBLANES, 3 * wid), F32),
                        pltpu.VMEM((t_blk, wid), F32), pltpu.VMEM((t_blk, wid), F32), pltpu.VMEM((t_blk, wid), F32),
                        pltpu.VMEM((t_blk, LANES), F32), pltpu.VMEM((t_blk, LANES), F32),
                        pltpu.VMEM((t_blk, wid), F32), pltpu.VMEM((t_blk, wid), F32),
                        pltpu.VMEM((GDN_HEADS, t_blk, t_blk), F32), pltpu.VMEM((t_blk, LANES), F32)],
        compiler_params=_cparams("parallel", "arbitrary"),
        name="gdn_delta",
    )(p_a, p_ba, conv_w, lane_pad(a_log), lane_pad(dt_bias), norm_w.reshape(1, GDN_HEAD_DIM))


def _moba_kernel(qt_ref, k_ref, vt_ref, o_ref, km_scr, sel_scr, qb_scr, *head_scr):
    i = pl.program_id(1)
    nh, nblk = k_ref.shape[0], k_ref.shape[1]
    scale = MOBA_HEAD_DIM ** -0.5
    m_scr, l_scr, acc_scr = head_scr[0:nh], head_scr[nh:2 * nh], head_scr[2 * nh:3 * nh]

    @pl.when(i == 0)
    def _():
        for h in range(nh):
            for n in range(nblk):
                km_scr[h, n:n + 1, :] = jnp.mean(k_ref[h, n], axis=0, keepdims=True)

    blk = lax.broadcasted_iota(jnp.int32, (nblk, MOBA_BLOCK), 0)
    kpos = lax.broadcasted_iota(jnp.int32, (MOBA_BLOCK, MOBA_BLOCK), 0)
    qpos = lax.broadcasted_iota(jnp.int32, (MOBA_BLOCK, MOBA_BLOCK), 1)
    for h in range(nh):
        qt = qt_ref[h]
        gate = jnp.dot(km_scr[h], qt, precision=HIGHEST, preferred_element_type=F32)
        cnt = jnp.zeros(gate.shape, F32)
        for m in range(nblk):
            gm = gate[m:m + 1, :]
            beats = (gm > gate) | ((gm == gate) & (m < blk))
            cnt = cnt + jnp.where(beats & (m < i), 1.0, 0.0)
        sel_scr[h] = jnp.where((blk < i) & (cnt < float(MOBA_TOPK)), 1.0, 0.0)
        qb = qt.astype(BF16)
        qb_scr[h] = qb
        s = jnp.dot(k_ref[h, i].astype(BF16), qb, preferred_element_type=F32) * scale
        s = jnp.where(kpos <= qpos, s, NEG_INF)
        m0 = jnp.max(s, axis=0, keepdims=True)
        p = jnp.exp(s - m0)
        m_scr[h][...] = m0
        l_scr[h][...] = jnp.sum(p, axis=0, keepdims=True)
        acc_scr[h][...] = jnp.dot(vt_ref[h, i].astype(BF16), p.astype(BF16), preferred_element_type=F32)

    def body(n, carry):
        scores = [jnp.dot(k_ref[h, n].astype(BF16), qb_scr[h], preferred_element_type=F32) for h in range(nh)]
        probs, alphas = [], []
        for h in range(nh):
            m = m_scr[h][...]
            s = jnp.where(sel_scr[h, pl.ds(n, 1), :] > 0.5, scores[h] * scale, NEG_INF)
            m_new = jnp.maximum(m, jnp.max(s, axis=0, keepdims=True))
            p = jnp.exp(s - m_new)
            alpha = jnp.exp(m - m_new)
            m_scr[h][...] = m_new
            l_scr[h][...] = alpha * l_scr[h][...] + jnp.sum(p, axis=0, keepdims=True)
            probs.append(p.astype(BF16))
            alphas.append(alpha)
        for h in range(nh):
            acc_scr[h][...] = alphas[h] * acc_scr[h][...] + jnp.dot(vt_ref[h, n].astype(BF16), probs[h],
                                                                    preferred_element_type=F32)
        return carry

    lax.fori_loop(0, i, body, 0)
    for h in range(nh):
        o_ref[h] = acc_scr[h][...] / l_scr[h][...]


def _moba_branch_pallas(p_c, bt, seq):
    nh, dh, bs = MOBA_HEADS, MOBA_HEAD_DIM, MOBA_BLOCK
    assert seq % bs == 0
    nblk = seq // bs
    qkv = p_c.reshape(bt, seq, 3, nh, dh)
    qt = qkv[:, :, 0].transpose(0, 2, 3, 1)
    k5 = qkv[:, :, 1].reshape(bt, nblk, bs, nh, dh).transpose(0, 3, 1, 2, 4)
    vt5 = qkv[:, :, 2].reshape(bt, nblk, bs, nh, dh).transpose(0, 3, 1, 4, 2)
    ot = pl.pallas_call(
        _moba_kernel,
        grid=(bt, nblk),
        in_specs=[pl.BlockSpec((None, nh, dh, bs), lambda b, i: (b, 0, 0, i)),
                  pl.BlockSpec((None, nh, nblk, bs, dh), lambda b, i: (b, 0, 0, 0, 0)),
                  pl.BlockSpec((None, nh, nblk, dh, bs), lambda b, i: (b, 0, 0, 0, 0))],
        out_specs=pl.BlockSpec((None, nh, dh, bs), lambda b, i: (b, 0, 0, i)),
        out_shape=jax.ShapeDtypeStruct((bt, nh, dh, seq), F32),
        scratch_shapes=[pltpu.VMEM((nh, nblk, dh), F32), pltpu.VMEM((nh, nblk, bs), F32),
                        pltpu.VMEM((nh, dh, bs), BF16)]
        + [pltpu.VMEM((1, bs), F32)] * (2 * nh) + [pltpu.VMEM((dh, bs), F32)] * nh,
        compiler_params=_cparams("parallel", "arbitrary"),
        name="moba_attn",
    )(qt, k5, vt5)
    return ot.transpose(0, 3, 1, 2).reshape(bt * seq, nh * dh)


def kernel(x, c, ada_w, ada_b, norm1_w, w_in, gdn_conv_w, gdn_a_log, gdn_dt_bias, gdn_norm_w, s5_a_re, s5_a_im, s5_b_re, s5_b_im, s5_c_re, s5_c_im, s5_d, s5_log_dt, s5_glu_w, s5_glu_b, w_branch_a, w_branch_b, w_branch_c, w_out, norm2_w, peer_wq, peer_k1, peer_k2, peer_u, peer_v, final_norm_w):
    bt, seq, d = x.shape
    n = bt * seq
    depth = ada_w.shape[0]
    x2d = x.reshape(n, d)
    mod = _ada_call(c, ada_w, ada_b)
    gate_perm = np.asarray(_SLOT_AT, np.int32)
    for l in range(depth):
        sh1, sc1, g1, sh2, sc2, g2 = (mod[l, :, j * d:(j + 1) * d].reshape(bt, 1, d) for j in range(6))
        p_a, p_ub, p_c, p_gate, p_ba = _in_proj_call(
            x2d, norm1_w[l].reshape(1, d), sc1, sh1, _arrange_w_in(w_in[l]), seq)
        o_a = _gdn_branch_pallas(p_a, p_ba, gdn_conv_w[l], gdn_a_log[l], gdn_dt_bias[l], gdn_norm_w[l], bt, seq)
        o_b = _s5_branch_pallas(p_ub, bt, seq, s5_a_re[l], s5_a_im[l], s5_b_re[l], s5_b_im[l], s5_c_re[l],
                                s5_c_im[l], s5_d[l], s5_log_dt[l], s5_glu_w[l], s5_glu_b[l])
        o_c = _moba_branch_pallas(p_c, bt, seq)
        x2d = _merge_call(o_a, o_b, o_c, p_gate, x2d, g1, w_branch_a[l].astype(BF16), w_branch_b[l].astype(BF16),
                          w_branch_c[l].astype(BF16), w_out[l].astype(BF16), seq)
        h2, idx_t, gate_t = _peer_query_call(x2d, norm2_w[l].reshape(1, d), sc2, sh2,
                                             _peer_keyproj_call(peer_wq[l], peer_k1[l], peer_k2[l]), seq)
        uv = _uv_pack_call(peer_u, peer_v, l)
        x3 = _peer_gather_call(idx_t.T, gate_t.T[:, gate_perm], h2.reshape(n, SUBLANES, LANES),
                               x2d.reshape(n, SUBLANES, LANES), g2.reshape(bt, SUBLANES, LANES), uv, seq)
        x2d = x3.reshape(n, d)
    return _final_norm_call(x2d, final_norm_w.reshape(1, d)).reshape(bt, seq, d)
```
